```python
import math
import jax
import jax.numpy as jnp
from jax import lax
import numpy as np

D_MODEL = 1024
BATCH = 8
SEQ = 2048
DEPTH = 2
DEC_BATCH = 32
DEC_SEQ = 4
PAST_LEN = 16384
PAGE_SIZE = 128

N_EVEN = (DEPTH + 1) // 2
N_ODD = DEPTH // 2

CONV_WIDTH = D_MODEL // 2
CONV_K = 31
MLA_HEADS = 8
Q_LORA = 256
KV_LORA = 128
NOPE_DIM = 64
ROPE_DIM = 32
V_DIM = 64
MLA_WIDTH = MLA_HEADS * V_DIM
MLA_SCALE = (NOPE_DIM + ROPE_DIM) ** -0.5
ROPE_THETA = 10000.0
Q_BLOCK = 128
EVEN_SPLITS = (CONV_WIDTH, 2 * CONV_WIDTH, 3 * CONV_WIDTH, 3 * CONV_WIDTH + Q_LORA,
               3 * CONV_WIDTH + Q_LORA + KV_LORA + ROPE_DIM)
EVEN_IN = 3 * CONV_WIDTH + Q_LORA + KV_LORA + ROPE_DIM + MLA_WIDTH
EVEN_OUT = CONV_WIDTH + MLA_WIDTH
SSM_INNER = 2 * D_MODEL
SSM_HEADDIM = 64
SSM_HEADS = SSM_INNER // SSM_HEADDIM
SSM_GROUPS = 4
SSM_HPG = SSM_HEADS // SSM_GROUPS
SSM_STATE = 128
SSM_CONV = 4
SSM_CHUNK = 128
XBC_DIM = SSM_INNER + 2 * SSM_GROUPS * SSM_STATE
ODD_IN = SSM_INNER + XBC_DIM + SSM_HEADS
MEM_LEN = 256
X_HEADS = 4
X_HEAD_DIM = D_MODEL // X_HEADS
EPS = 1e-6

kernel_name = 'hybrid_conv_mla_ssd_decoder_step'

F32 = jnp.float32


def rmsnorm(x, w):
    xf = x.astype(F32)
    y = xf * lax.rsqrt(jnp.mean(xf * xf, axis=-1, keepdims=True) + EPS)
    return (y * w.astype(F32)).astype(x.dtype)


def layernorm(x, w, b):
    xf = x.astype(F32)
    xc = xf - jnp.mean(xf, axis=-1, keepdims=True)
    var = jnp.mean(xc * xc, axis=-1, keepdims=True)
    return (xc * lax.rsqrt(var + EPS) * w.astype(F32) + b.astype(F32)).astype(x.dtype)


def rope(x, pos):
    half = ROPE_DIM // 2
    inv = ROPE_THETA ** (-jnp.arange(half, dtype=F32) / half)
    ang = pos.astype(F32)[:, None] * inv[None, :]
    ang = ang.reshape(ang.shape[0], *([1] * (x.ndim - 3)), half)
    cos, sin = jnp.cos(ang), jnp.sin(ang)
    x1, x2 = x[..., :half].astype(F32), x[..., half:].astype(F32)
    return jnp.concatenate([x1 * cos - x2 * sin, x1 * sin + x2 * cos], axis=-1).astype(x.dtype)


def causal_dwconv(x_ext, w, b):
    c = x_ext.shape[-1]
    y = lax.conv_general_dilated(x_ext, w[:, None, :].astype(x_ext.dtype), window_strides=(1,),
                                 padding='VALID', dimension_numbers=('NWC', 'WIO', 'NWC'),
                                 feature_group_count=c)
    return y + b.astype(x_ext.dtype)


def mla_attend(q_lat, q_rope, ckv, krope, q_pos, k_pos):
    s = (jnp.einsum('bqhc,bkc->bhqk', q_lat, ckv)
         + jnp.einsum('bqhr,bkr->bhqk', q_rope, krope)).astype(F32) * MLA_SCALE
    s = jnp.where(k_pos[None, None, None, :] <= q_pos[None, None, :, None], s, -jnp.inf)
    p = jax.nn.softmax(s, axis=-1).astype(ckv.dtype)
    return jnp.einsum('bhqk,bkc->bqhc', p, ckv)


def mla_attend_blocked(q_lat, q_rope, ckv, krope, q_pos, k_pos):
    b, t = q_lat.shape[:2]
    if t > Q_BLOCK and t % Q_BLOCK == 0:
        nb = t // Q_BLOCK

        def to_blocks(a):
            return jnp.moveaxis(a.reshape(b, nb, Q_BLOCK, *a.shape[2:]), 1, 0)

        def one(args):
            ql, qr, qp = args
            return mla_attend(ql, qr, ckv, krope, qp, k_pos)

        o = lax.map(one, (to_blocks(q_lat), to_blocks(q_rope), q_pos.reshape(nb, Q_BLOCK)))
        return jnp.moveaxis(o, 0, 1).reshape(b, t, *o.shape[3:])
    return mla_attend(q_lat, q_rope, ckv, krope, q_pos, k_pos)


def even_mixer(h, pos, conv_hist, past_ckv, past_krope, w_in, conv_w, conv_b, ln_w, ln_b,
               q_norm_w, kv_norm_w, w_uq, w_uk, w_uv, w_out):
    b, t, _ = h.shape
    u = h @ w_in
    a_val, a_gate, a_g, c_q, kv, b_g = jnp.split(u, EVEN_SPLITS, axis=-1)
    glu = a_val * jax.nn.sigmoid(a_gate)
    ext = jnp.concatenate([conv_hist, glu], axis=1)
    ya = jax.nn.silu(layernorm(causal_dwconv(ext, conv_w, conv_b), ln_w, ln_b))
    q = jnp.einsum('btr,rhd->bthd', rmsnorm(c_q, q_norm_w), w_uq)
    q_rope = rope(q[..., NOPE_DIM:], pos)
    q_lat = jnp.einsum('bthn,chn->bthc', q[..., :NOPE_DIM], w_uk)
    ckv = rmsnorm(kv[..., :KV_LORA], kv_norm_w)
    krope = rope(kv[..., KV_LORA:], pos)
    if past_ckv is None:
        k_ckv, k_rope, k_pos = ckv, krope, pos
    else:
        k_ckv = jnp.concatenate([past_ckv, ckv], axis=1)
        k_rope = jnp.concatenate([past_krope, krope], axis=1)
        k_pos = jnp.arange(past_ckv.shape[1] + t)
    o_lat = mla_attend_blocked(q_lat, q_rope, k_ckv, k_rope, pos, k_pos)
    yb = jnp.einsum('bthc,chv->bthv', o_lat, w_uv).reshape(b, t, MLA_WIDTH)
    y = jnp.concatenate([ya * jax.nn.silu(a_g), yb * jax.nn.silu(b_g)], axis=-1) @ w_out
    return y, ext[:, -(CONV_K - 1):], ckv, krope


def ssd_chunk(state, xs, dt, a, bm, cm):
    xf, bf, cf = xs.astype(F32), bm.astype(F32), cm.astype(F32)
    q = xs.shape[1]
    cum = jnp.cumsum(dt * a, axis=1)
    seg = cum[:, :, None] - cum[:, None, :]
    causal = jnp.tril(jnp.ones((q, q), dtype=bool))[None, :, :, None, None]
    decay = jnp.exp(jnp.where(causal, seg, -jnp.inf))
    cb = jnp.einsum('btgn,bsgn->btsg', cf, bf)
    w = cb[..., None] * decay * dt[:, None]
    y = jnp.einsum('btsge,bsgep->btgep', w, xf)
    y = y + jnp.einsum('btgn,bgepn->btgep', cf, state) * jnp.exp(cum)[..., None]
    to_end = jnp.exp(cum[:, -1:] - cum) * dt
    new_state = (state * jnp.exp(cum[:, -1])[..., None, None]
                 + jnp.einsum('bsge,bsgn,bsgep->bgepn', to_end, bf, xf))
    return new_state, y


def ssd(xs, dt, a, bm, cm, state):
    b, t = xs.shape[:2]
    if t > SSM_CHUNK and t % SSM_CHUNK == 0:
        nc = t // SSM_CHUNK

        def chunks(v):
            return jnp.moveaxis(v.reshape(b, nc, SSM_CHUNK, *v.shape[2:]), 1, 0)

        def step(s, inp):
            xc, dc, bc, cc = inp
            return ssd_chunk(s, xc, dc, a, bc, cc)

        state, ys = lax.scan(step, state, (chunks(xs), chunks(dt), chunks(bm), chunks(cm)))
        return jnp.moveaxis(ys, 0, 1).reshape(b, t, *ys.shape[3:]), state
    state, y = ssd_chunk(state, xs, dt, a, bm, cm)
    return y, state


def odd_mixer(h, conv_hist, ssm_state, w_in, conv_w, conv_b, dt_bias, a_log, d_skip, norm_w, w_out):
    b, t, _ = h.shape
    u = h @ w_in
    z, xbc, dt_raw = jnp.split(u, [SSM_INNER, SSM_INNER + XBC_DIM], axis=-1)
    ext = jnp.concatenate([conv_hist, xbc], axis=1)
    xbc = jax.nn.silu(causal_dwconv(ext, conv_w, conv_b))
    xs, bm, cm = jnp.split(xbc, [SSM_INNER, SSM_INNER + SSM_GROUPS * SSM_STATE], axis=-1)
    xs = xs.reshape(b, t, SSM_GROUPS, SSM_HPG, SSM_HEADDIM)
    bm = bm.reshape(b, t, SSM_GROUPS, SSM_STATE)
    cm = cm.reshape(b, t, SSM_GROUPS, SSM_STATE)
    dt = jax.nn.softplus(dt_raw.astype(F32) + dt_bias.astype(F32)).reshape(b, t, SSM_GROUPS, SSM_HPG)
    a = -jnp.exp(a_log.astype(F32)).reshape(SSM_GROUPS, SSM_HPG)
    state0 = ssm_state.astype(F32).reshape(b, SSM_GROUPS, SSM_HPG, SSM_HEADDIM, SSM_STATE)
    y, state = ssd(xs, dt, a, bm, cm, state0)
    y = y + d_skip.astype(F32).reshape(SSM_GROUPS, SSM_HPG)[..., None] * xs.astype(F32)
    y = y.reshape(b, t, SSM_INNER).astype(h.dtype) * jax.nn.silu(z)
    y = rmsnorm(y.reshape(b, t, SSM_GROUPS, -1), norm_w.reshape(SSM_GROUPS, -1)).reshape(b, t, SSM_INNER)
    new_state = state.reshape(b, SSM_HEADS, SSM_HEADDIM, SSM_STATE).astype(ssm_state.dtype)
    return y @ w_out, ext[:, -(SSM_CONV - 1):], new_state


def mem_kv(mem, norm_w, w_k, w_v):
    b, m, _ = mem.shape
    mn = rmsnorm(mem, norm_w)
    return ((mn @ w_k).reshape(b, m, X_HEADS, X_HEAD_DIM),
            (mn @ w_v).reshape(b, m, X_HEADS, X_HEAD_DIM))


def cross_attn(h, mk, mv, w_qg, w_o):
    b, t, _ = h.shape
    q, g = jnp.split(h @ w_qg, 2, axis=-1)
    q = q.reshape(b, t, X_HEADS, X_HEAD_DIM)
    s = jnp.einsum('bthd,bmhd->bhtm', q, mk).astype(F32) * (X_HEAD_DIM ** -0.5)
    p = jax.nn.softmax(s, axis=-1).astype(h.dtype)
    o = jnp.einsum('bhtm,bmhd->bthd', p, mv).reshape(b, t, D_MODEL)
    return (o * jax.nn.silu(g)) @ w_o


def setup_inputs(seed: int = 0) -> dict:
    key = jax.random.key(seed)
    ks = iter(jax.random.split(key, 64))

    def normal(shape, scale=1.0):
        return jax.random.normal(next(ks), shape, F32) * scale

    def gain(shape):
        return 1.0 + normal(shape, 0.02)

    n_pages = PAST_LEN // PAGE_SIZE
    n_used = DEC_BATCH * n_pages
    n_pool = (n_used * 5) // 4
    page_table = jax.random.permutation(next(ks), n_pool)[:n_used].reshape(DEC_BATCH, n_pages).astype(jnp.int32)

    dt0 = jnp.exp(jax.random.uniform(next(ks), (N_ODD, SSM_HEADS), F32, math.log(1e-3), math.log(1e-1)))
    dt_bias = dt0 + jnp.log(-jnp.expm1(-dt0))
    a_log = jnp.log(jax.random.uniform(next(ks), (N_ODD, SSM_HEADS), F32, 1.0, 16.0))

    return {
        'x_prompt': normal((BATCH, SEQ, D_MODEL)),
        'x_sample': normal((DEC_BATCH, DEC_SEQ, D_MODEL)),
        'mem_prompt': normal((BATCH, MEM_LEN, D_MODEL)),
        'cache_ckv': normal((n_pool, PAGE_SIZE, N_EVEN, KV_LORA)),
        'cache_krope': normal((n_pool, PAGE_SIZE, N_EVEN, ROPE_DIM)),
        'page_table': page_table,
        'state_conv_a': normal((N_EVEN, DEC_BATCH, CONV_K - 1, CONV_WIDTH), 0.5),
        'state_conv_c': normal((N_ODD, DEC_BATCH, SSM_CONV - 1, XBC_DIM)),
        'state_ssm': normal((N_ODD, DEC_BATCH, SSM_HEADS, SSM_HEADDIM, SSM_STATE), 0.3),
        'cache_mem_k': normal((DEPTH, DEC_BATCH, MEM_LEN, X_HEADS, X_HEAD_DIM)),
        'cache_mem_v': normal((DEPTH, DEC_BATCH, MEM_LEN, X_HEADS, X_HEAD_DIM)),
        'norm_pre_mix': gain((DEPTH, D_MODEL)),
        'norm_post_mix': gain((DEPTH, D_MODEL)),
        'norm_pre_x': gain((DEPTH, D_MODEL)),
        'norm_post_x': gain((DEPTH, D_MODEL)),
        'norm_mem': gain((DEPTH, D_MODEL)),
        'x_w_qg': normal((DEPTH, D_MODEL, 2 * D_MODEL), D_MODEL ** -0.5),
        'x_w_k': normal((DEPTH, D_MODEL, D_MODEL), D_MODEL ** -0.5),
        'x_w_v': normal((DEPTH, D_MODEL, D_MODEL), D_MODEL ** -0.5),
        'x_w_o': normal((DEPTH, D_MODEL, D_MODEL), D_MODEL ** -0.5),
        'e_w_in': normal((N_EVEN, D_MODEL, EVEN_IN), D_MODEL ** -0.5),
        'e_conv_w': normal((N_EVEN, CONV_K, CONV_WIDTH), CONV_K ** -0.5),
        'e_conv_b': normal((N_EVEN, CONV_WIDTH), 0.02),
        'e_ln_w': gain((N_EVEN, CONV_WIDTH)),
        'e_ln_b': normal((N_EVEN, CONV_WIDTH), 0.02),
        'e_q_norm': gain((N_EVEN, Q_LORA)),
        'e_kv_norm': gain((N_EVEN, KV_LORA)),
        'e_w_uq': normal((N_EVEN, Q_LORA, MLA_HEADS, NOPE_DIM + ROPE_DIM), Q_LORA ** -0.5),
        'e_w_uk': normal((N_EVEN, KV_LORA, MLA_HEADS, NOPE_DIM), KV_LORA ** -0.5),
        'e_w_uv': normal((N_EVEN, KV_LORA, MLA_HEADS, V_DIM), KV_LORA ** -0.5),
        'e_w_out': normal((N_EVEN, EVEN_OUT, D_MODEL), EVEN_OUT ** -0.5),
        'o_w_in': normal((N_ODD, D_MODEL, ODD_IN), D_MODEL ** -0.5),
        'o_conv_w': normal((N_ODD, SSM_CONV, XBC_DIM), SSM_CONV ** -0.5),
        'o_conv_b': normal((N_ODD, XBC_DIM), 0.02),
        'o_dt_bias': dt_bias,
        'o_a_log': a_log,
        'o_d': gain((N_ODD, SSM_HEADS)),
        'o_norm': gain((N_ODD, SSM_INNER)),
        'o_w_out': normal((N_ODD, SSM_INNER, D_MODEL), SSM_INNER ** -0.5),
    }


def reference(x_prompt, x_sample, mem_prompt, cache_ckv, cache_krope, page_table, state_conv_a,
              state_conv_c, state_ssm, cache_mem_k, cache_mem_v, norm_pre_mix, norm_post_mix,
              norm_pre_x, norm_post_x, norm_mem, x_w_qg, x_w_k, x_w_v, x_w_o, e_w_in, e_conv_w,
              e_conv_b, e_ln_w, e_ln_b, e_q_norm, e_kv_norm, e_w_uq, e_w_uk, e_w_uv, e_w_out,
              o_w_in, o_conv_w, o_conv_b, o_dt_bias, o_a_log, o_d, o_norm, o_w_out):
    bp, sp, _ = x_prompt.shape
    bs, ss, _ = x_sample.shape
    past_len = page_table.shape[1] * cache_ckv.shape[1]
    pos_p = jnp.arange(sp)
    pos_s = past_len + jnp.arange(ss)
    xp, xs = x_prompt, x_sample
    ckv_p, ckv_s, kr_p, kr_s, ca_p, ca_s = [], [], [], [], [], []
    cc_p, cc_s, ssm_p, ssm_s, mk_p, mv_p = [], [], [], [], [], []
    for i in range(DEPTH):
        hp = rmsnorm(xp, norm_pre_mix[i])
        hs = rmsnorm(xs, norm_pre_mix[i])
        if i % 2 == 0:
            l = i // 2
            ep = (e_w_in[l], e_conv_w[l], e_conv_b[l], e_ln_w[l], e_ln_b[l], e_q_norm[l],
                  e_kv_norm[l], e_w_uq[l], e_w_uk[l], e_w_uv[l], e_w_out[l])
            yp, a_new, c_new, r_new = even_mixer(
                hp, pos_p, jnp.zeros((bp, CONV_K - 1, CONV_WIDTH), xp.dtype), None, None, *ep)
            ca_p.append(a_new); ckv_p.append(c_new); kr_p.append(r_new)
            past_ckv = cache_ckv[page_table, :, l].reshape(bs, past_len, KV_LORA)
            past_kr = cache_krope[page_table, :, l].reshape(bs, past_len, ROPE_DIM)
            ys, a_new, c_new, r_new = even_mixer(hs, pos_s, state_conv_a[l], past_ckv, past_kr, *ep)
            ca_s.append(a_new); ckv_s.append(c_new); kr_s.append(r_new)
        else:
            l = i // 2
            op = (o_w_in[l], o_conv_w[l], o_conv_b[l], o_dt_bias[l], o_a_log[l], o_d[l],
                  o_norm[l], o_w_out[l])
            yp, c_new, s_new = odd_mixer(
                hp, jnp.zeros((bp, SSM_CONV - 1, XBC_DIM), xp.dtype),
                jnp.zeros((bp, SSM_HEADS, SSM_HEADDIM, SSM_STATE), xp.dtype), *op)
            cc_p.append(c_new); ssm_p.append(s_new)
            ys, c_new, s_new = odd_mixer(hs, state_conv_c[l], state_ssm[l], *op)
            cc_s.append(c_new); ssm_s.append(s_new)
        xp = xp + rmsnorm(yp, norm_post_mix[i])
        xs = xs + rmsnorm(ys, norm_post_mix[i])
        mk, mv = mem_kv(mem_prompt, norm_mem[i], x_w_k[i], x_w_v[i])
        mk_p.append(mk); mv_p.append(mv)
        xp = xp + rmsnorm(cross_attn(rmsnorm(xp, norm_pre_x[i]), mk, mv, x_w_qg[i], x_w_o[i]), norm_post_x[i])
        xs = xs + rmsnorm(cross_attn(rmsnorm(xs, norm_pre_x[i]), cache_mem_k[i], cache_mem_v[i],
                                     x_w_qg[i], x_w_o[i]), norm_post_x[i])
    return (xp, xs,
            jnp.stack(ckv_p, axis=2), jnp.stack(ckv_s, axis=2),
            jnp.stack(kr_p, axis=2), jnp.stack(kr_s, axis=2),
            jnp.stack(ca_p), jnp.stack(ca_s),
            jnp.stack(cc_p), jnp.stack(cc_s),
            jnp.stack(ssm_p), jnp.stack(ssm_s),
            jnp.stack(mk_p), jnp.stack(mv_p))
```

```python
import functools

import jax
import jax.numpy as jnp
from jax import lax
from jax.experimental import pallas as pl
from jax.experimental.pallas import tpu as pltpu

F32 = jnp.float32
BF16 = jnp.bfloat16

D_MODEL = 1024
CONV_WIDTH = 512
CONV_K = 31
MLA_HEADS = 8
Q_LORA = 256
KV_LORA = 128
NOPE_DIM = 64
ROPE_DIM = 32
V_DIM = 64
QK_CAT = KV_LORA + ROPE_DIM
MLA_SCALE = (NOPE_DIM + ROPE_DIM) ** -0.5
ROPE_THETA = 10000.0
SSM_INNER = 2048
SSM_HEADDIM = 64
SSM_HEADS = 32
SSM_GROUPS = 4
SSM_HPG = 8
SSM_STATE = 128
SSM_CONV = 4
SSM_CHUNK = 128
XBC_DIM = SSM_INNER + 2 * SSM_GROUPS * SSM_STATE
GROUP_WIDTH = SSM_INNER // SSM_GROUPS
MEM_LEN = 256
X_HEADS = 4
X_HEAD_DIM = 256
EPS = 1e-6

LANES = 128
SUBLANES = 8
CONV_A_HALO = 32
CONV_C_HALO = 8
VMEM_LIMIT = 56 * 1024 * 1024


def _params(*sem):
    return pltpu.CompilerParams(dimension_semantics=sem, vmem_limit_bytes=VMEM_LIMIT)


def _rms(x, w):
    return x * lax.rsqrt(jnp.mean(x * x, axis=-1, keepdims=True) + EPS) * w


def _silu(x):
    return x * jax.nn.sigmoid(x)


def _dot(a, b):
    return jnp.dot(a.astype(BF16), b.astype(BF16), preferred_element_type=F32)


def _dot_nt(a, b):
    return lax.dot_general(a.astype(BF16), b.astype(BF16), (((1,), (1,)), ((), ())),
                           preferred_element_type=F32)


def _split3(x):
    hi = x.astype(BF16)
    r = x - hi.astype(F32)
    mid = r.astype(BF16)
    lo = (r - mid.astype(F32)).astype(BF16)
    return hi, mid, lo


def _exact_dot(x, e):
    hi, mid, lo = _split3(x)
    return (jnp.dot(hi, e, preferred_element_type=F32) + jnp.dot(mid, e, preferred_element_type=F32)
            + jnp.dot(lo, e, preferred_element_type=F32))


def _exact_dot_left(e, x):
    hi, mid, lo = _split3(x)
    return (jnp.dot(e, hi, preferred_element_type=F32) + jnp.dot(e, mid, preferred_element_type=F32)
            + jnp.dot(e, lo, preferred_element_type=F32))


def _full(shape):
    nd = len(shape)
    return pl.BlockSpec(shape, lambda *_: (0,) * nd)


def _rope_table_kernel(inv_ref, cos_ref, sin_ref, *, base, period):
    row = lax.broadcasted_iota(jnp.int32, cos_ref.shape, 0)
    pos = (base + (row & (period - 1))).astype(F32)
    ang = pos * inv_ref[...]
    cos_ref[...] = jnp.cos(ang)
    sin_ref[...] = jnp.sin(ang)


def rope_table(inv_lanes, rows, base, period):
    assert period & (period - 1) == 0
    return pl.pallas_call(
        functools.partial(_rope_table_kernel, base=base, period=period),
        out_shape=(jax.ShapeDtypeStruct((rows, LANES), F32),) * 2,
        name="rope_table",
    )(inv_lanes)


def _even_in_kernel(x_ref, gpre_ref, wa_ref, wq_ref, wkv_ref, wbg_ref, gq_ref, gkv_ref,
                    wuqn_ref, wuqr_ref, wuqrot_ref, wukt_ref, cos_ref, sin_ref,
                    glu_ref, sga_ref, sgb_ref, qcat_ref, kcat_ref, ckv_ref, kr_ref):
    h = _rms(x_ref[...], gpre_ref[...]).astype(BF16)
    ua = _dot(h, wa_ref[...])
    glu_ref[...] = ua[:, :CONV_WIDTH] * jax.nn.sigmoid(ua[:, CONV_WIDTH:2 * CONV_WIDTH])
    sga_ref[...] = _silu(ua[:, 2 * CONV_WIDTH:])
    sgb_ref[...] = _silu(_dot(h, wbg_ref[...]))
    cos = cos_ref[...]
    sin = sin_ref[...]
    kv = _dot(h, wkv_ref[...])
    ckv = _rms(kv[:, :KV_LORA], gkv_ref[...])
    kr = (kv[:, KV_LORA:KV_LORA + ROPE_DIM] * cos[:, :ROPE_DIM]
          + kv[:, KV_LORA + ROPE_DIM:] * sin[:, :ROPE_DIM])
    ckv_ref[...] = ckv
    kr_ref[...] = kr
    kcat_ref[:, :KV_LORA] = ckv.astype(BF16)
    kcat_ref[:, KV_LORA:] = kr.astype(BF16)
    qn = _rms(_dot(h, wq_ref[...]), gq_ref[...]).astype(BF16)
    qnope = _dot(qn, wuqn_ref[...])
    cos2 = jnp.concatenate([cos, cos], axis=1)
    sin2 = jnp.concatenate([sin, sin], axis=1)
    qr = (_dot(qn, wuqr_ref[...]) * cos2 + _dot(qn, wuqrot_ref[...]) * sin2) * MLA_SCALE
    for hd in range(MLA_HEADS):
        ql = _dot(qnope[:, hd * NOPE_DIM:(hd + 1) * NOPE_DIM], wukt_ref[hd]) * MLA_SCALE
        qcat_ref[hd, :, :KV_LORA] = ql.astype(BF16)
        qcat_ref[hd, :, KV_LORA:] = qr[:, hd * ROPE_DIM:(hd + 1) * ROPE_DIM].astype(BF16)


def even_in(x, wts, cos, sin, tm):
    n = x.shape[0]
    nblk = cos.shape[0] // tm
    row = lambda d: pl.BlockSpec((tm, d), lambda i: (i, 0))
    tab = pl.BlockSpec((tm, LANES), lambda i: (i % nblk, 0))
    weights = (wts["gpre"], wts["wa"], wts["wq"], wts["wkv"], wts["wbg"], wts["gq"], wts["gkv"],
               wts["wuqn"], wts["wuqr"], wts["wuqrot"], wts["wukt"])
    return pl.pallas_call(
        _even_in_kernel,
        grid=(n // tm,),
        in_specs=[row(D_MODEL)] + [_full(w.shape) for w in weights] + [tab, tab],
        out_specs=(row(CONV_WIDTH), row(CONV_WIDTH), row(CONV_WIDTH),
                   pl.BlockSpec((MLA_HEADS, tm, QK_CAT), lambda i: (0, i, 0)),
                   row(QK_CAT), row(KV_LORA), row(ROPE_DIM)),
        out_shape=(jax.ShapeDtypeStruct((n, CONV_WIDTH), F32),) * 3 + (
            jax.ShapeDtypeStruct((MLA_HEADS, n, QK_CAT), BF16),
            jax.ShapeDtypeStruct((n, QK_CAT), BF16),
            jax.ShapeDtypeStruct((n, KV_LORA), F32),
            jax.ShapeDtypeStruct((n, ROPE_DIM), F32)),
        compiler_params=_params("parallel"),
        name="even_in",
    )(x, *weights, cos, sin)


def _conv_kernel(hist_ref, prev_ref, cur_ref, w_ref, b_ref, *rest, taps, halo, row_chunk, lane_chunk,
                 layernorm):
    if layernorm:
        lnw_ref, lnb_ref, gate_ref, out_ref, ext_scr = rest
    else:
        out_ref, ext_scr = rest
    tt, width = cur_ref.shape[1], cur_ref.shape[2]
    ext_scr[0:halo] = jnp.where(pl.program_id(1) == 0, hist_ref[0], prev_ref[0])
    ext_scr[halo:halo + tt] = cur_ref[0]
    off = halo - (taps - 1)
    for r0 in range(0, tt, row_chunk):
        for c0 in range(0, width, lane_chunk):
            cs = slice(c0, c0 + lane_chunk)
            acc = w_ref[0:1, cs] * ext_scr[off + r0:off + r0 + row_chunk, cs]
            for k in range(1, taps):
                acc = acc + w_ref[k:k + 1, cs] * ext_scr[off + r0 + k:off + r0 + k + row_chunk, cs]
            acc = acc + b_ref[:, cs]
            if layernorm:
                xc = acc - jnp.mean(acc, axis=-1, keepdims=True)
                var = jnp.mean(xc * xc, axis=-1, keepdims=True)
                ln = xc * lax.rsqrt(var + EPS) * lnw_ref[...] + lnb_ref[...]
                out_ref[0, r0:r0 + row_chunk, :] = _silu(ln) * gate_ref[0, r0:r0 + row_chunk, :]
            else:
                out_ref[0, r0:r0 + row_chunk, cs] = _silu(acc)


def causal_conv(x, hist, w, b, tt, halo, lane_chunk, ln=None):
    bsz, t, width = x.shape
    taps = w.shape[0]
    seq = pl.BlockSpec((1, tt, width), lambda bi, ti: (bi, ti, 0))
    first = pl.BlockSpec((1, halo, width), lambda bi, ti: (bi, 0, 0))
    if t == tt:
        prev, prev_spec = hist, first
    else:
        ratio = tt // halo
        prev = x
        prev_spec = pl.BlockSpec((1, halo, width), lambda bi, ti: (bi, jnp.maximum(ti * ratio - 1, 0), 0))
    in_specs = [first, prev_spec, seq, _full(w.shape), _full(b.shape)]
    args = [hist, prev, x, w, b]
    if ln is not None:
        in_specs += [_full(ln[0].shape), _full(ln[1].shape), seq]
        args += list(ln)
    return pl.pallas_call(
        functools.partial(_conv_kernel, taps=taps, halo=halo, row_chunk=min(32, tt),
                          lane_chunk=lane_chunk, layernorm=ln is not None),
        grid=(bsz, t // tt),
        in_specs=in_specs,
        out_specs=seq,
        out_shape=jax.ShapeDtypeStruct((bsz, t, width), F32),
        scratch_shapes=[pltpu.VMEM((halo + tt, width), F32)],
        compiler_params=_params("parallel", "arbitrary"),
        name="causal_conv_ln" if ln is not None else "causal_conv",
    )(*args)


def _mla_prompt_kernel(qi_ref, ki_ref, q_ref, k_ref, o_ref, m_scr, l_scr, acc_scr, *, tq):
    p_id = pl.program_id(1)
    qi = qi_ref[p_id]
    ki = ki_ref[p_id]
    rows = MLA_HEADS * tq

    @pl.when(ki == 0)
    def _():
        m_scr[...] = jnp.full(m_scr.shape, -jnp.inf, F32)
        l_scr[...] = jnp.zeros(l_scr.shape, F32)
        acc_scr[...] = jnp.zeros(acc_scr.shape, F32)

    def step(masked):
        q = q_ref[...].reshape(rows, QK_CAT)
        k = k_ref[0]
        s = _dot_nt(q, k)
        if masked:
            qpos = lax.broadcasted_iota(jnp.int32, s.shape, 0) & (tq - 1)
            kpos = lax.broadcasted_iota(jnp.int32, s.shape, 1)
            s = jnp.where(kpos <= qpos, s, -jnp.inf)
        m_prev = m_scr[...]
        m_new = jnp.maximum(m_prev, jnp.max(s, axis=1, keepdims=True))
        alpha = jnp.exp(m_prev - m_new)
        p = jnp.exp(s - m_new[:, 0:1])
        l_scr[...] = alpha * l_scr[...] + jnp.sum(p, axis=1, keepdims=True)
        m_scr[...] = m_new
        acc_scr[...] = alpha * acc_scr[...] + _dot(p, k[:, :KV_LORA])

    @pl.when(ki < qi)
    def _():
        step(False)

    @pl.when(ki == qi)
    def _():
        step(True)
        o = acc_scr[...] / l_scr[...]
        for hd in range(MLA_HEADS):
            o_ref[:, hd * KV_LORA:(hd + 1) * KV_LORA] = o[hd * tq:(hd + 1) * tq, :]


def mla_prompt(qcat, kcat, bsz, t, tq):
    nq = t // tq
    pairs = [(i, j) for i in range(nq) for j in range(i + 1)]
    qi = jnp.asarray([p[0] for p in pairs], jnp.int32)
    ki = jnp.asarray([p[1] for p in pairs], jnp.int32)
    rows = MLA_HEADS * tq
    grid_spec = pltpu.PrefetchScalarGridSpec(
        num_scalar_prefetch=2,
        grid=(bsz, len(pairs)),
        in_specs=[pl.BlockSpec((MLA_HEADS, tq, QK_CAT), lambda b, p, qi, ki: (0, b * nq + qi[p], 0)),
                  pl.BlockSpec((1, tq, QK_CAT), lambda b, p, qi, ki: (b, ki[p], 0))],
        out_specs=pl.BlockSpec((tq, MLA_HEADS * KV_LORA), lambda b, p, qi, ki: (b * nq + qi[p], 0)),
        scratch_shapes=[pltpu.VMEM((rows, LANES), F32), pltpu.VMEM((rows, LANES), F32),
                        pltpu.VMEM((rows, KV_LORA), F32)])
    return pl.pallas_call(
        functools.partial(_mla_prompt_kernel, tq=tq),
        grid_spec=grid_spec,
        out_shape=jax.ShapeDtypeStruct((bsz * t, MLA_HEADS * KV_LORA), F32),
        compiler_params=_params("parallel", "arbitrary"),
        name="mla_prompt",
    )(qi, ki, qcat, kcat)


def _mla_sample_kernel(pt_ref, q_ref, knew_ref, *rest, pages, page, tpad, tvalid):
    ckv_refs = rest[:pages]
    kr_refs = rest[pages:2 * pages]
    o_ref, kc_scr, m_scr, l_scr, acc_scr = rest[2 * pages:]
    j = pl.program_id(1)
    rows = MLA_HEADS * tpad

    @pl.when(j == 0)
    def _():
        m_scr[...] = jnp.full(m_scr.shape, -jnp.inf, F32)
        l_scr[...] = jnp.zeros(l_scr.shape, F32)
        acc_scr[...] = jnp.zeros(acc_scr.shape, F32)

    q = q_ref[...].reshape(rows, QK_CAT)

    def update(s, v):
        m_prev = m_scr[...]
        m_new = jnp.maximum(m_prev, jnp.max(s, axis=1, keepdims=True))
        alpha = jnp.exp(m_prev - m_new)
        p = jnp.exp(s - m_new[:, 0:1])
        l_scr[...] = alpha * l_scr[...] + jnp.sum(p, axis=1, keepdims=True)
        m_scr[...] = m_new
        acc_scr[...] = alpha * acc_scr[...] + _dot(p, v)

    for i in range(pages):
        kc_scr[i * page:(i + 1) * page, :KV_LORA] = ckv_refs[i][...].astype(BF16)
        kc_scr[i * page:(i + 1) * page, KV_LORA:] = kr_refs[i][...].astype(BF16)
    kc = kc_scr[...]
    update(_dot_nt(q, kc), kc[:, :KV_LORA])

    @pl.when(j == pl.num_programs(1) - 1)
    def _():
        knew = knew_ref[0]
        s = _dot_nt(q, knew)
        qpos = lax.broadcasted_iota(jnp.int32, s.shape, 0) & (tpad - 1)
        kpos = lax.broadcasted_iota(jnp.int32, s.shape, 1)
        s = jnp.where((kpos <= qpos) & (kpos < tvalid), s, -jnp.inf)
        update(s, knew[:, :KV_LORA])
        o = acc_scr[...] / l_scr[...]
        for hd in range(MLA_HEADS):
            o_ref[:, hd * KV_LORA:(hd + 1) * KV_LORA] = o[hd * tpad:(hd + 1) * tpad, :]


def mla_sample(qcat, knew, cache_ckv, cache_krope, page_table, layer, tpad, tvalid, pages):
    bsz, n_pages = page_table.shape
    page = cache_ckv.shape[1]
    steps = n_pages // pages
    rows = MLA_HEADS * tpad

    def page_spec(i, width):
        return pl.BlockSpec((None, page, None, width),
                            lambda b, j, pt: (pt[b, j * pages + i], 0, layer, 0))

    grid_spec = pltpu.PrefetchScalarGridSpec(
        num_scalar_prefetch=1,
        grid=(bsz, steps),
        in_specs=[pl.BlockSpec((MLA_HEADS, tpad, QK_CAT), lambda b, j, pt: (0, b, 0)),
                  pl.BlockSpec((1, knew.shape[1], QK_CAT), lambda b, j, pt: (b, 0, 0))]
        + [page_spec(i, KV_LORA) for i in range(pages)]
        + [page_spec(i, ROPE_DIM) for i in range(pages)],
        out_specs=pl.BlockSpec((tpad, MLA_HEADS * KV_LORA), lambda b, j, pt: (b, 0)),
        scratch_shapes=[pltpu.VMEM((pages * page, QK_CAT), BF16),
                        pltpu.VMEM((rows, LANES), F32), pltpu.VMEM((rows, LANES), F32),
                        pltpu.VMEM((rows, KV_LORA), F32)])
    return pl.pallas_call(
        functools.partial(_mla_sample_kernel, pages=pages, page=page, tpad=tpad, tvalid=tvalid),
        grid_spec=grid_spec,
        out_shape=jax.ShapeDtypeStruct((bsz * tpad, MLA_HEADS * KV_LORA), F32),
        compiler_params=_params("parallel", "arbitrary"),
        name="mla_sample",
    )(page_table, qcat, knew, *([cache_ckv] * pages), *([cache_krope] * pages))


def _even_out_kernel(x_ref, ya_ref, olat_ref, sgb_ref, wbd_ref, woa_ref, wob_ref, gpost_ref, o_ref):
    yb = _dot(olat_ref[...], wbd_ref[...]) * sgb_ref[...]
    y = _dot(ya_ref[...], woa_ref[...]) + _dot(yb, wob_ref[...])
    o_ref[...] = x_ref[...] + _rms(y, gpost_ref[...])


def even_out(x, ya, olat, sgb, wts, tm):
    n = x.shape[0]
    row = lambda d: pl.BlockSpec((tm, d), lambda i: (i, 0))
    weights = (wts["wbd"], wts["woa"], wts["wob"], wts["gpost"])
    return pl.pallas_call(
        _even_out_kernel,
        grid=(n // tm,),
        in_specs=[row(D_MODEL), row(CONV_WIDTH), row(MLA_HEADS * KV_LORA), row(CONV_WIDTH)]
        + [_full(w.shape) for w in weights],
        out_specs=row(D_MODEL),
        out_shape=jax.ShapeDtypeStruct((n, D_MODEL), F32),
        compiler_params=_params("parallel"),
        name="even_out",
    )(x, ya, olat, sgb, *weights)


def _mem_kv_kernel(x_ref, g_ref, wk_ref, wv_ref, k_ref, v_ref):
    h = _rms(x_ref[...], g_ref[...]).astype(BF16)
    k_ref[...] = _dot(h, wk_ref[...])
    v_ref[...] = _dot(h, wv_ref[...])


def mem_kv(mem, g, wk, wv, tm):
    n = mem.shape[0]
    row = pl.BlockSpec((tm, D_MODEL), lambda i: (i, 0))
    return pl.pallas_call(
        _mem_kv_kernel,
        grid=(n // tm,),
        in_specs=[row, _full(g.shape), _full(wk.shape), _full(wv.shape)],
        out_specs=(row, row),
        out_shape=(jax.ShapeDtypeStruct((n, D_MODEL), F32),) * 2,
        compiler_params=_params("parallel"),
        name="mem_kv",
    )(mem, g, wk, wv)


def _xattn_kernel(x_ref, mk_ref, mv_ref, gpre_ref, wqg_ref, wo_ref, gpost_ref, o_ref, att_scr, *, bb, tq):
    x = x_ref[...].reshape(bb * tq, D_MODEL)
    qg = _dot(_rms(x, gpre_ref[...]), wqg_ref[...])
    q = qg[:, :D_MODEL] * (X_HEAD_DIM ** -0.5)
    for b in range(bb):
        for hd in range(X_HEADS):
            cs = slice(hd * X_HEAD_DIM, (hd + 1) * X_HEAD_DIM)
            s = _dot_nt(q[b * tq:(b + 1) * tq, cs], mk_ref[b, :, cs])
            p = jnp.exp(s - jnp.max(s, axis=1, keepdims=True))
            o = _dot(p, mv_ref[b, :, cs]) / jnp.sum(p, axis=1, keepdims=True)
            att_scr[b * tq:(b + 1) * tq, cs] = o
    y = _dot(att_scr[...] * _silu(qg[:, D_MODEL:]), wo_ref[...])
    o_ref[...] = (x + _rms(y, gpost_ref[...])).reshape(bb, tq, D_MODEL)


def xattn(x, mk, mv, wts, bb, tq):
    bsz, t, _ = x.shape
    seq = pl.BlockSpec((bb, tq, D_MODEL), lambda b, i: (b, i, 0))
    mem = pl.BlockSpec((bb, MEM_LEN, D_MODEL), lambda b, i: (b, 0, 0))
    weights = (wts["gprex"], wts["wqg"], wts["wo"], wts["gpostx"])
    return pl.pallas_call(
        functools.partial(_xattn_kernel, bb=bb, tq=tq),
        grid=(bsz // bb, t // tq),
        in_specs=[seq, mem, mem] + [_full(w.shape) for w in weights],
        out_specs=seq,
        out_shape=jax.ShapeDtypeStruct(x.shape, F32),
        scratch_shapes=[pltpu.VMEM((bb * tq, D_MODEL), F32)],
        compiler_params=_params("parallel", "arbitrary"),
        name="xattn",
    )(x, mk, mv, *weights)


def _odd_in_kernel(x_ref, gpre_ref, wz_ref, wxbc_ref, wdt_ref, wdtt_ref, brow_ref, bcol_ref,
                   sz_ref, xbc_ref, dt_ref, dtt_ref):
    h = _rms(x_ref[...], gpre_ref[...]).astype(BF16)
    sz_ref[...] = _silu(_dot(h, wz_ref[...]))
    xbc_ref[...] = _dot(h, wxbc_ref[...])
    dt_ref[...] = jax.nn.softplus(_dot(h, wdt_ref[...]) + brow_ref[...])
    dtt_ref[...] = jax.nn.softplus(_dot_nt(wdtt_ref[...], h) + bcol_ref[...])


def odd_in(x, wts, tm):
    n = x.shape[0]
    row = lambda d: pl.BlockSpec((tm, d), lambda i: (i, 0))
    weights = (wts["gpre"], wts["wz"], wts["wxbc"], wts["wdt"], wts["wdtt"], wts["brow"], wts["bcol"])
    return pl.pallas_call(
        _odd_in_kernel,
        grid=(n // tm,),
        in_specs=[row(D_MODEL)] + [_full(w.shape) for w in weights],
        out_specs=(row(SSM_INNER), row(XBC_DIM), row(SSM_HEADS),
                   pl.BlockSpec((SSM_HEADS, tm), lambda i: (0, i))),
        out_shape=(jax.ShapeDtypeStruct((n, SSM_INNER), F32), jax.ShapeDtypeStruct((n, XBC_DIM), F32),
                   jax.ShapeDtypeStruct((n, SSM_HEADS), F32), jax.ShapeDtypeStruct((SSM_HEADS, n), F32)),
        compiler_params=_params("parallel"),
        name="odd_in",
    )(x, *weights)


def _ssd_kernel(*refs, q, t_in, tvalid, has_init):
    if has_init:
        (xs_ref, bm_ref, cm_ref, dt_ref, dtt_ref, arow_ref, acol_ref, drow_ref, e64_ref, tri_ref,
         trit_ref, init_ref, y_ref, st_ref, state_scr) = refs
    else:
        (xs_ref, bm_ref, cm_ref, dt_ref, dtt_ref, arow_ref, acol_ref, drow_ref, e64_ref, tri_ref,
         trit_ref, y_ref, st_ref, state_scr) = refs
    c = pl.program_id(1)

    @pl.when(c == 0)
    def _():
        if has_init:
            state_scr[...] = init_ref[0].T
        else:
            state_scr[...] = jnp.zeros(state_scr.shape, F32)

    def rows(ref):
        v = ref[0]
        if t_in < q:
            v = jnp.concatenate([v, jnp.zeros((q - t_in, v.shape[1]), F32)], axis=0)
        return v

    xs, bm, cm = rows(xs_ref), rows(bm_ref), rows(cm_ref)
    dt = dt_ref[0]
    dtt = dtt_ref[...]
    if tvalid < q:
        dt = jnp.where(lax.broadcasted_iota(jnp.int32, dt.shape, 0) < tvalid, dt, 0.0)
        dtt = jnp.where(lax.broadcasted_iota(jnp.int32, dtt.shape, 1) < tvalid, dtt, 0.0)
    cum = _exact_dot_left(tri_ref[...], dt * arow_ref[...])
    cumt = _exact_dot(dtt * acol_ref[...], trit_ref[...])
    cum_last = cum[q - 1:q, :]
    causal = (lax.broadcasted_iota(jnp.int32, (q, q), 1) <= lax.broadcasted_iota(jnp.int32, (q, q), 0))
    lane = lax.broadcasted_iota(jnp.int32, (1, LANES), 1)
    left = lane < SSM_HEADDIM

    state = state_scr[...]
    y_parts = []
    for g in range(SSM_GROUPS):
        gs = slice(g * SSM_STATE, (g + 1) * SSM_STATE)
        cm_g = cm[:, gs]
        cb = _dot_nt(cm_g, bm[:, gs])
        for pr in range(SSM_HPG // 2):
            pair = g * (SSM_HPG // 2) + pr
            ps = slice(pair * LANES, (pair + 1) * LANES)
            rhs = jnp.concatenate([xs[:, ps], state[:, ps]], axis=0)
            y_pair = None
            for side in range(2):
                hd = 2 * pair + side
                ccol = jnp.broadcast_to(cum[:, hd:hd + 1], (q, q))
                seg = ccol - cumt[hd:hd + 1, :]
                w = cb * jnp.exp(jnp.where(causal, seg, -jnp.inf)) * dtt[hd:hd + 1, :]
                lhs = jnp.concatenate([w, cm_g * jnp.exp(ccol)], axis=1)
                keep = left if side == 0 else jnp.logical_not(left)
                part = _dot(lhs, jnp.where(keep, rhs, 0.0))
                y_pair = part if y_pair is None else y_pair + part
            y_parts.append(y_pair)
    y = jnp.concatenate(y_parts, axis=1) + drow_ref[...] * xs
    y_ref[0] = y[:t_in] if t_in < q else y

    to_end = jnp.exp(cum_last - cum) * dt
    toxs = _exact_dot(to_end, e64_ref[...]) * xs
    dec = _exact_dot(jnp.broadcast_to(jnp.exp(cum_last), (SUBLANES, SSM_HEADS)), e64_ref[...])[0:1, :]
    new_parts = []
    for g in range(SSM_GROUPS):
        gs = slice(g * SSM_STATE, (g + 1) * SSM_STATE)
        hs = slice(g * GROUP_WIDTH, (g + 1) * GROUP_WIDTH)
        new_parts.append(_dot(bm[:, gs].T, toxs[:, hs]))
    new_state = state * dec + jnp.concatenate(new_parts, axis=1)
    state_scr[...] = new_state

    @pl.when(c == pl.num_programs(1) - 1)
    def _():
        st_ref[0] = new_state.T


def ssd(xbc, dt, dtt, wts, init, q, t_in, tvalid):
    bsz, t, _ = xbc.shape
    nc = max(t // q, 1)
    nxb = SSM_INNER // (SSM_GROUPS * SSM_STATE)
    in_specs = [pl.BlockSpec((1, t_in, SSM_INNER), lambda b, c: (b, c, 0)),
                pl.BlockSpec((1, t_in, SSM_GROUPS * SSM_STATE), lambda b, c: (b, c, nxb)),
                pl.BlockSpec((1, t_in, SSM_GROUPS * SSM_STATE), lambda b, c: (b, c, nxb + 1)),
                pl.BlockSpec((1, q, SSM_HEADS), lambda b, c: (b, c, 0)),
                pl.BlockSpec((SSM_HEADS, q), lambda b, c: (0, b * nc + c))]
    consts = (wts["arow"], wts["acol"], wts["drow"], wts["e64"], wts["tri"], wts["trit"])
    in_specs += [_full(w.shape) for w in consts]
    args = [xbc, xbc, xbc, dt, dtt, *consts]
    if init is not None:
        in_specs.append(pl.BlockSpec((1, SSM_INNER, SSM_STATE), lambda b, c: (b, 0, 0)))
        args.append(init)
    return pl.pallas_call(
        functools.partial(_ssd_kernel, q=q, t_in=t_in, tvalid=tvalid, has_init=init is not None),
        grid=(bsz, nc),
        in_specs=in_specs,
        out_specs=(pl.BlockSpec((1, t_in, SSM_INNER), lambda b, c: (b, c, 0)),
                   pl.BlockSpec((1, SSM_INNER, SSM_STATE), lambda b, c: (b, 0, 0))),
        out_shape=(jax.ShapeDtypeStruct((bsz, t, SSM_INNER), F32),
                   jax.ShapeDtypeStruct((bsz, SSM_INNER, SSM_STATE), F32)),
        scratch_shapes=[pltpu.VMEM((SSM_STATE, SSM_INNER), F32)],
        compiler_params=_params("parallel", "arbitrary"),
        name="ssd",
    )(*args)


def _odd_out_kernel(x_ref, y_ref, sz_ref, gn_ref, wout_ref, gpost_ref, o_ref):
    v = y_ref[...] * sz_ref[...]
    gn = gn_ref[...]
    parts = []
    for g in range(SSM_GROUPS):
        gs = slice(g * GROUP_WIDTH, (g + 1) * GROUP_WIDTH)
        parts.append(_rms(v[:, gs], gn[:, gs]).astype(BF16))
    y = _dot(jnp.concatenate(parts, axis=1), wout_ref[...])
    o_ref[...] = x_ref[...] + _rms(y, gpost_ref[...])


def odd_out(x, y, sz, wts, tm):
    n = x.shape[0]
    row = lambda d: pl.BlockSpec((tm, d), lambda i: (i, 0))
    weights = (wts["gnorm"], wts["wout"], wts["gpost"])
    return pl.pallas_call(
        _odd_out_kernel,
        grid=(n // tm,),
        in_specs=[row(D_MODEL), row(SSM_INNER), row(SSM_INNER)] + [_full(w.shape) for w in weights],
        out_specs=row(D_MODEL),
        out_shape=jax.ShapeDtypeStruct((n, D_MODEL), F32),
        compiler_params=_params("parallel"),
        name="odd_out",
    )(x, y, sz, *weights)


def _even_weights(i, l, p):
    w_in = p["e_w_in"][l]
    c = CONV_WIDTH
    o_q, o_kv, o_bg = 3 * c, 3 * c + Q_LORA, 3 * c + Q_LORA + KV_LORA + ROPE_DIM
    half = ROPE_DIM // 2
    w_kr = w_in[:, o_kv + KV_LORA:o_bg]
    w_uq = p["e_w_uq"][l]
    uq_r = w_uq[:, :, NOPE_DIM:]
    w_uv = jnp.transpose(p["e_w_uv"][l], (1, 0, 2))
    eye = jnp.eye(MLA_HEADS, dtype=F32)
    wbd = (eye[:, None, :, None] * w_uv[:, :, None, :]).reshape(MLA_HEADS * KV_LORA, MLA_HEADS * V_DIM)
    w_out = p["e_w_out"][l]
    return {
        "gpre": p["norm_pre_mix"][i][None], "gpost": p["norm_post_mix"][i][None],
        "wa": w_in[:, :o_q].astype(BF16), "wq": w_in[:, o_q:o_kv].astype(BF16),
        "wkv": jnp.concatenate([w_in[:, o_kv:o_bg], -w_kr[:, half:], w_kr[:, :half]], axis=1).astype(BF16),
        "wbg": w_in[:, o_bg:].astype(BF16),
        "gq": p["e_q_norm"][l][None], "gkv": p["e_kv_norm"][l][None],
        "wuqn": w_uq[:, :, :NOPE_DIM].reshape(Q_LORA, MLA_HEADS * NOPE_DIM).astype(BF16),
        "wuqr": uq_r.reshape(Q_LORA, MLA_HEADS * ROPE_DIM).astype(BF16),
        "wuqrot": jnp.concatenate([-uq_r[:, :, half:], uq_r[:, :, :half]], axis=2)
        .reshape(Q_LORA, MLA_HEADS * ROPE_DIM).astype(BF16),
        "wukt": jnp.transpose(p["e_w_uk"][l], (1, 2, 0)).astype(BF16),
        "conv_w": p["e_conv_w"][l], "conv_b": p["e_conv_b"][l][None],
        "ln_w": p["e_ln_w"][l][None], "ln_b": p["e_ln_b"][l][None],
        "wbd": wbd.astype(BF16), "woa": w_out[:c].astype(BF16), "wob": w_out[c:].astype(BF16),
    }


def _odd_weights(i, l, p):
    w_in = p["o_w_in"][l]
    w_dt = w_in[:, SSM_INNER + XBC_DIM:]
    a = -jnp.exp(p["o_a_log"][l])
    q = SSM_CHUNK
    tri = jnp.tril(jnp.ones((q, q), F32))
    return {
        "gpre": p["norm_pre_mix"][i][None], "gpost": p["norm_post_mix"][i][None],
        "wz": w_in[:, :SSM_INNER].astype(BF16),
        "wxbc": w_in[:, SSM_INNER:SSM_INNER + XBC_DIM].astype(BF16),
        "wdt": w_dt.astype(BF16), "wdtt": w_dt.T.astype(BF16),
        "brow": p["o_dt_bias"][l][None], "bcol": p["o_dt_bias"][l][:, None],
        "conv_w": p["o_conv_w"][l], "conv_b": p["o_conv_b"][l][None],
        "arow": a[None], "acol": a[:, None],
        "drow": jnp.repeat(p["o_d"][l], SSM_HEADDIM)[None],
        "e64": jnp.repeat(jnp.eye(SSM_HEADS, dtype=F32), SSM_HEADDIM, axis=1).astype(BF16),
        "tri": tri.astype(BF16), "trit": tri.T.astype(BF16),
        "gnorm": p["o_norm"][l][None], "wout": p["o_w_out"][l].astype(BF16),
    }


def _xattn_weights(i, p):
    return {"gprex": p["norm_pre_x"][i][None], "gpostx": p["norm_post_x"][i][None],
            "wqg": p["x_w_qg"][i].astype(BF16), "wo": p["x_w_o"][i].astype(BF16)}


def kernel(x_prompt, x_sample, mem_prompt, cache_ckv, cache_krope, page_table, state_conv_a,
           state_conv_c, state_ssm, cache_mem_k, cache_mem_v, norm_pre_mix, norm_post_mix,
           norm_pre_x, norm_post_x, norm_mem, x_w_qg, x_w_k, x_w_v, x_w_o, e_w_in, e_conv_w,
           e_conv_b, e_ln_w, e_ln_b, e_q_norm, e_kv_norm, e_w_uq, e_w_uk, e_w_uv, e_w_out,
           o_w_in, o_conv_w, o_conv_b, o_dt_bias, o_a_log, o_d, o_norm, o_w_out):
    p = dict(norm_pre_mix=norm_pre_mix, norm_post_mix=norm_post_mix, norm_pre_x=norm_pre_x,
             norm_post_x=norm_post_x, e_w_in=e_w_in, e_conv_w=e_conv_w, e_conv_b=e_conv_b,
             e_ln_w=e_ln_w, e_ln_b=e_ln_b, e_q_norm=e_q_norm, e_kv_norm=e_kv_norm, e_w_uq=e_w_uq,
             e_w_uk=e_w_uk, e_w_uv=e_w_uv, e_w_out=e_w_out, o_w_in=o_w_in, o_conv_w=o_conv_w,
             o_conv_b=o_conv_b, o_dt_bias=o_dt_bias, o_a_log=o_a_log, o_d=o_d, o_norm=o_norm,
             o_w_out=o_w_out, x_w_qg=x_w_qg, x_w_o=x_w_o)
    bp, sp, _ = x_prompt.shape
    bs, ss, _ = x_sample.shape
    depth = norm_pre_mix.shape[0]
    n_pages, page = page_table.shape[1], cache_ckv.shape[1]
    past_len = n_pages * page
    tpad = SUBLANES
    assert ss <= tpad and sp % SSM_CHUNK == 0
    np_, ns = bp * sp, bs * tpad
    tm_p, tm_s = 512, ns

    half = ROPE_DIM // 2
    inv = ROPE_THETA ** (-jnp.arange(half, dtype=F32) / half)
    inv_lanes = jnp.tile(inv, LANES // half)[None]
    cos_p, sin_p = rope_table(inv_lanes, sp, 0, sp)
    cos_s, sin_s = rope_table(inv_lanes, ns, past_len, tpad)

    xp = x_prompt.reshape(np_, D_MODEL)
    xs = jnp.pad(x_sample, ((0, 0), (0, tpad - ss), (0, 0))).reshape(ns, D_MODEL)
    mem_flat = mem_prompt.reshape(bp * MEM_LEN, D_MODEL)

    outs = {k: [] for k in ("ckv_p", "ckv_s", "kr_p", "kr_s", "ca_p", "ca_s", "cc_p", "cc_s",
                            "ssm_p", "ssm_s", "mk", "mv")}
    for i in range(depth):
        l = i // 2
        if i % 2 == 0:
            w = _even_weights(i, l, p)
            glu, sga, sgb, qcat, kcat, ckv, kr = even_in(xp, w, cos_p, sin_p, tm_p)
            glu3 = glu.reshape(bp, sp, CONV_WIDTH)
            ya = causal_conv(glu3, jnp.zeros((bp, CONV_A_HALO, CONV_WIDTH), F32), w["conv_w"], w["conv_b"],
                             256, CONV_A_HALO, CONV_WIDTH,
                             ln=(w["ln_w"], w["ln_b"], sga.reshape(bp, sp, CONV_WIDTH)))
            olat = mla_prompt(qcat, kcat.reshape(bp, sp, QK_CAT), bp, sp, 256)
            yp = even_out(xp, ya.reshape(np_, CONV_WIDTH), olat, sgb, w, tm_p)
            outs["ca_p"].append(glu3[:, sp - (CONV_K - 1):])
            outs["ckv_p"].append(ckv.reshape(bp, sp, KV_LORA))
            outs["kr_p"].append(kr.reshape(bp, sp, ROPE_DIM))
            glu, sga, sgb, qcat, kcat, ckv, kr = even_in(xs, w, cos_s, sin_s, tm_s)
            glu3 = glu.reshape(bs, tpad, CONV_WIDTH)
            hist = state_conv_a[l]
            hist_pad = jnp.pad(hist, ((0, 0), (CONV_A_HALO - (CONV_K - 1), 0), (0, 0)))
            ya = causal_conv(glu3, hist_pad, w["conv_w"], w["conv_b"], tpad, CONV_A_HALO, CONV_WIDTH,
                             ln=(w["ln_w"], w["ln_b"], sga.reshape(bs, tpad, CONV_WIDTH)))
            knew = jnp.pad(kcat.reshape(bs, tpad, QK_CAT), ((0, 0), (0, LANES - tpad), (0, 0)))
            olat = mla_sample(qcat.astype(F32), knew, cache_ckv, cache_krope, page_table, l, tpad, ss, 16)
            ys = even_out(xs, ya.reshape(ns, CONV_WIDTH), olat, sgb, w, tm_s)
            outs["ca_s"].append(jnp.concatenate([hist[:, ss:], glu3[:, :ss]], axis=1))
            outs["ckv_s"].append(ckv.reshape(bs, tpad, KV_LORA)[:, :ss])
            outs["kr_s"].append(kr.reshape(bs, tpad, ROPE_DIM)[:, :ss])
        else:
            w = _odd_weights(i, l, p)
            q = SSM_CHUNK
            sz, xbc, dt, dtt = odd_in(xp, w, 256)
            xbc3 = xbc.reshape(bp, sp, XBC_DIM)
            xbc_c = causal_conv(xbc3, jnp.zeros((bp, CONV_C_HALO, XBC_DIM), F32), w["conv_w"], w["conv_b"],
                                256, CONV_C_HALO, 512)
            y, st = ssd(xbc_c, dt.reshape(bp, sp, SSM_HEADS), dtt, w, None, q, q, q)
            yp = odd_out(xp, y.reshape(np_, SSM_INNER), sz, w, tm_p)
            outs["cc_p"].append(xbc3[:, sp - (SSM_CONV - 1):])
            outs["ssm_p"].append(st.reshape(bp, SSM_HEADS, SSM_HEADDIM, SSM_STATE))
            sz, xbc, dt, dtt = odd_in(xs, w, tm_s)
            xbc3 = xbc.reshape(bs, tpad, XBC_DIM)
            hist = state_conv_c[l]
            hist_pad = jnp.pad(hist, ((0, 0), (CONV_C_HALO - (SSM_CONV - 1), 0), (0, 0)))
            xbc_c = causal_conv(xbc3, hist_pad, w["conv_w"], w["conv_b"], tpad, CONV_C_HALO, 512)
            dt_pad = jnp.pad(dt.reshape(bs, tpad, SSM_HEADS), ((0, 0), (0, q - tpad), (0, 0)))
            dtt_pad = jnp.pad(dtt.reshape(SSM_HEADS, bs, tpad), ((0, 0), (0, 0), (0, q - tpad)))
            y, st = ssd(xbc_c, dt_pad, dtt_pad.reshape(SSM_HEADS, bs * q), w,
                        state_ssm[l].reshape(bs, SSM_INNER, SSM_STATE), q, tpad, ss)
            ys = odd_out(xs, y.reshape(ns, SSM_INNER), sz, w, tm_s)
            outs["cc_s"].append(jnp.concatenate([hist[:, ss:], xbc3[:, :ss]], axis=1)[:, -(SSM_CONV - 1):])
            outs["ssm_s"].append(st.reshape(bs, SSM_HEADS, SSM_HEADDIM, SSM_STATE))
        xp, xs = yp, ys
        wx = _xattn_weights(i, p)
        mk, mv = mem_kv(mem_flat, norm_mem[i][None], x_w_k[i].astype(BF16), x_w_v[i].astype(BF16), 512)
        mk3, mv3 = mk.reshape(bp, MEM_LEN, D_MODEL), mv.reshape(bp, MEM_LEN, D_MODEL)
        outs["mk"].append(mk3.reshape(bp, MEM_LEN, X_HEADS, X_HEAD_DIM))
        outs["mv"].append(mv3.reshape(bp, MEM_LEN, X_HEADS, X_HEAD_DIM))
        xp = xattn(xp.reshape(bp, sp, D_MODEL), mk3, mv3, wx, 1, 512).reshape(np_, D_MODEL)
        xs = xattn(xs.reshape(bs, tpad, D_MODEL), cache_mem_k[i].reshape(bs, MEM_LEN, D_MODEL),
                   cache_mem_v[i].reshape(bs, MEM_LEN, D_MODEL), wx, 4, tpad).reshape(ns, D_MODEL)

    return (xp.reshape(bp, sp, D_MODEL), xs.reshape(bs, tpad, D_MODEL)[:, :ss],
            jnp.stack(outs["ckv_p"], axis=2), jnp.stack(outs["ckv_s"], axis=2),
            jnp.stack(outs["kr_p"], axis=2), jnp.stack(outs["kr_s"], axis=2),
            jnp.stack(outs["ca_p"]), jnp.stack(outs["ca_s"]),
            jnp.stack(outs["cc_p"]), jnp.stack(outs["cc_s"]),
            jnp.stack(outs["ssm_p"]), jnp.stack(outs["ssm_s"]),
            jnp.stack(outs["mk"]), jnp.stack(outs["mv"]))
```

```python
import functools

import jax
import jax.numpy as jnp
from jax import lax
from jax.experimental import pallas as pl
from jax.experimental.pallas import tpu as pltpu

F32 = jnp.float32
BF16 = jnp.bfloat16

D_MODEL = 1024
CONV_WIDTH = 512
CONV_K = 31
MLA_HEADS = 8
Q_LORA = 256
KV_LORA = 128
NOPE_DIM = 64
ROPE_DIM = 32
V_DIM = 64
QK_CAT = KV_LORA + ROPE_DIM
LOG2E = 1.4426950408889634
MLA_SCALE = (NOPE_DIM + ROPE_DIM) ** -0.5 * LOG2E
ROPE_THETA = 10000.0
SSM_INNER = 2048
SSM_HEADDIM = 64
SSM_HEADS = 32
SSM_GROUPS = 4
SSM_HPG = 8
SSM_STATE = 128
SSM_CONV = 4
SSM_CHUNK = 128
XBC_DIM = SSM_INNER + 2 * SSM_GROUPS * SSM_STATE
GROUP_WIDTH = SSM_INNER // SSM_GROUPS
MEM_LEN = 256
X_HEADS = 4
X_HEAD_DIM = 256
EPS = 1e-6

LANES = 128
SUBLANES = 8
CONV_A_HALO = 32
CONV_C_HALO = 8
VMEM_LIMIT = 56 * 1024 * 1024


def _params(*sem):
    return pltpu.CompilerParams(dimension_semantics=sem, vmem_limit_bytes=VMEM_LIMIT)


def _rms(x, w):
    return x * lax.rsqrt(jnp.mean(x * x, axis=-1, keepdims=True) + EPS) * w


def _silu(x):
    return x * jax.nn.sigmoid(x)


def _dot(a, b):
    return jnp.dot(a.astype(BF16), b.astype(BF16), preferred_element_type=F32)


def _dot_nt(a, b):
    return lax.dot_general(a.astype(BF16), b.astype(BF16), (((1,), (1,)), ((), ())),
                           preferred_element_type=F32)


def _split3(x):
    hi = x.astype(BF16)
    r = x - hi.astype(F32)
    mid = r.astype(BF16)
    lo = (r - mid.astype(F32)).astype(BF16)
    return hi, mid, lo


def _exact_dot(x, e):
    hi, mid, lo = _split3(x)
    return (jnp.dot(hi, e, preferred_element_type=F32) + jnp.dot(mid, e, preferred_element_type=F32)
            + jnp.dot(lo, e, preferred_element_type=F32))


def _exact_dot_left(e, x):
    hi, mid, lo = _split3(x)
    return (jnp.dot(e, hi, preferred_element_type=F32) + jnp.dot(e, mid, preferred_element_type=F32)
            + jnp.dot(e, lo, preferred_element_type=F32))


def _full(shape):
    nd = len(shape)
    return pl.BlockSpec(shape, lambda *_: (0,) * nd)


def _rope_table_kernel(inv_ref, cos_ref, sin_ref, *, base, period):
    row = lax.broadcasted_iota(jnp.int32, cos_ref.shape, 0)
    pos = (base + (row & (period - 1))).astype(F32)
    ang = pos * inv_ref[...]
    cos_ref[...] = jnp.cos(ang)
    sin_ref[...] = jnp.sin(ang)


def rope_table(inv_lanes, rows, base, period):
    assert period & (period - 1) == 0
    return pl.pallas_call(
        functools.partial(_rope_table_kernel, base=base, period=period),
        out_shape=(jax.ShapeDtypeStruct((rows, LANES), F32),) * 2,
        name="rope_table",
    )(inv_lanes)


def _even_in_kernel(x_ref, gpre_ref, wa_ref, wq_ref, wkv_ref, wbg_ref, gq_ref, gkv_ref,
                    wuqn_ref, wuqr_ref, wuqrot_ref, wukt_ref, cos_ref, sin_ref,
                    glu_ref, sga_ref, sgb_ref, qcat_ref, kcat_ref, ckv_ref, kr_ref):
    h = _rms(x_ref[...], gpre_ref[...]).astype(BF16)
    ua = _dot(h, wa_ref[...])
    glu_ref[...] = ua[:, :CONV_WIDTH] * jax.nn.sigmoid(ua[:, CONV_WIDTH:2 * CONV_WIDTH])
    sga_ref[...] = _silu(ua[:, 2 * CONV_WIDTH:])
    sgb_ref[...] = _silu(_dot(h, wbg_ref[...]))
    cos = cos_ref[...]
    sin = sin_ref[...]
    kv = _dot(h, wkv_ref[...])
    ckv = _rms(kv[:, :KV_LORA], gkv_ref[...])
    kr = (kv[:, KV_LORA:KV_LORA + ROPE_DIM] * cos[:, :ROPE_DIM]
          + kv[:, KV_LORA + ROPE_DIM:] * sin[:, :ROPE_DIM])
    ckv_ref[...] = ckv
    kr_ref[...] = kr
    kcat_ref[:, :KV_LORA] = ckv.astype(BF16)
    kcat_ref[:, KV_LORA:] = kr.astype(BF16)
    qn = _rms(_dot(h, wq_ref[...]), gq_ref[...]).astype(BF16)
    qnope = _dot(qn, wuqn_ref[...])
    cos2 = jnp.concatenate([cos, cos], axis=1)
    sin2 = jnp.concatenate([sin, sin], axis=1)
    qr = (_dot(qn, wuqr_ref[...]) * cos2 + _dot(qn, wuqrot_ref[...]) * sin2) * MLA_SCALE
    for hd in range(MLA_HEADS):
        ql = _dot(qnope[:, hd * NOPE_DIM:(hd + 1) * NOPE_DIM], wukt_ref[hd]) * MLA_SCALE
        qcat_ref[hd, :, :KV_LORA] = ql.astype(BF16)
        qcat_ref[hd, :, KV_LORA:] = qr[:, hd * ROPE_DIM:(hd + 1) * ROPE_DIM].astype(BF16)


def even_in(x, wts, cos, sin, tm):
    n = x.shape[0]
    nblk = cos.shape[0] // tm
    row = lambda d: pl.BlockSpec((tm, d), lambda i: (i, 0))
    tab = pl.BlockSpec((tm, LANES), lambda i: (i % nblk, 0))
    weights = (wts["gpre"], wts["wa"], wts["wq"], wts["wkv"], wts["wbg"], wts["gq"], wts["gkv"],
               wts["wuqn"], wts["wuqr"], wts["wuqrot"], wts["wukt"])
    return pl.pallas_call(
        _even_in_kernel,
        grid=(n // tm,),
        in_specs=[row(D_MODEL)] + [_full(w.shape) for w in weights] + [tab, tab],
        out_specs=(row(CONV_WIDTH), row(CONV_WIDTH), row(CONV_WIDTH),
                   pl.BlockSpec((MLA_HEADS, tm, QK_CAT), lambda i: (0, i, 0)),
                   row(QK_CAT), row(KV_LORA), row(ROPE_DIM)),
        out_shape=(jax.ShapeDtypeStruct((n, CONV_WIDTH), F32),) * 3 + (
            jax.ShapeDtypeStruct((MLA_HEADS, n, QK_CAT), BF16),
            jax.ShapeDtypeStruct((n, QK_CAT), BF16),
            jax.ShapeDtypeStruct((n, KV_LORA), F32),
            jax.ShapeDtypeStruct((n, ROPE_DIM), F32)),
        compiler_params=_params("parallel"),
        name="even_in",
    )(x, *weights, cos, sin)


def _conv_kernel(hist_ref, prev_ref, cur_ref, w_ref, b_ref, *rest, taps, halo, row_chunk, lane_chunk,
                 layernorm):
    if layernorm:
        lnw_ref, lnb_ref, gate_ref, out_ref, ext_scr, *shift_scr = rest
    else:
        out_ref, ext_scr, *shift_scr = rest
    tt, width = cur_ref.shape[1], cur_ref.shape[2]
    ext_scr[0:halo] = jnp.where(pl.program_id(1) == 0, hist_ref[0], prev_ref[0])
    ext_scr[halo:halo + tt] = cur_ref[0]
    off = halo - (taps - 1)
    if shift_scr:
        sh_scr, = shift_scr
        span = sh_scr.shape[1]
        for r in range(1, SUBLANES):
            sh_scr[r - 1] = ext_scr[r:r + span]

    def ext_rows(d, r0, cs):
        a, r = divmod(d, SUBLANES)
        if not shift_scr or r == 0:
            return ext_scr[d + r0:d + r0 + row_chunk, cs]
        return sh_scr[r - 1, SUBLANES * a + r0:SUBLANES * a + r0 + row_chunk, cs]

    for r0 in range(0, tt, row_chunk):
        for c0 in range(0, width, lane_chunk):
            cs = slice(c0, c0 + lane_chunk)
            acc = w_ref[0:1, cs] * ext_rows(off, r0, cs)
            for k in range(1, taps):
                acc = acc + w_ref[k:k + 1, cs] * ext_rows(off + k, r0, cs)
            acc = acc + b_ref[:, cs]
            if layernorm:
                xc = acc - jnp.mean(acc, axis=-1, keepdims=True)
                var = jnp.mean(xc * xc, axis=-1, keepdims=True)
                ln = xc * lax.rsqrt(var + EPS) * lnw_ref[...] + lnb_ref[...]
                out_ref[0, r0:r0 + row_chunk, :] = _silu(ln) * gate_ref[0, r0:r0 + row_chunk, :]
            else:
                out_ref[0, r0:r0 + row_chunk, cs] = _silu(acc)


def causal_conv(x, hist, w, b, tt, halo, lane_chunk, ln=None):
    bsz, t, width = x.shape
    taps = w.shape[0]
    seq = pl.BlockSpec((1, tt, width), lambda bi, ti: (bi, ti, 0))
    first = pl.BlockSpec((1, halo, width), lambda bi, ti: (bi, 0, 0))
    if t == tt:
        prev, prev_spec = hist, first
    else:
        ratio = tt // halo
        prev = x
        prev_spec = pl.BlockSpec((1, halo, width), lambda bi, ti: (bi, jnp.maximum(ti * ratio - 1, 0), 0))
    in_specs = [first, prev_spec, seq, _full(w.shape), _full(b.shape)]
    args = [hist, prev, x, w, b]
    if ln is not None:
        in_specs += [_full(ln[0].shape), _full(ln[1].shape), seq]
        args += list(ln)
    return pl.pallas_call(
        functools.partial(_conv_kernel, taps=taps, halo=halo, row_chunk=min(32, tt),
                          lane_chunk=lane_chunk, layernorm=ln is not None),
        grid=(bsz, t // tt),
        in_specs=in_specs,
        out_specs=seq,
        out_shape=jax.ShapeDtypeStruct((bsz, t, width), F32),
        scratch_shapes=[pltpu.VMEM((halo + tt, width), F32)]
        + ([pltpu.VMEM((SUBLANES - 1, halo + tt - SUBLANES, width), F32)] if taps > SUBLANES else []),
        compiler_params=_params("parallel", "arbitrary"),
        name="causal_conv_ln" if ln is not None else "causal_conv",
    )(*args)


def _mla_prompt_kernel(q_ref, k_ref, o_ref, s_scr, acc_scr, mx_scr, sum_scr, *, tq):
    i = pl.program_id(1)
    rows = MLA_HEADS * tq
    q = q_ref[...].reshape(rows, QK_CAT)

    def keys(j):
        return k_ref[0, pl.ds(pl.multiple_of(j * tq, tq), tq), :]

    def fold(x):
        parts = [x[:, c:c + LANES] for c in range(0, tq, LANES)]
        return parts

    mx_scr[...] = jnp.full(mx_scr.shape, -jnp.inf, F32)

    def pass1(j, carry):
        s = _dot_nt(q, keys(j))
        s_scr[j] = s
        mx_scr[...] = functools.reduce(jnp.maximum, fold(s), mx_scr[...])
        return carry

    lax.fori_loop(0, i, pass1, 0)
    s = _dot_nt(q, keys(i))
    qpos = lax.broadcasted_iota(jnp.int32, s.shape, 0) & (tq - 1)
    kpos = lax.broadcasted_iota(jnp.int32, s.shape, 1)
    s = jnp.where(kpos <= qpos, s, -jnp.inf)
    s_scr[i] = s
    m = jnp.max(functools.reduce(jnp.maximum, fold(s), mx_scr[...]), axis=1, keepdims=True)

    acc_scr[...] = jnp.zeros(acc_scr.shape, F32)
    sum_scr[...] = jnp.zeros(sum_scr.shape, F32)

    def pass2(j, carry):
        p = jnp.exp2(s_scr[j] - m)
        acc_scr[...] += _dot(p, keys(j)[:, :KV_LORA])
        sum_scr[...] = functools.reduce(jnp.add, fold(p), sum_scr[...])
        return carry

    lax.fori_loop(0, i + 1, pass2, 0)
    o = acc_scr[...] / jnp.sum(sum_scr[...], axis=1, keepdims=True)
    for hd in range(MLA_HEADS):
        o_ref[:, hd * KV_LORA:(hd + 1) * KV_LORA] = o[hd * tq:(hd + 1) * tq, :]


def mla_prompt(qcat, kcat, bsz, t, tq):
    nq = t // tq
    rows = MLA_HEADS * tq
    return pl.pallas_call(
        functools.partial(_mla_prompt_kernel, tq=tq),
        grid=(bsz, nq),
        in_specs=[pl.BlockSpec((MLA_HEADS, tq, QK_CAT), lambda b, i: (0, b * nq + i, 0)),
                  pl.BlockSpec((1, t, QK_CAT), lambda b, i: (b, 0, 0))],
        out_specs=pl.BlockSpec((tq, MLA_HEADS * KV_LORA), lambda b, i: (b * nq + i, 0)),
        out_shape=jax.ShapeDtypeStruct((bsz * t, MLA_HEADS * KV_LORA), F32),
        scratch_shapes=[pltpu.VMEM((nq, rows, tq), F32), pltpu.VMEM((rows, KV_LORA), F32),
                        pltpu.VMEM((rows, LANES), F32), pltpu.VMEM((rows, LANES), F32)],
        compiler_params=_params("parallel", "arbitrary"),
        name="mla_prompt",
    )(qcat, kcat)


def _mla_sample_kernel(pt_ref, q_ref, knew_ref, *rest, pages, page, tpad, tvalid):
    ckv_refs = rest[:pages]
    krt_refs = rest[pages:2 * pages]
    o_ref, kc_scr, krt_scr, m_scr, l_scr, acc_scr = rest[2 * pages:]
    j = pl.program_id(1)
    rows = MLA_HEADS * tpad

    @pl.when(j == 0)
    def _():
        m_scr[...] = jnp.full(m_scr.shape, -jnp.inf, F32)
        l_scr[...] = jnp.zeros(l_scr.shape, F32)
        acc_scr[...] = jnp.zeros(acc_scr.shape, F32)

    q = q_ref[...].reshape(rows, QK_CAT)

    def update(s, v):
        m_prev = m_scr[...]
        m_new = jnp.maximum(m_prev, jnp.max(s, axis=1, keepdims=True))
        alpha = jnp.exp2(m_prev - m_new)
        p = jnp.exp2(s - m_new[:, 0:1])
        l_scr[...] = alpha * l_scr[...] + jnp.sum(p, axis=1, keepdims=True)
        m_scr[...] = m_new
        acc_scr[...] = alpha * acc_scr[...] + _dot(p, v)

    for i in range(pages):
        kc_scr[i * page:(i + 1) * page, :] = ckv_refs[i][...].astype(BF16)
        krt_scr[:, i * page:(i + 1) * page] = krt_refs[i][...].astype(BF16)
    kc = kc_scr[...]
    update(_dot_nt(q[:, :KV_LORA], kc) + _dot(q[:, KV_LORA:], krt_scr[...]), kc)

    @pl.when(j == pl.num_programs(1) - 1)
    def _():
        knew = knew_ref[0]
        s = _dot_nt(q, knew)
        qpos = lax.broadcasted_iota(jnp.int32, s.shape, 0) & (tpad - 1)
        kpos = lax.broadcasted_iota(jnp.int32, s.shape, 1)
        s = jnp.where((kpos <= qpos) & (kpos < tvalid), s, -jnp.inf)
        update(s, knew[:, :KV_LORA])
        o = acc_scr[...] / l_scr[...]
        for hd in range(MLA_HEADS):
            o_ref[:, hd * KV_LORA:(hd + 1) * KV_LORA] = o[hd * tpad:(hd + 1) * tpad, :]


def mla_sample(qcat, knew, ckv_pages, krt_pages, page_table, tpad, tvalid, pages):
    bsz, n_pages = page_table.shape
    page = ckv_pages.shape[1]
    steps = n_pages // pages
    rows = MLA_HEADS * tpad

    def page_spec(i, shape):
        return pl.BlockSpec((None,) + shape, lambda b, j, pt: (pt[b, j * pages + i], 0, 0))

    grid_spec = pltpu.PrefetchScalarGridSpec(
        num_scalar_prefetch=1,
        grid=(bsz, steps),
        in_specs=[pl.BlockSpec((MLA_HEADS, tpad, QK_CAT), lambda b, j, pt: (0, b, 0)),
                  pl.BlockSpec((1, knew.shape[1], QK_CAT), lambda b, j, pt: (b, 0, 0))]
        + [page_spec(i, (page, KV_LORA)) for i in range(pages)]
        + [page_spec(i, (ROPE_DIM, page)) for i in range(pages)],
        out_specs=pl.BlockSpec((tpad, MLA_HEADS * KV_LORA), lambda b, j, pt: (b, 0)),
        scratch_shapes=[pltpu.VMEM((pages * page, KV_LORA), BF16),
                        pltpu.VMEM((ROPE_DIM, pages * page), BF16),
                        pltpu.VMEM((rows, LANES), F32), pltpu.VMEM((rows, LANES), F32),
                        pltpu.VMEM((rows, KV_LORA), F32)])
    return pl.pallas_call(
        functools.partial(_mla_sample_kernel, pages=pages, page=page, tpad=tpad, tvalid=tvalid),
        grid_spec=grid_spec,
        out_shape=jax.ShapeDtypeStruct((bsz * tpad, MLA_HEADS * KV_LORA), F32),
        compiler_params=_params("parallel", "arbitrary"),
        name="mla_sample",
    )(page_table, qcat, knew, *([ckv_pages] * pages), *([krt_pages] * pages))


def _even_out_kernel(x_ref, ya_ref, olat_ref, sgb_ref, wbd_ref, woa_ref, wob_ref, gpost_ref, o_ref):
    yb = _dot(olat_ref[...], wbd_ref[...]) * sgb_ref[...]
    y = _dot(ya_ref[...], woa_ref[...]) + _dot(yb, wob_ref[...])
    o_ref[...] = x_ref[...] + _rms(y, gpost_ref[...])


def even_out(x, ya, olat, sgb, wts, tm):
    n = x.shape[0]
    row = lambda d: pl.BlockSpec((tm, d), lambda i: (i, 0))
    weights = (wts["wbd"], wts["woa"], wts["wob"], wts["gpost"])
    return pl.pallas_call(
        _even_out_kernel,
        grid=(n // tm,),
        in_specs=[row(D_MODEL), row(CONV_WIDTH), row(MLA_HEADS * KV_LORA), row(CONV_WIDTH)]
        + [_full(w.shape) for w in weights],
        out_specs=row(D_MODEL),
        out_shape=jax.ShapeDtypeStruct((n, D_MODEL), F32),
        compiler_params=_params("parallel"),
        name="even_out",
    )(x, ya, olat, sgb, *weights)


def _mem_kv_kernel(x_ref, g_ref, wk_ref, wv_ref, k_ref, v_ref):
    h = _rms(x_ref[...], g_ref[...]).astype(BF16)
    k_ref[...] = _dot(h, wk_ref[...])
    v_ref[...] = _dot(h, wv_ref[...])


def mem_kv(mem, g, wk, wv, tm):
    n = mem.shape[0]
    row = pl.BlockSpec((tm, D_MODEL), lambda i: (i, 0))
    return pl.pallas_call(
        _mem_kv_kernel,
        grid=(n // tm,),
        in_specs=[row, _full(g.shape), _full(wk.shape), _full(wv.shape)],
        out_specs=(row, row),
        out_shape=(jax.ShapeDtypeStruct((n, D_MODEL), F32),) * 2,
        compiler_params=_params("parallel"),
        name="mem_kv",
    )(mem, g, wk, wv)


def _xattn_kernel(x_ref, mk_ref, mv_ref, gpre_ref, wqg_ref, wo_ref, gpost_ref, o_ref, att_scr, *, bb, tq):
    x = x_ref[...].reshape(bb * tq, D_MODEL)
    qg = _dot(_rms(x, gpre_ref[...]), wqg_ref[...])
    q = qg[:, :D_MODEL] * (X_HEAD_DIM ** -0.5)
    for b in range(bb):
        for hd in range(X_HEADS):
            cs = slice(hd * X_HEAD_DIM, (hd + 1) * X_HEAD_DIM)
            s = _dot_nt(q[b * tq:(b + 1) * tq, cs], mk_ref[b, :, cs])
            p = jnp.exp(s - jnp.max(s, axis=1, keepdims=True))
            o = _dot(p, mv_ref[b, :, cs]) / jnp.sum(p, axis=1, keepdims=True)
            att_scr[b * tq:(b + 1) * tq, cs] = o
    y = _dot(att_scr[...] * _silu(qg[:, D_MODEL:]), wo_ref[...])
    o_ref[...] = (x + _rms(y, gpost_ref[...])).reshape(bb, tq, D_MODEL)


def xattn(x, mk, mv, layer, wts, bb, tq):
    bsz, t, _ = x.shape
    seq = pl.BlockSpec((bb, tq, D_MODEL), lambda b, i: (b, i, 0))
    mem = pl.BlockSpec((None, bb, MEM_LEN, D_MODEL), lambda b, i: (layer, b, 0, 0))
    weights = (wts["gprex"], wts["wqg"], wts["wo"], wts["gpostx"])
    return pl.pallas_call(
        functools.partial(_xattn_kernel, bb=bb, tq=tq),
        grid=(bsz // bb, t // tq),
        in_specs=[seq, mem, mem] + [_full(w.shape) for w in weights],
        out_specs=seq,
        out_shape=jax.ShapeDtypeStruct(x.shape, F32),
        scratch_shapes=[pltpu.VMEM((bb * tq, D_MODEL), F32)],
        compiler_params=_params("parallel", "arbitrary"),
        name="xattn",
    )(x, mk, mv, *weights)


def _odd_in_kernel(x_ref, gpre_ref, wz_ref, wxbc_ref, wdt_ref, wdtt_ref, brow_ref, bcol_ref,
                   sz_ref, xbc_ref, dt_ref, dtt_ref):
    h = _rms(x_ref[...], gpre_ref[...]).astype(BF16)
    sz_ref[...] = _silu(_dot(h, wz_ref[...]))
    xbc_ref[...] = _dot(h, wxbc_ref[...])
    dt_ref[...] = jax.nn.softplus(_dot(h, wdt_ref[...]) + brow_ref[...])
    dtt_ref[...] = jax.nn.softplus(_dot_nt(wdtt_ref[...], h) + bcol_ref[...])


def odd_in(x, wts, tm):
    n = x.shape[0]
    row = lambda d: pl.BlockSpec((tm, d), lambda i: (i, 0))
    weights = (wts["gpre"], wts["wz"], wts["wxbc"], wts["wdt"], wts["wdtt"], wts["brow"], wts["bcol"])
    return pl.pallas_call(
        _odd_in_kernel,
        grid=(n // tm,),
        in_specs=[row(D_MODEL)] + [_full(w.shape) for w in weights],
        out_specs=(row(SSM_INNER), row(XBC_DIM), row(SSM_HEADS),
                   pl.BlockSpec((SSM_HEADS, tm), lambda i: (0, i))),
        out_shape=(jax.ShapeDtypeStruct((n, SSM_INNER), F32), jax.ShapeDtypeStruct((n, XBC_DIM), F32),
                   jax.ShapeDtypeStruct((n, SSM_HEADS), F32), jax.ShapeDtypeStruct((SSM_HEADS, n), F32)),
        compiler_params=_params("parallel"),
        name="odd_in",
    )(x, *weights)


def _ssd_kernel(*refs, q, t_in, tvalid, has_init):
    if has_init:
        (xs_ref, bm_ref, cm_ref, dt_ref, dtt_ref, arow_ref, acol_ref, drow_ref, e64_ref, tri_ref,
         trit_ref, init_ref, y_ref, st_ref, state_scr) = refs
    else:
        (xs_ref, bm_ref, cm_ref, dt_ref, dtt_ref, arow_ref, acol_ref, drow_ref, e64_ref, tri_ref,
         trit_ref, y_ref, st_ref, state_scr) = refs
    c = pl.program_id(1)

    @pl.when(c == 0)
    def _():
        if has_init:
            state_scr[...] = init_ref[0].T
        else:
            state_scr[...] = jnp.zeros(state_scr.shape, F32)

    def rows(ref):
        v = ref[0]
        if t_in < q:
            v = jnp.concatenate([v, jnp.zeros((q - t_in, v.shape[1]), F32)], axis=0)
        return v

    xs, bm, cm = rows(xs_ref), rows(bm_ref), rows(cm_ref)
    dt = dt_ref[0]
    dtt = dtt_ref[...]
    if tvalid < q:
        dt = jnp.where(lax.broadcasted_iota(jnp.int32, dt.shape, 0) < tvalid, dt, 0.0)
        dtt = jnp.where(lax.broadcasted_iota(jnp.int32, dtt.shape, 1) < tvalid, dtt, 0.0)
    cum = _exact_dot_left(tri_ref[...], dt * arow_ref[...])
    cumt = _exact_dot(dtt * acol_ref[...], trit_ref[...])
    cum_last = cum[q - 1:q, :]
    causal = (lax.broadcasted_iota(jnp.int32, (q, q), 1) <= lax.broadcasted_iota(jnp.int32, (q, q), 0))
    lane = lax.broadcasted_iota(jnp.int32, (1, LANES), 1)
    left = lane < SSM_HEADDIM

    state = state_scr[...]
    y_parts = []
    for g in range(SSM_GROUPS):
        gs = slice(g * SSM_STATE, (g + 1) * SSM_STATE)
        cm_g = cm[:, gs]
        cb = _dot_nt(cm_g, bm[:, gs])
        for pr in range(SSM_HPG // 2):
            pair = g * (SSM_HPG // 2) + pr
            ps = slice(pair * LANES, (pair + 1) * LANES)
            rhs = jnp.concatenate([xs[:, ps], state[:, ps]], axis=0)
            y_pair = None
            for side in range(2):
                hd = 2 * pair + side
                ccol = jnp.broadcast_to(cum[:, hd:hd + 1], (q, q))
                seg = ccol - cumt[hd:hd + 1, :]
                w = cb * jnp.exp(jnp.where(causal, seg, -jnp.inf)) * dtt[hd:hd + 1, :]
                lhs = jnp.concatenate([w, cm_g * jnp.exp(ccol)], axis=1)
                keep = left if side == 0 else jnp.logical_not(left)
                part = _dot(lhs, jnp.where(keep, rhs, 0.0))
                y_pair = part if y_pair is None else y_pair + part
            y_parts.append(y_pair)
    y = jnp.concatenate(y_parts, axis=1) + drow_ref[...] * xs
    y_ref[0] = y[:t_in] if t_in < q else y

    to_end = jnp.exp(cum_last - cum) * dt
    toxs = _exact_dot(to_end, e64_ref[...]) * xs
    dec = _exact_dot(jnp.broadcast_to(jnp.exp(cum_last), (SUBLANES, SSM_HEADS)), e64_ref[...])[0:1, :]
    new_parts = []
    for g in range(SSM_GROUPS):
        gs = slice(g * SSM_STATE, (g + 1) * SSM_STATE)
        hs = slice(g * GROUP_WIDTH, (g + 1) * GROUP_WIDTH)
        new_parts.append(_dot(bm[:, gs].T, toxs[:, hs]))
    new_state = state * dec + jnp.concatenate(new_parts, axis=1)
    state_scr[...] = new_state

    @pl.when(c == pl.num_programs(1) - 1)
    def _():
        st_ref[0] = new_state.T


def ssd(xbc, dt, dtt, wts, init, q, t_in, tvalid):
    bsz, t, _ = xbc.shape
    nc = max(t // q, 1)
    nxb = SSM_INNER // (SSM_GROUPS * SSM_STATE)
    in_specs = [pl.BlockSpec((1, t_in, SSM_INNER), lambda b, c: (b, c, 0)),
                pl.BlockSpec((1, t_in, SSM_GROUPS * SSM_STATE), lambda b, c: (b, c, nxb)),
                pl.BlockSpec((1, t_in, SSM_GROUPS * SSM_STATE), lambda b, c: (b, c, nxb + 1)),
                pl.BlockSpec((1, q, SSM_HEADS), lambda b, c: (b, c, 0)),
                pl.BlockSpec((SSM_HEADS, q), lambda b, c: (0, b * nc + c))]
    consts = (wts["arow"], wts["acol"], wts["drow"], wts["e64"], wts["tri"], wts["trit"])
    in_specs += [_full(w.shape) for w in consts]
    args = [xbc, xbc, xbc, dt, dtt, *consts]
    if init is not None:
        in_specs.append(pl.BlockSpec((1, SSM_INNER, SSM_STATE), lambda b, c: (b, 0, 0)))
        args.append(init)
    return pl.pallas_call(
        functools.partial(_ssd_kernel, q=q, t_in=t_in, tvalid=tvalid, has_init=init is not None),
        grid=(bsz, nc),
        in_specs=in_specs,
        out_specs=(pl.BlockSpec((1, t_in, SSM_INNER), lambda b, c: (b, c, 0)),
                   pl.BlockSpec((1, SSM_INNER, SSM_STATE), lambda b, c: (b, 0, 0))),
        out_shape=(jax.ShapeDtypeStruct((bsz, t, SSM_INNER), F32),
                   jax.ShapeDtypeStruct((bsz, SSM_INNER, SSM_STATE), F32)),
        scratch_shapes=[pltpu.VMEM((SSM_STATE, SSM_INNER), F32)],
        compiler_params=_params("parallel", "arbitrary"),
        name="ssd",
    )(*args)


def _odd_out_kernel(x_ref, y_ref, sz_ref, gn_ref, wout_ref, gpost_ref, o_ref):
    v = y_ref[...] * sz_ref[...]
    gn = gn_ref[...]
    parts = []
    for g in range(SSM_GROUPS):
        gs = slice(g * GROUP_WIDTH, (g + 1) * GROUP_WIDTH)
        parts.append(_rms(v[:, gs], gn[:, gs]).astype(BF16))
    y = _dot(jnp.concatenate(parts, axis=1), wout_ref[...])
    o_ref[...] = x_ref[...] + _rms(y, gpost_ref[...])


def odd_out(x, y, sz, wts, tm):
    n = x.shape[0]
    row = lambda d: pl.BlockSpec((tm, d), lambda i: (i, 0))
    weights = (wts["gnorm"], wts["wout"], wts["gpost"])
    return pl.pallas_call(
        _odd_out_kernel,
        grid=(n // tm,),
        in_specs=[row(D_MODEL), row(SSM_INNER), row(SSM_INNER)] + [_full(w.shape) for w in weights],
        out_specs=row(D_MODEL),
        out_shape=jax.ShapeDtypeStruct((n, D_MODEL), F32),
        compiler_params=_params("parallel"),
        name="odd_out",
    )(x, y, sz, *weights)


def _even_weights(i, l, p):
    w_in = p["e_w_in"][l]
    c = CONV_WIDTH
    o_q, o_kv, o_bg = 3 * c, 3 * c + Q_LORA, 3 * c + Q_LORA + KV_LORA + ROPE_DIM
    half = ROPE_DIM // 2
    w_kr = w_in[:, o_kv + KV_LORA:o_bg]
    w_uq = p["e_w_uq"][l]
    uq_r = w_uq[:, :, NOPE_DIM:]
    w_uv = jnp.transpose(p["e_w_uv"][l], (1, 0, 2))
    eye = jnp.eye(MLA_HEADS, dtype=F32)
    wbd = (eye[:, None, :, None] * w_uv[:, :, None, :]).reshape(MLA_HEADS * KV_LORA, MLA_HEADS * V_DIM)
    w_out = p["e_w_out"][l]
    return {
        "gpre": p["norm_pre_mix"][i][None], "gpost": p["norm_post_mix"][i][None],
        "wa": w_in[:, :o_q].astype(BF16), "wq": w_in[:, o_q:o_kv].astype(BF16),
        "wkv": jnp.concatenate([w_in[:, o_kv:o_bg], -w_kr[:, half:], w_kr[:, :half]], axis=1).astype(BF16),
        "wbg": w_in[:, o_bg:].astype(BF16),
        "gq": p["e_q_norm"][l][None], "gkv": p["e_kv_norm"][l][None],
        "wuqn": w_uq[:, :, :NOPE_DIM].reshape(Q_LORA, MLA_HEADS * NOPE_DIM).astype(BF16),
        "wuqr": uq_r.reshape(Q_LORA, MLA_HEADS * ROPE_DIM).astype(BF16),
        "wuqrot": jnp.concatenate([-uq_r[:, :, half:], uq_r[:, :, :half]], axis=2)
        .reshape(Q_LORA, MLA_HEADS * ROPE_DIM).astype(BF16),
        "wukt": jnp.transpose(p["e_w_uk"][l], (1, 2, 0)).astype(BF16),
        "conv_w": p["e_conv_w"][l], "conv_b": p["e_conv_b"][l][None],
        "ln_w": p["e_ln_w"][l][None], "ln_b": p["e_ln_b"][l][None],
        "wbd": wbd.astype(BF16), "woa": w_out[:c].astype(BF16), "wob": w_out[c:].astype(BF16),
    }


def _odd_weights(i, l, p):
    w_in = p["o_w_in"][l]
    w_dt = w_in[:, SSM_INNER + XBC_DIM:]
    a = -jnp.exp(p["o_a_log"][l])
    q = SSM_CHUNK
    tri = jnp.tril(jnp.ones((q, q), F32))
    return {
        "gpre": p["norm_pre_mix"][i][None], "gpost": p["norm_post_mix"][i][None],
        "wz": w_in[:, :SSM_INNER].astype(BF16),
        "wxbc": w_in[:, SSM_INNER:SSM_INNER + XBC_DIM].astype(BF16),
        "wdt": w_dt.astype(BF16), "wdtt": w_dt.T.astype(BF16),
        "brow": p["o_dt_bias"][l][None], "bcol": p["o_dt_bias"][l][:, None],
        "conv_w": p["o_conv_w"][l], "conv_b": p["o_conv_b"][l][None],
        "arow": a[None], "acol": a[:, None],
        "drow": jnp.repeat(p["o_d"][l], SSM_HEADDIM)[None],
        "e64": jnp.repeat(jnp.eye(SSM_HEADS, dtype=F32), SSM_HEADDIM, axis=1).astype(BF16),
        "tri": tri.astype(BF16), "trit": tri.T.astype(BF16),
        "gnorm": p["o_norm"][l][None], "wout": p["o_w_out"][l].astype(BF16),
    }


def _xattn_weights(i, p):
    return {"gprex": p["norm_pre_x"][i][None], "gpostx": p["norm_post_x"][i][None],
            "wqg": p["x_w_qg"][i].astype(BF16), "wo": p["x_w_o"][i].astype(BF16)}


def kernel(x_prompt, x_sample, mem_prompt, cache_ckv, cache_krope, page_table, state_conv_a,
           state_conv_c, state_ssm, cache_mem_k, cache_mem_v, norm_pre_mix, norm_post_mix,
           norm_pre_x, norm_post_x, norm_mem, x_w_qg, x_w_k, x_w_v, x_w_o, e_w_in, e_conv_w,
           e_conv_b, e_ln_w, e_ln_b, e_q_norm, e_kv_norm, e_w_uq, e_w_uk, e_w_uv, e_w_out,
           o_w_in, o_conv_w, o_conv_b, o_dt_bias, o_a_log, o_d, o_norm, o_w_out):
    p = dict(norm_pre_mix=norm_pre_mix, norm_post_mix=norm_post_mix, norm_pre_x=norm_pre_x,
             norm_post_x=norm_post_x, e_w_in=e_w_in, e_conv_w=e_conv_w, e_conv_b=e_conv_b,
             e_ln_w=e_ln_w, e_ln_b=e_ln_b, e_q_norm=e_q_norm, e_kv_norm=e_kv_norm, e_w_uq=e_w_uq,
             e_w_uk=e_w_uk, e_w_uv=e_w_uv, e_w_out=e_w_out, o_w_in=o_w_in, o_conv_w=o_conv_w,
             o_conv_b=o_conv_b, o_dt_bias=o_dt_bias, o_a_log=o_a_log, o_d=o_d, o_norm=o_norm,
             o_w_out=o_w_out, x_w_qg=x_w_qg, x_w_o=x_w_o)
    bp, sp, _ = x_prompt.shape
    bs, ss, _ = x_sample.shape
    depth = norm_pre_mix.shape[0]
    n_pages, page = page_table.shape[1], cache_ckv.shape[1]
    past_len = n_pages * page
    tpad = SUBLANES
    assert ss <= tpad and sp % SSM_CHUNK == 0
    np_, ns = bp * sp, bs * tpad
    tm_p, tm_s = 512, ns

    half = ROPE_DIM // 2
    inv = ROPE_THETA ** (-jnp.arange(half, dtype=F32) / half)
    inv_lanes = jnp.tile(inv, LANES // half)[None]
    cos_p, sin_p = rope_table(inv_lanes, sp, 0, sp)
    cos_s, sin_s = rope_table(inv_lanes, ns, past_len, tpad)

    xp = x_prompt.reshape(np_, D_MODEL)
    xs = jnp.pad(x_sample, ((0, 0), (0, tpad - ss), (0, 0))).reshape(ns, D_MODEL)
    mem_flat = mem_prompt.reshape(bp * MEM_LEN, D_MODEL)
    mem_k_s = cache_mem_k.reshape(depth, bs, MEM_LEN, D_MODEL)
    mem_v_s = cache_mem_v.reshape(depth, bs, MEM_LEN, D_MODEL)

    outs = {k: [] for k in ("ckv_p", "ckv_s", "kr_p", "kr_s", "ca_p", "ca_s", "cc_p", "cc_s",
                            "ssm_p", "ssm_s", "mk", "mv")}
    for i in range(depth):
        l = i // 2
        if i % 2 == 0:
            w = _even_weights(i, l, p)
            glu, sga, sgb, qcat, kcat, ckv, kr = even_in(xp, w, cos_p, sin_p, tm_p)
            glu3 = glu.reshape(bp, sp, CONV_WIDTH)
            ya = causal_conv(glu3, jnp.zeros((bp, CONV_A_HALO, CONV_WIDTH), F32), w["conv_w"], w["conv_b"],
                             256, CONV_A_HALO, CONV_WIDTH,
                             ln=(w["ln_w"], w["ln_b"], sga.reshape(bp, sp, CONV_WIDTH)))
            olat = mla_prompt(qcat, kcat.reshape(bp, sp, QK_CAT), bp, sp, 256)
            yp = even_out(xp, ya.reshape(np_, CONV_WIDTH), olat, sgb, w, tm_p)
            outs["ca_p"].append(glu3[:, sp - (CONV_K - 1):])
            outs["ckv_p"].append(ckv.reshape(bp, sp, KV_LORA))
            outs["kr_p"].append(kr.reshape(bp, sp, ROPE_DIM))
            glu, sga, sgb, qcat, kcat, ckv, kr = even_in(xs, w, cos_s, sin_s, tm_s)
            glu3 = glu.reshape(bs, tpad, CONV_WIDTH)
            hist = state_conv_a[l]
            hist_pad = jnp.pad(hist, ((0, 0), (CONV_A_HALO - (CONV_K - 1), 0), (0, 0)))
            ya = causal_conv(glu3, hist_pad, w["conv_w"], w["conv_b"], tpad, CONV_A_HALO, CONV_WIDTH,
                             ln=(w["ln_w"], w["ln_b"], sga.reshape(bs, tpad, CONV_WIDTH)))
            knew = jnp.pad(kcat.reshape(bs, tpad, QK_CAT), ((0, 0), (0, LANES - tpad), (0, 0)))
            olat = mla_sample(qcat.astype(F32), knew, cache_ckv[:, :, l, :],
                              jnp.transpose(cache_krope[:, :, l, :], (0, 2, 1)), page_table, tpad, ss, 16)
            ys = even_out(xs, ya.reshape(ns, CONV_WIDTH), olat, sgb, w, tm_s)
            outs["ca_s"].append(jnp.concatenate([hist[:, ss:], glu3[:, :ss]], axis=1))
            outs["ckv_s"].append(ckv.reshape(bs, tpad, KV_LORA)[:, :ss])
            outs["kr_s"].append(kr.reshape(bs, tpad, ROPE_DIM)[:, :ss])
        else:
            w = _odd_weights(i, l, p)
            q = SSM_CHUNK
            sz, xbc, dt, dtt = odd_in(xp, w, 256)
            xbc3 = xbc.reshape(bp, sp, XBC_DIM)
            xbc_c = causal_conv(xbc3, jnp.zeros((bp, CONV_C_HALO, XBC_DIM), F32), w["conv_w"], w["conv_b"],
                                256, CONV_C_HALO, 512)
            y, st = ssd(xbc_c, dt.reshape(bp, sp, SSM_HEADS), dtt, w, None, q, q, q)
            yp = odd_out(xp, y.reshape(np_, SSM_INNER), sz, w, tm_p)
            outs["cc_p"].append(xbc3[:, sp - (SSM_CONV - 1):])
            outs["ssm_p"].append(st.reshape(bp, SSM_HEADS, SSM_HEADDIM, SSM_STATE))
            sz, xbc, dt, dtt = odd_in(xs, w, tm_s)
            xbc3 = xbc.reshape(bs, tpad, XBC_DIM)
            hist = state_conv_c[l]
            hist_pad = jnp.pad(hist, ((0, 0), (CONV_C_HALO - (SSM_CONV - 1), 0), (0, 0)))
            xbc_c = causal_conv(xbc3, hist_pad, w["conv_w"], w["conv_b"], tpad, CONV_C_HALO, 512)
            dt_pad = jnp.pad(dt.reshape(bs, tpad, SSM_HEADS), ((0, 0), (0, q - tpad), (0, 0)))
            dtt_pad = jnp.pad(dtt.reshape(SSM_HEADS, bs, tpad), ((0, 0), (0, 0), (0, q - tpad)))
            y, st = ssd(xbc_c, dt_pad, dtt_pad.reshape(SSM_HEADS, bs * q), w,
                        state_ssm[l].reshape(bs, SSM_INNER, SSM_STATE), q, tpad, ss)
            ys = odd_out(xs, y.reshape(ns, SSM_INNER), sz, w, tm_s)
            outs["cc_s"].append(jnp.concatenate([hist[:, ss:], xbc3[:, :ss]], axis=1)[:, -(SSM_CONV - 1):])
            outs["ssm_s"].append(st.reshape(bs, SSM_HEADS, SSM_HEADDIM, SSM_STATE))
        xp, xs = yp, ys
        wx = _xattn_weights(i, p)
        mk, mv = mem_kv(mem_flat, norm_mem[i][None], x_w_k[i].astype(BF16), x_w_v[i].astype(BF16), 512)
        mk3, mv3 = mk.reshape(bp, MEM_LEN, D_MODEL), mv.reshape(bp, MEM_LEN, D_MODEL)
        outs["mk"].append(mk3.reshape(bp, MEM_LEN, X_HEADS, X_HEAD_DIM))
        outs["mv"].append(mv3.reshape(bp, MEM_LEN, X_HEADS, X_HEAD_DIM))
        xp = xattn(xp.reshape(bp, sp, D_MODEL), mk3[None], mv3[None], 0, wx, 1, 512).reshape(np_, D_MODEL)
        xs = xattn(xs.reshape(bs, tpad, D_MODEL), mem_k_s, mem_v_s, i, wx, 4, tpad).reshape(ns, D_MODEL)

    return (xp.reshape(bp, sp, D_MODEL), xs.reshape(bs, tpad, D_MODEL)[:, :ss],
            jnp.stack(outs["ckv_p"], axis=2), jnp.stack(outs["ckv_s"], axis=2),
            jnp.stack(outs["kr_p"], axis=2), jnp.stack(outs["kr_s"], axis=2),
            jnp.stack(outs["ca_p"]), jnp.stack(outs["ca_s"]),
            jnp.stack(outs["cc_p"]), jnp.stack(outs["cc_s"]),
            jnp.stack(outs["ssm_p"]), jnp.stack(outs["ssm_s"]),
            jnp.stack(outs["mk"]), jnp.stack(outs["mv"]))
```

```python
import functools

import jax
import jax.numpy as jnp
from jax import lax
from jax.experimental import pallas as pl
from jax.experimental.pallas import tpu as pltpu

F32 = jnp.float32
BF16 = jnp.bfloat16

D_MODEL = 1024
CONV_WIDTH = 512
CONV_K = 31
MLA_HEADS = 8
Q_LORA = 256
KV_LORA = 128
NOPE_DIM = 64
ROPE_DIM = 32
V_DIM = 64
QK_CAT = KV_LORA + ROPE_DIM
LOG2E = 1.4426950408889634
MLA_SCALE = (NOPE_DIM + ROPE_DIM) ** -0.5 * LOG2E
ROPE_THETA = 10000.0
SSM_INNER = 2048
SSM_HEADDIM = 64
SSM_HEADS = 32
SSM_GROUPS = 4
SSM_HPG = 8
SSM_STATE = 128
SSM_CONV = 4
SSM_CHUNK = 128
XBC_DIM = SSM_INNER + 2 * SSM_GROUPS * SSM_STATE
GROUP_WIDTH = SSM_INNER // SSM_GROUPS
MEM_LEN = 256
X_HEADS = 4
X_HEAD_DIM = 256
EPS = 1e-6

LANES = 128
SUBLANES = 8
CONV_A_HALO = 32
CONV_C_HALO = 8
VMEM_LIMIT = 56 * 1024 * 1024


def _params(*sem):
    return pltpu.CompilerParams(dimension_semantics=sem, vmem_limit_bytes=VMEM_LIMIT)


def _rms(x, w):
    return x * lax.rsqrt(jnp.mean(x * x, axis=-1, keepdims=True) + EPS) * w


def _silu(x):
    return x * jax.nn.sigmoid(x)


def _dot(a, b):
    return jnp.dot(a.astype(BF16), b.astype(BF16), preferred_element_type=F32)


def _dot_nt(a, b):
    return lax.dot_general(a.astype(BF16), b.astype(BF16), (((1,), (1,)), ((), ())),
                           preferred_element_type=F32)


def _split3(x):
    hi = x.astype(BF16)
    r = x - hi.astype(F32)
    mid = r.astype(BF16)
    lo = (r - mid.astype(F32)).astype(BF16)
    return hi, mid, lo


def _exact_dot(x, e):
    hi, mid, lo = _split3(x)
    return (jnp.dot(hi, e, preferred_element_type=F32) + jnp.dot(mid, e, preferred_element_type=F32)
            + jnp.dot(lo, e, preferred_element_type=F32))


def _exact_dot_left(e, x):
    hi, mid, lo = _split3(x)
    return (jnp.dot(e, hi, preferred_element_type=F32) + jnp.dot(e, mid, preferred_element_type=F32)
            + jnp.dot(e, lo, preferred_element_type=F32))


def _full(shape):
    nd = len(shape)
    return pl.BlockSpec(shape, lambda *_: (0,) * nd)


def _rope_table_kernel(inv_ref, cos_ref, sin_ref, *, base, period):
    row = lax.broadcasted_iota(jnp.int32, cos_ref.shape, 0)
    pos = (base + (row & (period - 1))).astype(F32)
    ang = pos * inv_ref[...]
    cos_ref[...] = jnp.cos(ang)
    sin_ref[...] = jnp.sin(ang)


def rope_table(inv_lanes, rows, base, period):
    assert period & (period - 1) == 0
    return pl.pallas_call(
        functools.partial(_rope_table_kernel, base=base, period=period),
        out_shape=(jax.ShapeDtypeStruct((rows, LANES), F32),) * 2,
        name="rope_table",
    )(inv_lanes)


def _even_in_kernel(x_ref, gpre_ref, wa_ref, wq_ref, wkv_ref, wbg_ref, gq_ref, gkv_ref,
                    wuqn_ref, wuqr_ref, wuqrot_ref, wukt_ref, cos_ref, sin_ref,
                    glu_ref, sga_ref, sgb_ref, qcat_ref, kcat_ref, ckv_ref, kr_ref):
    h = _rms(x_ref[...], gpre_ref[...]).astype(BF16)
    ua = _dot(h, wa_ref[...])
    glu_ref[...] = ua[:, :CONV_WIDTH] * jax.nn.sigmoid(ua[:, CONV_WIDTH:2 * CONV_WIDTH])
    sga_ref[...] = _silu(ua[:, 2 * CONV_WIDTH:])
    sgb_ref[...] = _silu(_dot(h, wbg_ref[...]))
    cos = cos_ref[...]
    sin = sin_ref[...]
    kv = _dot(h, wkv_ref[...])
    ckv = _rms(kv[:, :KV_LORA], gkv_ref[...])
    kr = (kv[:, KV_LORA:KV_LORA + ROPE_DIM] * cos[:, :ROPE_DIM]
          + kv[:, KV_LORA + ROPE_DIM:] * sin[:, :ROPE_DIM])
    ckv_ref[...] = ckv
    kr_ref[...] = kr
    kcat_ref[:, :KV_LORA] = ckv.astype(BF16)
    kcat_ref[:, KV_LORA:] = kr.astype(BF16)
    qn = _rms(_dot(h, wq_ref[...]), gq_ref[...]).astype(BF16)
    qnope = _dot(qn, wuqn_ref[...])
    cos2 = jnp.concatenate([cos, cos], axis=1)
    sin2 = jnp.concatenate([sin, sin], axis=1)
    qr = (_dot(qn, wuqr_ref[...]) * cos2 + _dot(qn, wuqrot_ref[...]) * sin2) * MLA_SCALE
    for hd in range(MLA_HEADS):
        ql = _dot(qnope[:, hd * NOPE_DIM:(hd + 1) * NOPE_DIM], wukt_ref[hd]) * MLA_SCALE
        qcat_ref[hd, :, :KV_LORA] = ql.astype(BF16)
        qcat_ref[hd, :, KV_LORA:] = qr[:, hd * ROPE_DIM:(hd + 1) * ROPE_DIM].astype(BF16)


def even_in(x, wts, cos, sin, tm):
    n = x.shape[0]
    nblk = cos.shape[0] // tm
    row = lambda d: pl.BlockSpec((tm, d), lambda i: (i, 0))
    tab = pl.BlockSpec((tm, LANES), lambda i: (i % nblk, 0))
    weights = (wts["gpre"], wts["wa"], wts["wq"], wts["wkv"], wts["wbg"], wts["gq"], wts["gkv"],
               wts["wuqn"], wts["wuqr"], wts["wuqrot"], wts["wukt"])
    return pl.pallas_call(
        _even_in_kernel,
        grid=(n // tm,),
        in_specs=[row(D_MODEL)] + [_full(w.shape) for w in weights] + [tab, tab],
        out_specs=(row(CONV_WIDTH), row(CONV_WIDTH), row(CONV_WIDTH),
                   pl.BlockSpec((MLA_HEADS, tm, QK_CAT), lambda i: (0, i, 0)),
                   row(QK_CAT), row(KV_LORA), row(ROPE_DIM)),
        out_shape=(jax.ShapeDtypeStruct((n, CONV_WIDTH), F32),) * 3 + (
            jax.ShapeDtypeStruct((MLA_HEADS, n, QK_CAT), BF16),
            jax.ShapeDtypeStruct((n, QK_CAT), BF16),
            jax.ShapeDtypeStruct((n, KV_LORA), F32),
            jax.ShapeDtypeStruct((n, ROPE_DIM), F32)),
        compiler_params=_params("parallel"),
        name="even_in",
    )(x, *weights, cos, sin)


def _conv_kernel(hist_ref, prev_ref, cur_ref, w_ref, b_ref, *rest, taps, halo, row_chunk, lane_chunk,
                 layernorm):
    if layernorm:
        lnw_ref, lnb_ref, gate_ref, out_ref, ext_scr, *shift_scr = rest
    else:
        out_ref, ext_scr, *shift_scr = rest
    tt, width = cur_ref.shape[1], cur_ref.shape[2]
    ext_scr[0:halo] = jnp.where(pl.program_id(1) == 0, hist_ref[0], prev_ref[0])
    ext_scr[halo:halo + tt] = cur_ref[0]
    off = halo - (taps - 1)
    if shift_scr:
        sh_scr, = shift_scr
        span = sh_scr.shape[1]
        for r in range(1, SUBLANES):
            sh_scr[r - 1] = ext_scr[r:r + span]

    def ext_rows(d, r0, cs):
        a, r = divmod(d, SUBLANES)
        if not shift_scr or r == 0:
            return ext_scr[d + r0:d + r0 + row_chunk, cs]
        return sh_scr[r - 1, SUBLANES * a + r0:SUBLANES * a + r0 + row_chunk, cs]

    for r0 in range(0, tt, row_chunk):
        for c0 in range(0, width, lane_chunk):
            cs = slice(c0, c0 + lane_chunk)
            if shift_scr:
                taps_in = [ext_rows(off + k, r0, cs) for k in range(taps)]
            else:
                win = ext_scr[r0 + halo - SUBLANES:r0 + halo + row_chunk, cs]
                taps_in = [(pltpu.roll(win, taps - 1 - k, axis=0) if k < taps - 1 else win)[SUBLANES:]
                           for k in range(taps)]
            acc = w_ref[0:1, cs] * taps_in[0]
            for k in range(1, taps):
                acc = acc + w_ref[k:k + 1, cs] * taps_in[k]
            acc = acc + b_ref[:, cs]
            if layernorm:
                xc = acc - jnp.mean(acc, axis=-1, keepdims=True)
                var = jnp.mean(xc * xc, axis=-1, keepdims=True)
                ln = xc * lax.rsqrt(var + EPS) * lnw_ref[...] + lnb_ref[...]
                out_ref[0, r0:r0 + row_chunk, :] = _silu(ln) * gate_ref[0, r0:r0 + row_chunk, :]
            else:
                out_ref[0, r0:r0 + row_chunk, cs] = _silu(acc)


def causal_conv(x, hist, w, b, tt, halo, lane_chunk, ln=None):
    bsz, t, width = x.shape
    taps = w.shape[0]
    seq = pl.BlockSpec((1, tt, width), lambda bi, ti: (bi, ti, 0))
    first = pl.BlockSpec((1, halo, width), lambda bi, ti: (bi, 0, 0))
    if t == tt:
        prev, prev_spec = hist, first
    else:
        ratio = tt // halo
        prev = x
        prev_spec = pl.BlockSpec((1, halo, width), lambda bi, ti: (bi, jnp.maximum(ti * ratio - 1, 0), 0))
    in_specs = [first, prev_spec, seq, _full(w.shape), _full(b.shape)]
    args = [hist, prev, x, w, b]
    if ln is not None:
        in_specs += [_full(ln[0].shape), _full(ln[1].shape), seq]
        args += list(ln)
    return pl.pallas_call(
        functools.partial(_conv_kernel, taps=taps, halo=halo, row_chunk=min(32, tt),
                          lane_chunk=lane_chunk, layernorm=ln is not None),
        grid=(bsz, t // tt),
        in_specs=in_specs,
        out_specs=seq,
        out_shape=jax.ShapeDtypeStruct((bsz, t, width), F32),
        scratch_shapes=[pltpu.VMEM((halo + tt, width), F32)]
        + ([pltpu.VMEM((SUBLANES - 1, halo + tt - SUBLANES, width), F32)] if taps > SUBLANES else []),
        compiler_params=_params("parallel", "arbitrary"),
        name="causal_conv_ln" if ln is not None else "causal_conv",
    )(*args)


def _mla_prompt_kernel(q_ref, k_ref, o_ref, s_scr, acc_scr, mx_scr, sum_scr, *, tq):
    i = pl.program_id(1)
    rows = MLA_HEADS * tq
    q = q_ref[...].reshape(rows, QK_CAT)

    def keys(j):
        return k_ref[0, pl.ds(pl.multiple_of(j * tq, tq), tq), :]

    def fold(x):
        parts = [x[:, c:c + LANES] for c in range(0, tq, LANES)]
        return parts

    mx_scr[...] = jnp.full(mx_scr.shape, -jnp.inf, F32)

    def pass1(j, carry):
        s = _dot_nt(q, keys(j))
        s_scr[j] = s
        mx_scr[...] = functools.reduce(jnp.maximum, fold(s), mx_scr[...])
        return carry

    lax.fori_loop(0, i, pass1, 0)
    s = _dot_nt(q, keys(i))
    qpos = lax.broadcasted_iota(jnp.int32, s.shape, 0) & (tq - 1)
    kpos = lax.broadcasted_iota(jnp.int32, s.shape, 1)
    s = jnp.where(kpos <= qpos, s, -jnp.inf)
    s_scr[i] = s
    m = jnp.max(functools.reduce(jnp.maximum, fold(s), mx_scr[...]), axis=1, keepdims=True)

    acc_scr[...] = jnp.zeros(acc_scr.shape, F32)
    sum_scr[...] = jnp.zeros(sum_scr.shape, F32)

    def pass2(j, carry):
        p = jnp.exp2(s_scr[j] - m)
        acc_scr[...] += _dot(p, keys(j)[:, :KV_LORA])
        sum_scr[...] = functools.reduce(jnp.add, fold(p), sum_scr[...])
        return carry

    lax.fori_loop(0, i + 1, pass2, 0)
    o = acc_scr[...] / jnp.sum(sum_scr[...], axis=1, keepdims=True)
    for hd in range(MLA_HEADS):
        o_ref[:, hd * KV_LORA:(hd + 1) * KV_LORA] = o[hd * tq:(hd + 1) * tq, :]


def mla_prompt(qcat, kcat, bsz, t, tq):
    nq = t // tq
    rows = MLA_HEADS * tq
    return pl.pallas_call(
        functools.partial(_mla_prompt_kernel, tq=tq),
        grid=(bsz, nq),
        in_specs=[pl.BlockSpec((MLA_HEADS, tq, QK_CAT), lambda b, i: (0, b * nq + i, 0)),
                  pl.BlockSpec((1, t, QK_CAT), lambda b, i: (b, 0, 0))],
        out_specs=pl.BlockSpec((tq, MLA_HEADS * KV_LORA), lambda b, i: (b * nq + i, 0)),
        out_shape=jax.ShapeDtypeStruct((bsz * t, MLA_HEADS * KV_LORA), F32),
        scratch_shapes=[pltpu.VMEM((nq, rows, tq), F32), pltpu.VMEM((rows, KV_LORA), F32),
                        pltpu.VMEM((rows, LANES), F32), pltpu.VMEM((rows, LANES), F32)],
        compiler_params=_params("parallel", "arbitrary"),
        name="mla_prompt",
    )(qcat, kcat)


def _mla_sample_kernel(pt_ref, q_ref, knew_ref, ckv_hbm, krt_hbm, o_ref, ckv_buf, krt_buf, sem,
                       kc_scr, krt_scr, m_scr, l_scr, acc_scr, *, pages, page, tpad, tvalid):
    b = pl.program_id(0)
    j = pl.program_id(1)
    steps = pl.num_programs(1)
    rows = MLA_HEADS * tpad
    g = b * steps + j
    slot = g & 1

    def page_copies(bi, ji, sl):
        copies = []
        for i in range(pages):
            pg = pt_ref[bi, ji * pages + i]
            copies.append(pltpu.make_async_copy(ckv_hbm.at[pg], ckv_buf.at[sl, pl.ds(i * page, page)],
                                                sem.at[sl]))
            copies.append(pltpu.make_async_copy(krt_hbm.at[pg], krt_buf.at[sl, i], sem.at[sl]))
        return copies

    @pl.when(g == 0)
    def _():
        for cp in page_copies(b, j, slot):
            cp.start()

    @pl.when(g + 1 < pl.num_programs(0) * steps)
    def _():
        wrap = j + 1 == steps
        for cp in page_copies(jnp.where(wrap, b + 1, b), jnp.where(wrap, 0, j + 1), 1 - slot):
            cp.start()

    for cp in page_copies(b, j, slot):
        cp.wait()

    @pl.when(j == 0)
    def _():
        m_scr[...] = jnp.full(m_scr.shape, -jnp.inf, F32)
        l_scr[...] = jnp.zeros(l_scr.shape, F32)
        acc_scr[...] = jnp.zeros(acc_scr.shape, F32)

    q = q_ref[...].reshape(rows, QK_CAT)

    def update(s, v):
        m_prev = m_scr[...]
        m_new = jnp.maximum(m_prev, jnp.max(s, axis=1, keepdims=True))
        alpha = jnp.exp2(m_prev - m_new)
        p = jnp.exp2(s - m_new[:, 0:1])
        l_scr[...] = alpha * l_scr[...] + jnp.sum(p, axis=1, keepdims=True)
        m_scr[...] = m_new
        acc_scr[...] = alpha * acc_scr[...] + _dot(p, v)

    kc_scr[...] = ckv_buf[slot].astype(BF16)
    for i in range(pages):
        krt_scr[:, i * page:(i + 1) * page] = krt_buf[slot, i].astype(BF16)
    kc = kc_scr[...]
    update(_dot_nt(q[:, :KV_LORA], kc) + _dot(q[:, KV_LORA:], krt_scr[...]), kc)

    @pl.when(j == pl.num_programs(1) - 1)
    def _():
        knew = knew_ref[0]
        s = _dot_nt(q, knew)
        qpos = lax.broadcasted_iota(jnp.int32, s.shape, 0) & (tpad - 1)
        kpos = lax.broadcasted_iota(jnp.int32, s.shape, 1)
        s = jnp.where((kpos <= qpos) & (kpos < tvalid), s, -jnp.inf)
        update(s, knew[:, :KV_LORA])
        o = acc_scr[...] / l_scr[...]
        for hd in range(MLA_HEADS):
            o_ref[:, hd * KV_LORA:(hd + 1) * KV_LORA] = o[hd * tpad:(hd + 1) * tpad, :]


def mla_sample(qcat, knew, ckv_pages, krt_pages, page_table, tpad, tvalid, pages):
    bsz, n_pages = page_table.shape
    page = ckv_pages.shape[1]
    steps = n_pages // pages
    rows = MLA_HEADS * tpad

    grid_spec = pltpu.PrefetchScalarGridSpec(
        num_scalar_prefetch=1,
        grid=(bsz, steps),
        in_specs=[pl.BlockSpec((MLA_HEADS, tpad, QK_CAT), lambda b, j, pt: (0, b, 0)),
                  pl.BlockSpec((1, knew.shape[1], QK_CAT), lambda b, j, pt: (b, 0, 0)),
                  pl.BlockSpec(memory_space=pl.ANY), pl.BlockSpec(memory_space=pl.ANY)],
        out_specs=pl.BlockSpec((tpad, MLA_HEADS * KV_LORA), lambda b, j, pt: (b, 0)),
        scratch_shapes=[pltpu.VMEM((2, pages * page, KV_LORA), F32),
                        pltpu.VMEM((2, pages, ROPE_DIM, page), F32),
                        pltpu.SemaphoreType.DMA((2,)),
                        pltpu.VMEM((pages * page, KV_LORA), BF16),
                        pltpu.VMEM((ROPE_DIM, pages * page), BF16),
                        pltpu.VMEM((rows, LANES), F32), pltpu.VMEM((rows, LANES), F32),
                        pltpu.VMEM((rows, KV_LORA), F32)])
    return pl.pallas_call(
        functools.partial(_mla_sample_kernel, pages=pages, page=page, tpad=tpad, tvalid=tvalid),
        grid_spec=grid_spec,
        out_shape=jax.ShapeDtypeStruct((bsz * tpad, MLA_HEADS * KV_LORA), F32),
        compiler_params=_params("arbitrary", "arbitrary"),
        name="mla_sample",
    )(page_table, qcat, knew, ckv_pages, krt_pages)


def _even_out_kernel(x_ref, ya_ref, olat_ref, sgb_ref, wbd_ref, woa_ref, wob_ref, gpost_ref, o_ref):
    yb = _dot(olat_ref[...], wbd_ref[...]) * sgb_ref[...]
    y = _dot(ya_ref[...], woa_ref[...]) + _dot(yb, wob_ref[...])
    o_ref[...] = x_ref[...] + _rms(y, gpost_ref[...])


def even_out(x, ya, olat, sgb, wts, tm):
    n = x.shape[0]
    row = lambda d: pl.BlockSpec((tm, d), lambda i: (i, 0))
    weights = (wts["wbd"], wts["woa"], wts["wob"], wts["gpost"])
    return pl.pallas_call(
        _even_out_kernel,
        grid=(n // tm,),
        in_specs=[row(D_MODEL), row(CONV_WIDTH), row(MLA_HEADS * KV_LORA), row(CONV_WIDTH)]
        + [_full(w.shape) for w in weights],
        out_specs=row(D_MODEL),
        out_shape=jax.ShapeDtypeStruct((n, D_MODEL), F32),
        compiler_params=_params("parallel"),
        name="even_out",
    )(x, ya, olat, sgb, *weights)


def _mem_kv_kernel(x_ref, g_ref, wk_ref, wv_ref, k_ref, v_ref):
    h = _rms(x_ref[...], g_ref[...]).astype(BF16)
    k_ref[...] = _dot(h, wk_ref[...])
    v_ref[...] = _dot(h, wv_ref[...])


def mem_kv(mem, g, wk, wv, tm):
    n = mem.shape[0]
    row = pl.BlockSpec((tm, D_MODEL), lambda i: (i, 0))
    return pl.pallas_call(
        _mem_kv_kernel,
        grid=(n // tm,),
        in_specs=[row, _full(g.shape), _full(wk.shape), _full(wv.shape)],
        out_specs=(row, row),
        out_shape=(jax.ShapeDtypeStruct((n, D_MODEL), F32),) * 2,
        compiler_params=_params("parallel"),
        name="mem_kv",
    )(mem, g, wk, wv)


def _xattn_kernel(x_ref, mk_ref, mv_ref, gpre_ref, wqg_ref, wo_ref, gpost_ref, o_ref, att_scr, *, bb, tq):
    x = x_ref[...].reshape(bb * tq, D_MODEL)
    qg = _dot(_rms(x, gpre_ref[...]), wqg_ref[...])
    q = qg[:, :D_MODEL] * (X_HEAD_DIM ** -0.5)
    split_heads = len(mk_ref.shape) == 4
    for b in range(bb):
        for hd in range(X_HEADS):
            cs = slice(hd * X_HEAD_DIM, (hd + 1) * X_HEAD_DIM)
            mk = mk_ref[b, :, hd, :] if split_heads else mk_ref[b, :, cs]
            mv = mv_ref[b, :, hd, :] if split_heads else mv_ref[b, :, cs]
            s = _dot_nt(q[b * tq:(b + 1) * tq, cs], mk)
            p = jnp.exp(s - jnp.max(s, axis=1, keepdims=True))
            o = _dot(p, mv) / jnp.sum(p, axis=1, keepdims=True)
            att_scr[b * tq:(b + 1) * tq, cs] = o
    y = _dot(att_scr[...] * _silu(qg[:, D_MODEL:]), wo_ref[...])
    o_ref[...] = (x + _rms(y, gpost_ref[...])).reshape(bb, tq, D_MODEL)


def xattn(x, mk, mv, layer, wts, bb, tq):
    bsz, t, _ = x.shape
    seq = pl.BlockSpec((bb, tq, D_MODEL), lambda b, i: (b, i, 0))
    tail = mk.shape[2:]
    mem = pl.BlockSpec((None, bb) + tail, lambda b, i: (layer, b) + (0,) * len(tail))
    weights = (wts["gprex"], wts["wqg"], wts["wo"], wts["gpostx"])
    return pl.pallas_call(
        functools.partial(_xattn_kernel, bb=bb, tq=tq),
        grid=(bsz // bb, t // tq),
        in_specs=[seq, mem, mem] + [_full(w.shape) for w in weights],
        out_specs=seq,
        out_shape=jax.ShapeDtypeStruct(x.shape, F32),
        scratch_shapes=[pltpu.VMEM((bb * tq, D_MODEL), F32)],
        compiler_params=_params("parallel", "arbitrary"),
        name="xattn",
    )(x, mk, mv, *weights)


def _odd_in_kernel(x_ref, gpre_ref, wz_ref, wxbc_ref, wdt_ref, wdtt_ref, brow_ref, bcol_ref,
                   sz_ref, xbc_ref, dt_ref, dtt_ref):
    h = _rms(x_ref[...], gpre_ref[...]).astype(BF16)
    sz_ref[...] = _silu(_dot(h, wz_ref[...]))
    xbc_ref[...] = _dot(h, wxbc_ref[...])
    dt_ref[...] = jax.nn.softplus(_dot(h, wdt_ref[...]) + brow_ref[...])
    dtt_ref[...] = jax.nn.softplus(_dot_nt(wdtt_ref[...], h) + bcol_ref[...])


def odd_in(x, wts, tm):
    n = x.shape[0]
    row = lambda d: pl.BlockSpec((tm, d), lambda i: (i, 0))
    weights = (wts["gpre"], wts["wz"], wts["wxbc"], wts["wdt"], wts["wdtt"], wts["brow"], wts["bcol"])
    return pl.pallas_call(
        _odd_in_kernel,
        grid=(n // tm,),
        in_specs=[row(D_MODEL)] + [_full(w.shape) for w in weights],
        out_specs=(row(SSM_INNER), row(XBC_DIM), row(SSM_HEADS),
                   pl.BlockSpec((SSM_HEADS, tm), lambda i: (0, i))),
        out_shape=(jax.ShapeDtypeStruct((n, SSM_INNER), F32), jax.ShapeDtypeStruct((n, XBC_DIM), F32),
                   jax.ShapeDtypeStruct((n, SSM_HEADS), F32), jax.ShapeDtypeStruct((SSM_HEADS, n), F32)),
        compiler_params=_params("parallel"),
        name="odd_in",
    )(x, *weights)


def _ssd_kernel(*refs, q, t_in, tvalid, has_init):
    if has_init:
        (xs_ref, bm_ref, cm_ref, dt_ref, dtt_ref, arow_ref, acol_ref, drow_ref, e64_ref, tri_ref,
         trit_ref, init_ref, y_ref, st_ref, state_scr) = refs
    else:
        (xs_ref, bm_ref, cm_ref, dt_ref, dtt_ref, arow_ref, acol_ref, drow_ref, e64_ref, tri_ref,
         trit_ref, y_ref, st_ref, state_scr) = refs
    c = pl.program_id(1)

    @pl.when(c == 0)
    def _():
        if has_init:
            state_scr[...] = init_ref[0].T
        else:
            state_scr[...] = jnp.zeros(state_scr.shape, F32)

    def rows(ref):
        v = ref[0]
        if t_in < q:
            v = jnp.concatenate([v, jnp.zeros((q - t_in, v.shape[1]), F32)], axis=0)
        return v

    xs, bm, cm = rows(xs_ref), rows(bm_ref), rows(cm_ref)
    dt = dt_ref[0]
    dtt = dtt_ref[...]
    if tvalid < q:
        dt = jnp.where(lax.broadcasted_iota(jnp.int32, dt.shape, 0) < tvalid, dt, 0.0)
        dtt = jnp.where(lax.broadcasted_iota(jnp.int32, dtt.shape, 1) < tvalid, dtt, 0.0)
    cum = _exact_dot_left(tri_ref[...], dt * arow_ref[...])
    cumt = _exact_dot(dtt * acol_ref[...], trit_ref[...])
    cum_last = cum[q - 1:q, :]
    causal = (lax.broadcasted_iota(jnp.int32, (q, q), 1) <= lax.broadcasted_iota(jnp.int32, (q, q), 0))
    lane = lax.broadcasted_iota(jnp.int32, (1, LANES), 1)
    left = lane < SSM_HEADDIM

    state = state_scr[...]
    y_parts = []
    for g in range(SSM_GROUPS):
        gs = slice(g * SSM_STATE, (g + 1) * SSM_STATE)
        cm_g = cm[:, gs]
        cb = _dot_nt(cm_g, bm[:, gs])
        for pr in range(SSM_HPG // 2):
            pair = g * (SSM_HPG // 2) + pr
            ps = slice(pair * LANES, (pair + 1) * LANES)
            rhs = jnp.concatenate([xs[:, ps], state[:, ps]], axis=0)
            y_pair = None
            for side in range(2):
                hd = 2 * pair + side
                ccol = jnp.broadcast_to(cum[:, hd:hd + 1], (q, q))
                seg = ccol - cumt[hd:hd + 1, :]
                w = cb * jnp.exp(jnp.where(causal, seg, -jnp.inf)) * dtt[hd:hd + 1, :]
                lhs = jnp.concatenate([w, cm_g * jnp.exp(ccol)], axis=1)
                keep = left if side == 0 else jnp.logical_not(left)
                part = _dot(lhs, jnp.where(keep, rhs, 0.0))
                y_pair = part if y_pair is None else y_pair + part
            y_parts.append(y_pair)
    y = jnp.concatenate(y_parts, axis=1) + drow_ref[...] * xs
    y_ref[0] = y[:t_in] if t_in < q else y

    to_end = jnp.exp(cum_last - cum) * dt
    toxs = _dot(to_end, e64_ref[...]) * xs
    dec = _exact_dot(jnp.broadcast_to(jnp.exp(cum_last), (SUBLANES, SSM_HEADS)), e64_ref[...])[0:1, :]
    new_parts = []
    for g in range(SSM_GROUPS):
        gs = slice(g * SSM_STATE, (g + 1) * SSM_STATE)
        hs = slice(g * GROUP_WIDTH, (g + 1) * GROUP_WIDTH)
        new_parts.append(_dot(bm[:, gs].T, toxs[:, hs]))
    new_state = state * dec + jnp.concatenate(new_parts, axis=1)
    state_scr[...] = new_state

    @pl.when(c == pl.num_programs(1) - 1)
    def _():
        st_ref[0] = new_state.T


def ssd(xbc, dt, dtt, wts, init, q, t_in, tvalid):
    bsz, t, _ = xbc.shape
    nc = max(t // q, 1)
    nxb = SSM_INNER // (SSM_GROUPS * SSM_STATE)
    in_specs = [pl.BlockSpec((1, t_in, SSM_INNER), lambda b, c: (b, c, 0)),
                pl.BlockSpec((1, t_in, SSM_GROUPS * SSM_STATE), lambda b, c: (b, c, nxb)),
                pl.BlockSpec((1, t_in, SSM_GROUPS * SSM_STATE), lambda b, c: (b, c, nxb + 1)),
                pl.BlockSpec((1, q, SSM_HEADS), lambda b, c: (b, c, 0)),
                pl.BlockSpec((SSM_HEADS, q), lambda b, c: (0, b * nc + c))]
    consts = (wts["arow"], wts["acol"], wts["drow"], wts["e64"], wts["tri"], wts["trit"])
    in_specs += [_full(w.shape) for w in consts]
    args = [xbc, xbc, xbc, dt, dtt, *consts]
    if init is not None:
        in_specs.append(pl.BlockSpec((1, SSM_INNER, SSM_STATE), lambda b, c: (b, 0, 0)))
        args.append(init)
    return pl.pallas_call(
        functools.partial(_ssd_kernel, q=q, t_in=t_in, tvalid=tvalid, has_init=init is not None),
        grid=(bsz, nc),
        in_specs=in_specs,
        out_specs=(pl.BlockSpec((1, t_in, SSM_INNER), lambda b, c: (b, c, 0)),
                   pl.BlockSpec((1, SSM_INNER, SSM_STATE), lambda b, c: (b, 0, 0))),
        out_shape=(jax.ShapeDtypeStruct((bsz, t, SSM_INNER), F32),
                   jax.ShapeDtypeStruct((bsz, SSM_INNER, SSM_STATE), F32)),
        scratch_shapes=[pltpu.VMEM((SSM_STATE, SSM_INNER), F32)],
        compiler_params=_params("parallel", "arbitrary"),
        name="ssd",
    )(*args)


def _odd_out_kernel(x_ref, y_ref, sz_ref, gn_ref, wout_ref, gpost_ref, o_ref):
    v = y_ref[...] * sz_ref[...]
    gn = gn_ref[...]
    parts = []
    for g in range(SSM_GROUPS):
        gs = slice(g * GROUP_WIDTH, (g + 1) * GROUP_WIDTH)
        parts.append(_rms(v[:, gs], gn[:, gs]).astype(BF16))
    y = _dot(jnp.concatenate(parts, axis=1), wout_ref[...])
    o_ref[...] = x_ref[...] + _rms(y, gpost_ref[...])


def odd_out(x, y, sz, wts, tm):
    n = x.shape[0]
    row = lambda d: pl.BlockSpec((tm, d), lambda i: (i, 0))
    weights = (wts["gnorm"], wts["wout"], wts["gpost"])
    return pl.pallas_call(
        _odd_out_kernel,
        grid=(n // tm,),
        in_specs=[row(D_MODEL), row(SSM_INNER), row(SSM_INNER)] + [_full(w.shape) for w in weights],
        out_specs=row(D_MODEL),
        out_shape=jax.ShapeDtypeStruct((n, D_MODEL), F32),
        compiler_params=_params("parallel"),
        name="odd_out",
    )(x, y, sz, *weights)


def _even_weights(i, l, p):
    w_in = p["e_w_in"][l]
    c = CONV_WIDTH
    o_q, o_kv, o_bg = 3 * c, 3 * c + Q_LORA, 3 * c + Q_LORA + KV_LORA + ROPE_DIM
    half = ROPE_DIM // 2
    w_kr = w_in[:, o_kv + KV_LORA:o_bg]
    w_uq = p["e_w_uq"][l]
    uq_r = w_uq[:, :, NOPE_DIM:]
    w_uv = jnp.transpose(p["e_w_uv"][l], (1, 0, 2))
    eye = jnp.eye(MLA_HEADS, dtype=F32)
    wbd = (eye[:, None, :, None] * w_uv[:, :, None, :]).reshape(MLA_HEADS * KV_LORA, MLA_HEADS * V_DIM)
    w_out = p["e_w_out"][l]
    return {
        "gpre": p["norm_pre_mix"][i][None], "gpost": p["norm_post_mix"][i][None],
        "wa": w_in[:, :o_q].astype(BF16), "wq": w_in[:, o_q:o_kv].astype(BF16),
        "wkv": jnp.concatenate([w_in[:, o_kv:o_bg], -w_kr[:, half:], w_kr[:, :half]], axis=1).astype(BF16),
        "wbg": w_in[:, o_bg:].astype(BF16),
        "gq": p["e_q_norm"][l][None], "gkv": p["e_kv_norm"][l][None],
        "wuqn": w_uq[:, :, :NOPE_DIM].reshape(Q_LORA, MLA_HEADS * NOPE_DIM).astype(BF16),
        "wuqr": uq_r.reshape(Q_LORA, MLA_HEADS * ROPE_DIM).astype(BF16),
        "wuqrot": jnp.concatenate([-uq_r[:, :, half:], uq_r[:, :, :half]], axis=2)
        .reshape(Q_LORA, MLA_HEADS * ROPE_DIM).astype(BF16),
        "wukt": jnp.transpose(p["e_w_uk"][l], (1, 2, 0)).astype(BF16),
        "conv_w": p["e_conv_w"][l], "conv_b": p["e_conv_b"][l][None],
        "ln_w": p["e_ln_w"][l][None], "ln_b": p["e_ln_b"][l][None],
        "wbd": wbd.astype(BF16), "woa": w_out[:c].astype(BF16), "wob": w_out[c:].astype(BF16),
    }


def _odd_weights(i, l, p):
    w_in = p["o_w_in"][l]
    w_dt = w_in[:, SSM_INNER + XBC_DIM:]
    a = -jnp.exp(p["o_a_log"][l])
    q = SSM_CHUNK
    tri = jnp.tril(jnp.ones((q, q), F32))
    return {
        "gpre": p["norm_pre_mix"][i][None], "gpost": p["norm_post_mix"][i][None],
        "wz": w_in[:, :SSM_INNER].astype(BF16),
        "wxbc": w_in[:, SSM_INNER:SSM_INNER + XBC_DIM].astype(BF16),
        "wdt": w_dt.astype(BF16), "wdtt": w_dt.T.astype(BF16),
        "brow": p["o_dt_bias"][l][None], "bcol": p["o_dt_bias"][l][:, None],
        "conv_w": p["o_conv_w"][l], "conv_b": p["o_conv_b"][l][None],
        "arow": a[None], "acol": a[:, None],
        "drow": jnp.repeat(p["o_d"][l], SSM_HEADDIM)[None],
        "e64": jnp.repeat(jnp.eye(SSM_HEADS, dtype=F32), SSM_HEADDIM, axis=1).astype(BF16),
        "tri": tri.astype(BF16), "trit": tri.T.astype(BF16),
        "gnorm": p["o_norm"][l][None], "wout": p["o_w_out"][l].astype(BF16),
    }


def _xattn_weights(i, p):
    return {"gprex": p["norm_pre_x"][i][None], "gpostx": p["norm_post_x"][i][None],
            "wqg": p["x_w_qg"][i].astype(BF16), "wo": p["x_w_o"][i].astype(BF16)}


def kernel(x_prompt, x_sample, mem_prompt, cache_ckv, cache_krope, page_table, state_conv_a,
           state_conv_c, state_ssm, cache_mem_k, cache_mem_v, norm_pre_mix, norm_post_mix,
           norm_pre_x, norm_post_x, norm_mem, x_w_qg, x_w_k, x_w_v, x_w_o, e_w_in, e_conv_w,
           e_conv_b, e_ln_w, e_ln_b, e_q_norm, e_kv_norm, e_w_uq, e_w_uk, e_w_uv, e_w_out,
           o_w_in, o_conv_w, o_conv_b, o_dt_bias, o_a_log, o_d, o_norm, o_w_out):
    p = dict(norm_pre_mix=norm_pre_mix, norm_post_mix=norm_post_mix, norm_pre_x=norm_pre_x,
             norm_post_x=norm_post_x, e_w_in=e_w_in, e_conv_w=e_conv_w, e_conv_b=e_conv_b,
             e_ln_w=e_ln_w, e_ln_b=e_ln_b, e_q_norm=e_q_norm, e_kv_norm=e_kv_norm, e_w_uq=e_w_uq,
             e_w_uk=e_w_uk, e_w_uv=e_w_uv, e_w_out=e_w_out, o_w_in=o_w_in, o_conv_w=o_conv_w,
             o_conv_b=o_conv_b, o_dt_bias=o_dt_bias, o_a_log=o_a_log, o_d=o_d, o_norm=o_norm,
             o_w_out=o_w_out, x_w_qg=x_w_qg, x_w_o=x_w_o)
    bp, sp, _ = x_prompt.shape
    bs, ss, _ = x_sample.shape
    depth = norm_pre_mix.shape[0]
    n_pages, page = page_table.shape[1], cache_ckv.shape[1]
    past_len = n_pages * page
    tpad = SUBLANES
    assert ss <= tpad and sp % SSM_CHUNK == 0
    np_, ns = bp * sp, bs * tpad
    tm_p, tm_s = 512, ns

    half = ROPE_DIM // 2
    inv = ROPE_THETA ** (-jnp.arange(half, dtype=F32) / half)
    inv_lanes = jnp.tile(inv, LANES // half)[None]
    cos_p, sin_p = rope_table(inv_lanes, sp, 0, sp)
    cos_s, sin_s = rope_table(inv_lanes, ns, past_len, tpad)

    xp = x_prompt.reshape(np_, D_MODEL)
    xs = jnp.pad(x_sample, ((0, 0), (0, tpad - ss), (0, 0))).reshape(ns, D_MODEL)
    mem_flat = mem_prompt.reshape(bp * MEM_LEN, D_MODEL)
    mem_k_s, mem_v_s = cache_mem_k, cache_mem_v

    outs = {k: [] for k in ("ckv_p", "ckv_s", "kr_p", "kr_s", "ca_p", "ca_s", "cc_p", "cc_s",
                            "ssm_p", "ssm_s", "mk", "mv")}
    for i in range(depth):
        l = i // 2
        if i % 2 == 0:
            w = _even_weights(i, l, p)
            glu, sga, sgb, qcat, kcat, ckv, kr = even_in(xp, w, cos_p, sin_p, tm_p)
            glu3 = glu.reshape(bp, sp, CONV_WIDTH)
            ya = causal_conv(glu3, jnp.zeros((bp, CONV_A_HALO, CONV_WIDTH), F32), w["conv_w"], w["conv_b"],
                             256, CONV_A_HALO, CONV_WIDTH,
                             ln=(w["ln_w"], w["ln_b"], sga.reshape(bp, sp, CONV_WIDTH)))
            olat = mla_prompt(qcat, kcat.reshape(bp, sp, QK_CAT), bp, sp, 256)
            yp = even_out(xp, ya.reshape(np_, CONV_WIDTH), olat, sgb, w, tm_p)
            outs["ca_p"].append(glu3[:, sp - (CONV_K - 1):])
            outs["ckv_p"].append(ckv.reshape(bp, sp, KV_LORA))
            outs["kr_p"].append(kr.reshape(bp, sp, ROPE_DIM))
            glu, sga, sgb, qcat, kcat, ckv, kr = even_in(xs, w, cos_s, sin_s, tm_s)
            glu3 = glu.reshape(bs, tpad, CONV_WIDTH)
            hist = state_conv_a[l]
            hist_pad = jnp.pad(hist, ((0, 0), (CONV_A_HALO - (CONV_K - 1), 0), (0, 0)))
            ya = causal_conv(glu3, hist_pad, w["conv_w"], w["conv_b"], tpad, CONV_A_HALO, CONV_WIDTH,
                             ln=(w["ln_w"], w["ln_b"], sga.reshape(bs, tpad, CONV_WIDTH)))
            knew = jnp.pad(kcat.reshape(bs, tpad, QK_CAT), ((0, 0), (0, LANES - tpad), (0, 0)))
            olat = mla_sample(qcat.astype(F32), knew, cache_ckv[:, :, l, :],
                              jnp.transpose(cache_krope[:, :, l, :], (0, 2, 1)), page_table, tpad, ss,
                              min(32, n_pages))
            ys = even_out(xs, ya.reshape(ns, CONV_WIDTH), olat, sgb, w, tm_s)
            outs["ca_s"].append(jnp.concatenate([hist[:, ss:], glu3[:, :ss]], axis=1))
            outs["ckv_s"].append(ckv.reshape(bs, tpad, KV_LORA)[:, :ss])
            outs["kr_s"].append(kr.reshape(bs, tpad, ROPE_DIM)[:, :ss])
        else:
            w = _odd_weights(i, l, p)
            q = SSM_CHUNK
            sz, xbc, dt, dtt = odd_in(xp, w, 256)
            xbc3 = xbc.reshape(bp, sp, XBC_DIM)
            xbc_c = causal_conv(xbc3, jnp.zeros((bp, CONV_C_HALO, XBC_DIM), F32), w["conv_w"], w["conv_b"],
                                256, CONV_C_HALO, 512)
            y, st = ssd(xbc_c, dt.reshape(bp, sp, SSM_HEADS), dtt, w, None, q, q, q)
            yp = odd_out(xp, y.reshape(np_, SSM_INNER), sz, w, tm_p)
            outs["cc_p"].append(xbc3[:, sp - (SSM_CONV - 1):])
            outs["ssm_p"].append(st.reshape(bp, SSM_HEADS, SSM_HEADDIM, SSM_STATE))
            sz, xbc, dt, dtt = odd_in(xs, w, tm_s)
            xbc3 = xbc.reshape(bs, tpad, XBC_DIM)
            hist = state_conv_c[l]
            hist_pad = jnp.pad(hist, ((0, 0), (CONV_C_HALO - (SSM_CONV - 1), 0), (0, 0)))
            xbc_c = causal_conv(xbc3, hist_pad, w["conv_w"], w["conv_b"], tpad, CONV_C_HALO, 512)
            dt_pad = jnp.pad(dt.reshape(bs, tpad, SSM_HEADS), ((0, 0), (0, q - tpad), (0, 0)))
            dtt_pad = jnp.pad(dtt.reshape(SSM_HEADS, bs, tpad), ((0, 0), (0, 0), (0, q - tpad)))
            y, st = ssd(xbc_c, dt_pad, dtt_pad.reshape(SSM_HEADS, bs * q), w,
                        state_ssm[l].reshape(bs, SSM_INNER, SSM_STATE), q, tpad, ss)
            ys = odd_out(xs, y.reshape(ns, SSM_INNER), sz, w, tm_s)
            outs["cc_s"].append(jnp.concatenate([hist[:, ss:], xbc3[:, :ss]], axis=1)[:, -(SSM_CONV - 1):])
            outs["ssm_s"].append(st.reshape(bs, SSM_HEADS, SSM_HEADDIM, SSM_STATE))
        xp, xs = yp, ys
        wx = _xattn_weights(i, p)
        mk, mv = mem_kv(mem_flat, norm_mem[i][None], x_w_k[i].astype(BF16), x_w_v[i].astype(BF16), 512)
        mk3, mv3 = mk.reshape(bp, MEM_LEN, D_MODEL), mv.reshape(bp, MEM_LEN, D_MODEL)
        outs["mk"].append(mk3.reshape(bp, MEM_LEN, X_HEADS, X_HEAD_DIM))
        outs["mv"].append(mv3.reshape(bp, MEM_LEN, X_HEADS, X_HEAD_DIM))
        xp = xattn(xp.reshape(bp, sp, D_MODEL), mk3[None], mv3[None], 0, wx, 1, 512).reshape(np_, D_MODEL)
        xs = xattn(xs.reshape(bs, tpad, D_MODEL), mem_k_s, mem_v_s, i, wx, 4, tpad).reshape(ns, D_MODEL)

    return (xp.reshape(bp, sp, D_MODEL), xs.reshape(bs, tpad, D_MODEL)[:, :ss],
            jnp.stack(outs["ckv_p"], axis=2), jnp.stack(outs["ckv_s"], axis=2),
            jnp.stack(outs["kr_p"], axis=2), jnp.stack(outs["kr_s"], axis=2),
            jnp.stack(outs["ca_p"]), jnp.stack(outs["ca_s"]),
            jnp.stack(outs["cc_p"]), jnp.stack(outs["cc_s"]),
            jnp.stack(outs["ssm_p"]), jnp.stack(outs["ssm_s"]),
            jnp.stack(outs["mk"]), jnp.stack(outs["mv"]))
```

```python
import functools

import jax
import jax.numpy as jnp
from jax import lax
from jax.experimental import pallas as pl
from jax.experimental.pallas import tpu as pltpu

F32 = jnp.float32
BF16 = jnp.bfloat16

D_MODEL = 1024
CONV_WIDTH = 512
CONV_K = 31
MLA_HEADS = 8
Q_LORA = 256
KV_LORA = 128
NOPE_DIM = 64
ROPE_DIM = 32
V_DIM = 64
QK_CAT = KV_LORA + ROPE_DIM
LOG2E = 1.4426950408889634
MLA_SCALE = (NOPE_DIM + ROPE_DIM) ** -0.5 * LOG2E
ROPE_THETA = 10000.0
SSM_INNER = 2048
SSM_HEADDIM = 64
SSM_HEADS = 32
SSM_GROUPS = 4
SSM_HPG = 8
SSM_STATE = 128
SSM_CONV = 4
SSM_CHUNK = 128
XBC_DIM = SSM_INNER + 2 * SSM_GROUPS * SSM_STATE
GROUP_WIDTH = SSM_INNER // SSM_GROUPS
MEM_LEN = 256
X_HEADS = 4
X_HEAD_DIM = 256
EPS = 1e-6

LANES = 128
SUBLANES = 8
CONV_A_HALO = 32
CONV_C_HALO = 8
VMEM_LIMIT = 56 * 1024 * 1024


def _params(*sem):
    return pltpu.CompilerParams(dimension_semantics=sem, vmem_limit_bytes=VMEM_LIMIT)


def _rms(x, w):
    return x * lax.rsqrt(jnp.mean(x * x, axis=-1, keepdims=True) + EPS) * w


def _silu(x):
    return x * jax.nn.sigmoid(x)


def _dot(a, b):
    return jnp.dot(a.astype(BF16), b.astype(BF16), preferred_element_type=F32)


def _dot_nt(a, b):
    return lax.dot_general(a.astype(BF16), b.astype(BF16), (((1,), (1,)), ((), ())),
                           preferred_element_type=F32)


def _split3(x):
    hi = x.astype(BF16)
    r = x - hi.astype(F32)
    mid = r.astype(BF16)
    lo = (r - mid.astype(F32)).astype(BF16)
    return hi, mid, lo


def _exact_dot(x, e):
    hi, mid, lo = _split3(x)
    return (jnp.dot(hi, e, preferred_element_type=F32) + jnp.dot(mid, e, preferred_element_type=F32)
            + jnp.dot(lo, e, preferred_element_type=F32))


def _exact_dot_left(e, x):
    hi, mid, lo = _split3(x)
    return (jnp.dot(e, hi, preferred_element_type=F32) + jnp.dot(e, mid, preferred_element_type=F32)
            + jnp.dot(e, lo, preferred_element_type=F32))


def _full(shape):
    nd = len(shape)
    return pl.BlockSpec(shape, lambda *_: (0,) * nd)


def _rope_table_kernel(inv_ref, cos_ref, sin_ref, *, base, period):
    row = lax.broadcasted_iota(jnp.int32, cos_ref.shape, 0)
    pos = (base + (row & (period - 1))).astype(F32)
    ang = pos * inv_ref[...]
    cos_ref[...] = jnp.cos(ang)
    sin_ref[...] = jnp.sin(ang)


def rope_table(inv_lanes, rows, base, period):
    assert period & (period - 1) == 0
    return pl.pallas_call(
        functools.partial(_rope_table_kernel, base=base, period=period),
        out_shape=(jax.ShapeDtypeStruct((rows, LANES), F32),) * 2,
        name="rope_table",
    )(inv_lanes)


def _even_in_kernel(x_ref, gpre_ref, wa_ref, wq_ref, wkv_ref, wbg_ref, gq_ref, gkv_ref,
                    wuqn_ref, wuqr_ref, wuqrot_ref, wukt_ref, cos_ref, sin_ref,
                    glu_ref, sga_ref, sgb_ref, qcat_ref, kcat_ref, ckv_ref, kr_ref):
    h = _rms(x_ref[...], gpre_ref[...]).astype(BF16)
    ua = _dot(h, wa_ref[...])
    glu_ref[...] = ua[:, :CONV_WIDTH] * jax.nn.sigmoid(ua[:, CONV_WIDTH:2 * CONV_WIDTH])
    sga_ref[...] = _silu(ua[:, 2 * CONV_WIDTH:])
    sgb_ref[...] = _silu(_dot(h, wbg_ref[...])).astype(BF16)
    cos = cos_ref[...]
    sin = sin_ref[...]
    kv = _dot(h, wkv_ref[...])
    ckv = _rms(kv[:, :KV_LORA], gkv_ref[...])
    kr = (kv[:, KV_LORA:KV_LORA + ROPE_DIM] * cos[:, :ROPE_DIM]
          + kv[:, KV_LORA + ROPE_DIM:] * sin[:, :ROPE_DIM])
    ckv_ref[...] = ckv
    kr_ref[...] = kr
    kcat_ref[:, :KV_LORA] = ckv.astype(BF16)
    kcat_ref[:, KV_LORA:] = kr.astype(BF16)
    qn = _rms(_dot(h, wq_ref[...]), gq_ref[...]).astype(BF16)
    qnope = _dot(qn, wuqn_ref[...])
    cos2 = jnp.concatenate([cos, cos], axis=1)
    sin2 = jnp.concatenate([sin, sin], axis=1)
    qr = (_dot(qn, wuqr_ref[...]) * cos2 + _dot(qn, wuqrot_ref[...]) * sin2) * MLA_SCALE
    for hd in range(MLA_HEADS):
        ql = _dot(qnope[:, hd * NOPE_DIM:(hd + 1) * NOPE_DIM], wukt_ref[hd]) * MLA_SCALE
        qcat_ref[hd, :, :KV_LORA] = ql.astype(BF16)
        qcat_ref[hd, :, KV_LORA:] = qr[:, hd * ROPE_DIM:(hd + 1) * ROPE_DIM].astype(BF16)


def even_in(x, wts, cos, sin, tm):
    n = x.shape[0]
    nblk = cos.shape[0] // tm
    row = lambda d: pl.BlockSpec((tm, d), lambda i: (i, 0))
    tab = pl.BlockSpec((tm, LANES), lambda i: (i % nblk, 0))
    weights = (wts["gpre"], wts["wa"], wts["wq"], wts["wkv"], wts["wbg"], wts["gq"], wts["gkv"],
               wts["wuqn"], wts["wuqr"], wts["wuqrot"], wts["wukt"])
    return pl.pallas_call(
        _even_in_kernel,
        grid=(n // tm,),
        in_specs=[row(D_MODEL)] + [_full(w.shape) for w in weights] + [tab, tab],
        out_specs=(row(CONV_WIDTH), row(CONV_WIDTH), row(CONV_WIDTH),
                   pl.BlockSpec((MLA_HEADS, tm, QK_CAT), lambda i: (0, i, 0)),
                   row(QK_CAT), row(KV_LORA), row(ROPE_DIM)),
        out_shape=(jax.ShapeDtypeStruct((n, CONV_WIDTH), F32),) * 2 + (
            jax.ShapeDtypeStruct((n, CONV_WIDTH), BF16),
            jax.ShapeDtypeStruct((MLA_HEADS, n, QK_CAT), BF16),
            jax.ShapeDtypeStruct((n, QK_CAT), BF16),
            jax.ShapeDtypeStruct((n, KV_LORA), F32),
            jax.ShapeDtypeStruct((n, ROPE_DIM), F32)),
        compiler_params=_params("parallel"),
        name="even_in",
    )(x, *weights, cos, sin)


def _conv_kernel(hist_ref, prev_ref, cur_ref, w_ref, b_ref, *rest, taps, halo, row_chunk, lane_chunk,
                 layernorm):
    if layernorm:
        lnw_ref, lnb_ref, gate_ref, out_ref, ext_scr, *shift_scr = rest
    else:
        out_ref, ext_scr, *shift_scr = rest
    tt, width = cur_ref.shape[1], cur_ref.shape[2]
    ext_scr[0:halo] = jnp.where(pl.program_id(1) == 0, hist_ref[0], prev_ref[0])
    ext_scr[halo:halo + tt] = cur_ref[0]
    off = halo - (taps - 1)
    if shift_scr:
        sh_scr, = shift_scr
        span = sh_scr.shape[1]
        for r in range(1, SUBLANES):
            sh_scr[r - 1] = ext_scr[r:r + span]

    def ext_rows(d, r0, cs):
        a, r = divmod(d, SUBLANES)
        if not shift_scr or r == 0:
            return ext_scr[d + r0:d + r0 + row_chunk, cs]
        return sh_scr[r - 1, SUBLANES * a + r0:SUBLANES * a + r0 + row_chunk, cs]

    for r0 in range(0, tt, row_chunk):
        for c0 in range(0, width, lane_chunk):
            cs = slice(c0, c0 + lane_chunk)
            if shift_scr:
                taps_in = [ext_rows(off + k, r0, cs) for k in range(taps)]
            else:
                win = ext_scr[r0 + halo - SUBLANES:r0 + halo + row_chunk, cs]
                taps_in = [(pltpu.roll(win, taps - 1 - k, axis=0) if k < taps - 1 else win)[SUBLANES:]
                           for k in range(taps)]
            acc = w_ref[0:1, cs] * taps_in[0]
            for k in range(1, taps):
                acc = acc + w_ref[k:k + 1, cs] * taps_in[k]
            acc = acc + b_ref[:, cs]
            if layernorm:
                xc = acc - jnp.mean(acc, axis=-1, keepdims=True)
                var = jnp.mean(xc * xc, axis=-1, keepdims=True)
                ln = xc * lax.rsqrt(var + EPS) * lnw_ref[...] + lnb_ref[...]
                out_ref[0, r0:r0 + row_chunk, :] = (_silu(ln) * gate_ref[0, r0:r0 + row_chunk, :]
                                                    ).astype(out_ref.dtype)
            else:
                out_ref[0, r0:r0 + row_chunk, cs] = _silu(acc)


def causal_conv(x, hist, w, b, tt, halo, lane_chunk, ln=None, out_dtype=F32):
    bsz, t, width = x.shape
    taps = w.shape[0]
    seq = pl.BlockSpec((1, tt, width), lambda bi, ti: (bi, ti, 0))
    first = pl.BlockSpec((1, halo, width), lambda bi, ti: (bi, 0, 0))
    if t == tt:
        prev, prev_spec = hist, first
    else:
        ratio = tt // halo
        prev = x
        prev_spec = pl.BlockSpec((1, halo, width), lambda bi, ti: (bi, jnp.maximum(ti * ratio - 1, 0), 0))
    in_specs = [first, prev_spec, seq, _full(w.shape), _full(b.shape)]
    args = [hist, prev, x, w, b]
    if ln is not None:
        in_specs += [_full(ln[0].shape), _full(ln[1].shape), seq]
        args += list(ln)
    return pl.pallas_call(
        functools.partial(_conv_kernel, taps=taps, halo=halo, row_chunk=min(32, tt),
                          lane_chunk=lane_chunk, layernorm=ln is not None),
        grid=(bsz, t // tt),
        in_specs=in_specs,
        out_specs=seq,
        out_shape=jax.ShapeDtypeStruct((bsz, t, width), out_dtype),
        scratch_shapes=[pltpu.VMEM((halo + tt, width), F32)]
        + ([pltpu.VMEM((SUBLANES - 1, halo + tt - SUBLANES, width), F32)] if taps > SUBLANES else []),
        compiler_params=_params("parallel", "arbitrary"),
        name="causal_conv_ln" if ln is not None else "causal_conv",
    )(*args)


def _mla_prompt_kernel(q_ref, k_ref, o_ref, s_scr, acc_scr, mx_scr, sum_scr, *, tq):
    i = pl.program_id(1)
    rows = MLA_HEADS * tq
    q = q_ref[...].reshape(rows, QK_CAT)

    def keys(j):
        return k_ref[0, pl.ds(pl.multiple_of(j * tq, tq), tq), :]

    def fold(x):
        parts = [x[:, c:c + LANES] for c in range(0, tq, LANES)]
        return parts

    mx_scr[...] = jnp.full(mx_scr.shape, -jnp.inf, F32)

    def pass1(j, carry):
        s = _dot_nt(q, keys(j))
        s_scr[j] = s
        mx_scr[...] = functools.reduce(jnp.maximum, fold(s), mx_scr[...])
        return carry

    lax.fori_loop(0, i, pass1, 0)
    s = _dot_nt(q, keys(i))
    qpos = lax.broadcasted_iota(jnp.int32, s.shape, 0) & (tq - 1)
    kpos = lax.broadcasted_iota(jnp.int32, s.shape, 1)
    s = jnp.where(kpos <= qpos, s, -jnp.inf)
    s_scr[i] = s
    m = jnp.max(functools.reduce(jnp.maximum, fold(s), mx_scr[...]), axis=1, keepdims=True)

    acc_scr[...] = jnp.zeros(acc_scr.shape, F32)
    sum_scr[...] = jnp.zeros(sum_scr.shape, F32)

    def pass2(j, carry):
        p = jnp.exp2(s_scr[j] - m)
        acc_scr[...] += _dot(p, keys(j)[:, :KV_LORA])
        sum_scr[...] = functools.reduce(jnp.add, fold(p), sum_scr[...])
        return carry

    lax.fori_loop(0, i + 1, pass2, 0)
    o = acc_scr[...] / jnp.sum(sum_scr[...], axis=1, keepdims=True)
    for hd in range(MLA_HEADS):
        o_ref[:, hd * KV_LORA:(hd + 1) * KV_LORA] = o[hd * tq:(hd + 1) * tq, :].astype(BF16)


def mla_prompt(qcat, kcat, bsz, t, tq):
    nq = t // tq
    rows = MLA_HEADS * tq
    return pl.pallas_call(
        functools.partial(_mla_prompt_kernel, tq=tq),
        grid=(bsz, nq),
        in_specs=[pl.BlockSpec((MLA_HEADS, tq, QK_CAT), lambda b, i: (0, b * nq + i, 0)),
                  pl.BlockSpec((1, t, QK_CAT), lambda b, i: (b, 0, 0))],
        out_specs=pl.BlockSpec((tq, MLA_HEADS * KV_LORA), lambda b, i: (b * nq + i, 0)),
        out_shape=jax.ShapeDtypeStruct((bsz * t, MLA_HEADS * KV_LORA), BF16),
        scratch_shapes=[pltpu.VMEM((nq, rows, tq), F32), pltpu.VMEM((rows, KV_LORA), F32),
                        pltpu.VMEM((rows, LANES), F32), pltpu.VMEM((rows, LANES), F32)],
        compiler_params=_params("parallel", "arbitrary"),
        name="mla_prompt",
    )(qcat, kcat)


def _mla_sample_kernel(pt_ref, q_ref, knew_ref, ckv_hbm, krt_hbm, o_ref, ckv_buf, krt_buf, sem,
                       kc_scr, krt_scr, m_scr, l_scr, acc_scr, *, pages, page, tpad, tvalid):
    b = pl.program_id(0)
    j = pl.program_id(1)
    steps = pl.num_programs(1)
    rows = MLA_HEADS * tpad
    g = b * steps + j
    slot = g & 1

    def page_copies(bi, ji, sl):
        copies = []
        for i in range(pages):
            pg = pt_ref[bi, ji * pages + i]
            copies.append(pltpu.make_async_copy(ckv_hbm.at[pg], ckv_buf.at[sl, pl.ds(i * page, page)],
                                                sem.at[sl]))
            copies.append(pltpu.make_async_copy(krt_hbm.at[pg], krt_buf.at[sl, i], sem.at[sl]))
        return copies

    def start_all(copies):
        for n, cp in enumerate(copies):
            cp.start(priority=(n // 2) % 2)

    @pl.when(g == 0)
    def _():
        start_all(page_copies(b, j, slot))

    @pl.when(g + 1 < pl.num_programs(0) * steps)
    def _():
        wrap = j + 1 == steps
        start_all(page_copies(jnp.where(wrap, b + 1, b), jnp.where(wrap, 0, j + 1), 1 - slot))

    for cp in page_copies(b, j, slot):
        cp.wait()

    @pl.when(j == 0)
    def _():
        m_scr[...] = jnp.full(m_scr.shape, -jnp.inf, F32)
        l_scr[...] = jnp.zeros(l_scr.shape, F32)
        acc_scr[...] = jnp.zeros(acc_scr.shape, F32)

    q = q_ref[...].reshape(rows, QK_CAT)

    def update(s, v):
        m_prev = m_scr[...]
        m_new = jnp.maximum(m_prev, jnp.max(s, axis=1, keepdims=True))
        alpha = jnp.exp2(m_prev - m_new)
        p = jnp.exp2(s - m_new[:, 0:1])
        l_scr[...] = alpha * l_scr[...] + jnp.sum(p, axis=1, keepdims=True)
        m_scr[...] = m_new
        acc_scr[...] = alpha * acc_scr[...] + _dot(p, v)

    kc_scr[...] = ckv_buf[slot].astype(BF16)
    for i in range(pages):
        krt_scr[:, i * page:(i + 1) * page] = krt_buf[slot, i].astype(BF16)
    kc = kc_scr[...]
    update(_dot_nt(q[:, :KV_LORA], kc) + _dot(q[:, KV_LORA:], krt_scr[...]), kc)

    @pl.when(j == pl.num_programs(1) - 1)
    def _():
        knew = knew_ref[0]
        s = _dot_nt(q, knew)
        qpos = lax.broadcasted_iota(jnp.int32, s.shape, 0) & (tpad - 1)
        kpos = lax.broadcasted_iota(jnp.int32, s.shape, 1)
        s = jnp.where((kpos <= qpos) & (kpos < tvalid), s, -jnp.inf)
        update(s, knew[:, :KV_LORA])
        o = acc_scr[...] / l_scr[...]
        for hd in range(MLA_HEADS):
            o_ref[:, hd * KV_LORA:(hd + 1) * KV_LORA] = o[hd * tpad:(hd + 1) * tpad, :]


def mla_sample(qcat, knew, ckv_pages, krt_pages, page_table, tpad, tvalid, pages):
    bsz, n_pages = page_table.shape
    page = ckv_pages.shape[1]
    steps = n_pages // pages
    rows = MLA_HEADS * tpad

    grid_spec = pltpu.PrefetchScalarGridSpec(
        num_scalar_prefetch=1,
        grid=(bsz, steps),
        in_specs=[pl.BlockSpec((MLA_HEADS, tpad, QK_CAT), lambda b, j, pt: (0, b, 0)),
                  pl.BlockSpec((1, knew.shape[1], QK_CAT), lambda b, j, pt: (b, 0, 0)),
                  pl.BlockSpec(memory_space=pl.ANY), pl.BlockSpec(memory_space=pl.ANY)],
        out_specs=pl.BlockSpec((tpad, MLA_HEADS * KV_LORA), lambda b, j, pt: (b, 0)),
        scratch_shapes=[pltpu.VMEM((2, pages * page, KV_LORA), F32),
                        pltpu.VMEM((2, pages, ROPE_DIM, page), F32),
                        pltpu.SemaphoreType.DMA((2,)),
                        pltpu.VMEM((pages * page, KV_LORA), BF16),
                        pltpu.VMEM((ROPE_DIM, pages * page), BF16),
                        pltpu.VMEM((rows, LANES), F32), pltpu.VMEM((rows, LANES), F32),
                        pltpu.VMEM((rows, KV_LORA), F32)])
    return pl.pallas_call(
        functools.partial(_mla_sample_kernel, pages=pages, page=page, tpad=tpad, tvalid=tvalid),
        grid_spec=grid_spec,
        out_shape=jax.ShapeDtypeStruct((bsz * tpad, MLA_HEADS * KV_LORA), F32),
        compiler_params=_params("arbitrary", "arbitrary"),
        name="mla_sample",
    )(page_table, qcat, knew, ckv_pages, krt_pages)


def _even_out_kernel(x_ref, ya_ref, olat_ref, sgb_ref, wbd_ref, woa_ref, wob_ref, gpost_ref, o_ref):
    yb = _dot(olat_ref[...], wbd_ref[...]) * sgb_ref[...]
    y = _dot(ya_ref[...], woa_ref[...]) + _dot(yb, wob_ref[...])
    o_ref[...] = x_ref[...] + _rms(y, gpost_ref[...])


def even_out(x, ya, olat, sgb, wts, tm):
    n = x.shape[0]
    row = lambda d: pl.BlockSpec((tm, d), lambda i: (i, 0))
    weights = (wts["wbd"], wts["woa"], wts["wob"], wts["gpost"])
    return pl.pallas_call(
        _even_out_kernel,
        grid=(n // tm,),
        in_specs=[row(D_MODEL), row(CONV_WIDTH), row(MLA_HEADS * KV_LORA), row(CONV_WIDTH)]
        + [_full(w.shape) for w in weights],
        out_specs=row(D_MODEL),
        out_shape=jax.ShapeDtypeStruct((n, D_MODEL), F32),
        compiler_params=_params("parallel"),
        name="even_out",
    )(x, ya, olat, sgb, *weights)


def _mem_kv_kernel(x_ref, g_ref, wk_ref, wv_ref, k_ref, v_ref, k4_ref, v4_ref):
    h = _rms(x_ref[...], g_ref[0]).astype(BF16)
    k = _dot(h, wk_ref[0])
    v = _dot(h, wv_ref[0])
    k_ref[0] = k
    v_ref[0] = v
    for hd in range(X_HEADS):
        cs = slice(hd * X_HEAD_DIM, (hd + 1) * X_HEAD_DIM)
        k4_ref[0, :, hd, :] = k[:, cs]
        v4_ref[0, :, hd, :] = v[:, cs]


def mem_kv(mem, g, wk, wv, tm):
    n = mem.shape[0]
    layers = g.shape[0]
    per_layer = lambda shape: pl.BlockSpec((1,) + shape, lambda l, i: (l,) + (0,) * len(shape))
    flat = pl.BlockSpec((1, tm, D_MODEL), lambda l, i: (l, i, 0))
    split = pl.BlockSpec((1, tm, X_HEADS, X_HEAD_DIM), lambda l, i: (l, i, 0, 0))
    return pl.pallas_call(
        _mem_kv_kernel,
        grid=(layers, n // tm),
        in_specs=[pl.BlockSpec((tm, D_MODEL), lambda l, i: (i, 0)), per_layer((1, D_MODEL)),
                  per_layer((D_MODEL, D_MODEL)), per_layer((D_MODEL, D_MODEL))],
        out_specs=(flat, flat, split, split),
        out_shape=(jax.ShapeDtypeStruct((layers, n, D_MODEL), F32),) * 2
        + (jax.ShapeDtypeStruct((layers, n, X_HEADS, X_HEAD_DIM), F32),) * 2,
        compiler_params=_params("parallel", "parallel"),
        name="mem_kv",
    )(mem, g, wk, wv)


def _xattn_kernel(x_ref, mk_ref, mv_ref, gpre_ref, wqg_ref, wo_ref, gpost_ref, o_ref, att_scr, *, bb, tq):
    x = x_ref[...].reshape(bb * tq, D_MODEL)
    qg = _dot(_rms(x, gpre_ref[...]), wqg_ref[...])
    q = qg[:, :D_MODEL] * (X_HEAD_DIM ** -0.5)
    split_heads = len(mk_ref.shape) == 4
    for b in range(bb):
        for hd in range(X_HEADS):
            cs = slice(hd * X_HEAD_DIM, (hd + 1) * X_HEAD_DIM)
            mk = mk_ref[b, :, hd, :] if split_heads else mk_ref[b, :, cs]
            mv = mv_ref[b, :, hd, :] if split_heads else mv_ref[b, :, cs]
            s = _dot_nt(q[b * tq:(b + 1) * tq, cs], mk)
            p = jnp.exp(s - jnp.max(s, axis=1, keepdims=True))
            o = _dot(p, mv) / jnp.sum(p, axis=1, keepdims=True)
            att_scr[b * tq:(b + 1) * tq, cs] = o
    y = _dot(att_scr[...] * _silu(qg[:, D_MODEL:]), wo_ref[...])
    o_ref[...] = (x + _rms(y, gpost_ref[...])).reshape(bb, tq, D_MODEL)


def xattn(x, mk, mv, layer, wts, bb, tq):
    bsz, t, _ = x.shape
    seq = pl.BlockSpec((bb, tq, D_MODEL), lambda b, i: (b, i, 0))
    tail = mk.shape[2:]
    mem = pl.BlockSpec((None, bb) + tail, lambda b, i: (layer, b) + (0,) * len(tail))
    weights = (wts["gprex"], wts["wqg"], wts["wo"], wts["gpostx"])
    return pl.pallas_call(
        functools.partial(_xattn_kernel, bb=bb, tq=tq),
        grid=(bsz // bb, t // tq),
        in_specs=[seq, mem, mem] + [_full(w.shape) for w in weights],
        out_specs=seq,
        out_shape=jax.ShapeDtypeStruct(x.shape, F32),
        scratch_shapes=[pltpu.VMEM((bb * tq, D_MODEL), F32)],
        compiler_params=_params("parallel", "arbitrary"),
        name="xattn",
    )(x, mk, mv, *weights)


def _odd_in_kernel(x_ref, gpre_ref, wz_ref, wxbc_ref, wdt_ref, wdtt_ref, brow_ref, bcol_ref,
                   sz_ref, xbc_ref, dt_ref, dtt_ref):
    h = _rms(x_ref[...], gpre_ref[...]).astype(BF16)
    sz_ref[...] = _silu(_dot(h, wz_ref[...])).astype(BF16)
    xbc_ref[...] = _dot(h, wxbc_ref[...])
    dt_ref[...] = jax.nn.softplus(_dot(h, wdt_ref[...]) + brow_ref[...])
    dtt_ref[...] = jax.nn.softplus(_dot_nt(wdtt_ref[...], h) + bcol_ref[...])


def odd_in(x, wts, tm):
    n = x.shape[0]
    row = lambda d: pl.BlockSpec((tm, d), lambda i: (i, 0))
    weights = (wts["gpre"], wts["wz"], wts["wxbc"], wts["wdt"], wts["wdtt"], wts["brow"], wts["bcol"])
    return pl.pallas_call(
        _odd_in_kernel,
        grid=(n // tm,),
        in_specs=[row(D_MODEL)] + [_full(w.shape) for w in weights],
        out_specs=(row(SSM_INNER), row(XBC_DIM), row(SSM_HEADS),
                   pl.BlockSpec((SSM_HEADS, tm), lambda i: (0, i))),
        out_shape=(jax.ShapeDtypeStruct((n, SSM_INNER), BF16), jax.ShapeDtypeStruct((n, XBC_DIM), F32),
                   jax.ShapeDtypeStruct((n, SSM_HEADS), F32), jax.ShapeDtypeStruct((SSM_HEADS, n), F32)),
        compiler_params=_params("parallel"),
        name="odd_in",
    )(x, *weights)


def _ssd_kernel(*refs, q, t_in, tvalid, has_init):
    if has_init:
        (xs_ref, bm_ref, cm_ref, dt_ref, dtt_ref, arow_ref, acol_ref, drow_ref, e64_ref, tri_ref,
         trit_ref, init_ref, y_ref, st_ref, state_scr) = refs
    else:
        (xs_ref, bm_ref, cm_ref, dt_ref, dtt_ref, arow_ref, acol_ref, drow_ref, e64_ref, tri_ref,
         trit_ref, y_ref, st_ref, state_scr) = refs
    c = pl.program_id(1)

    @pl.when(c == 0)
    def _():
        if has_init:
            state_scr[...] = init_ref[0].T
        else:
            state_scr[...] = jnp.zeros(state_scr.shape, F32)

    def rows(ref):
        v = ref[0]
        if t_in < q:
            v = jnp.concatenate([v, jnp.zeros((q - t_in, v.shape[1]), F32)], axis=0)
        return v

    xs, bm, cm = rows(xs_ref), rows(bm_ref), rows(cm_ref)
    dt = dt_ref[0]
    dtt = dtt_ref[...]
    if tvalid < q:
        dt = jnp.where(lax.broadcasted_iota(jnp.int32, dt.shape, 0) < tvalid, dt, 0.0)
        dtt = jnp.where(lax.broadcasted_iota(jnp.int32, dtt.shape, 1) < tvalid, dtt, 0.0)
    cum = _exact_dot_left(tri_ref[...], dt * arow_ref[...])
    cumt = _exact_dot(dtt * acol_ref[...], trit_ref[...])
    cum_last = cum[q - 1:q, :]
    causal = (lax.broadcasted_iota(jnp.int32, (q, q), 1) <= lax.broadcasted_iota(jnp.int32, (q, q), 0))
    lane = lax.broadcasted_iota(jnp.int32, (1, LANES), 1)
    left = lane < SSM_HEADDIM

    state = state_scr[...]
    y_parts = []
    for g in range(SSM_GROUPS):
        gs = slice(g * SSM_STATE, (g + 1) * SSM_STATE)
        cm_g = cm[:, gs]
        cb = _dot_nt(cm_g, bm[:, gs])
        for pr in range(SSM_HPG // 2):
            pair = g * (SSM_HPG // 2) + pr
            ps = slice(pair * LANES, (pair + 1) * LANES)
            rhs = jnp.concatenate([xs[:, ps], state[:, ps]], axis=0)
            y_pair = None
            for side in range(2):
                hd = 2 * pair + side
                ccol = jnp.broadcast_to(cum[:, hd:hd + 1], (q, q))
                seg = ccol - cumt[hd:hd + 1, :]
                w = cb * jnp.exp(jnp.where(causal, seg, -jnp.inf)) * dtt[hd:hd + 1, :]
                lhs = jnp.concatenate([w, cm_g * jnp.exp(ccol)], axis=1)
                keep = left if side == 0 else jnp.logical_not(left)
                part = _dot(lhs, jnp.where(keep, rhs, 0.0))
                y_pair = part if y_pair is None else y_pair + part
            y_parts.append(y_pair)
    y = jnp.concatenate(y_parts, axis=1) + drow_ref[...] * xs
    y_ref[0] = (y[:t_in] if t_in < q else y).astype(y_ref.dtype)

    to_end = jnp.exp(cum_last - cum) * dt
    toxs = _dot(to_end, e64_ref[...]) * xs
    dec = _exact_dot(jnp.broadcast_to(jnp.exp(cum_last), (SUBLANES, SSM_HEADS)), e64_ref[...])[0:1, :]
    new_parts = []
    for g in range(SSM_GROUPS):
        gs = slice(g * SSM_STATE, (g + 1) * SSM_STATE)
        hs = slice(g * GROUP_WIDTH, (g + 1) * GROUP_WIDTH)
        new_parts.append(_dot(bm[:, gs].T, toxs[:, hs]))
    new_state = state * dec + jnp.concatenate(new_parts, axis=1)
    state_scr[...] = new_state

    @pl.when(c == pl.num_programs(1) - 1)
    def _():
        st_ref[0] = new_state.T


def ssd(xbc, dt, dtt, wts, init, q, t_in, tvalid, y_dtype):
    bsz, t, _ = xbc.shape
    nc = max(t // q, 1)
    nxb = SSM_INNER // (SSM_GROUPS * SSM_STATE)
    in_specs = [pl.BlockSpec((1, t_in, SSM_INNER), lambda b, c: (b, c, 0)),
                pl.BlockSpec((1, t_in, SSM_GROUPS * SSM_STATE), lambda b, c: (b, c, nxb)),
                pl.BlockSpec((1, t_in, SSM_GROUPS * SSM_STATE), lambda b, c: (b, c, nxb + 1)),
                pl.BlockSpec((1, q, SSM_HEADS), lambda b, c: (b, c, 0)),
                pl.BlockSpec((SSM_HEADS, q), lambda b, c: (0, b * nc + c))]
    consts = (wts["arow"], wts["acol"], wts["drow"], wts["e64"], wts["tri"], wts["trit"])
    in_specs += [_full(w.shape) for w in consts]
    args = [xbc, xbc, xbc, dt, dtt, *consts]
    if init is not None:
        in_specs.append(pl.BlockSpec((1, SSM_INNER, SSM_STATE), lambda b, c: (b, 0, 0)))
        args.append(init)
    return pl.pallas_call(
        functools.partial(_ssd_kernel, q=q, t_in=t_in, tvalid=tvalid, has_init=init is not None),
        grid=(bsz, nc),
        in_specs=in_specs,
        out_specs=(pl.BlockSpec((1, t_in, SSM_INNER), lambda b, c: (b, c, 0)),
                   pl.BlockSpec((1, SSM_INNER, SSM_STATE), lambda b, c: (b, 0, 0))),
        out_shape=(jax.ShapeDtypeStruct((bsz, t, SSM_INNER), y_dtype),
                   jax.ShapeDtypeStruct((bsz, SSM_INNER, SSM_STATE), F32)),
        scratch_shapes=[pltpu.VMEM((SSM_STATE, SSM_INNER), F32)],
        compiler_params=_params("parallel", "arbitrary"),
        name="ssd",
    )(*args)


def _odd_out_kernel(x_ref, y_ref, sz_ref, gn_ref, wout_ref, gpost_ref, o_ref):
    v = y_ref[...].astype(F32) * sz_ref[...].astype(F32)
    gn = gn_ref[...]
    parts = []
    for g in range(SSM_GROUPS):
        gs = slice(g * GROUP_WIDTH, (g + 1) * GROUP_WIDTH)
        parts.append(_rms(v[:, gs], gn[:, gs]).astype(BF16))
    y = _dot(jnp.concatenate(parts, axis=1), wout_ref[...])
    o_ref[...] = x_ref[...] + _rms(y, gpost_ref[...])


def odd_out(x, y, sz, wts, tm):
    n = x.shape[0]
    row = lambda d: pl.BlockSpec((tm, d), lambda i: (i, 0))
    weights = (wts["gnorm"], wts["wout"], wts["gpost"])
    return pl.pallas_call(
        _odd_out_kernel,
        grid=(n // tm,),
        in_specs=[row(D_MODEL), row(SSM_INNER), row(SSM_INNER)] + [_full(w.shape) for w in weights],
        out_specs=row(D_MODEL),
        out_shape=jax.ShapeDtypeStruct((n, D_MODEL), F32),
        compiler_params=_params("parallel"),
        name="odd_out",
    )(x, y, sz, *weights)


def _even_weights(i, l, p):
    w_in = p["e_w_in"][l]
    c = CONV_WIDTH
    o_q, o_kv, o_bg = 3 * c, 3 * c + Q_LORA, 3 * c + Q_LORA + KV_LORA + ROPE_DIM
    half = ROPE_DIM // 2
    w_kr = w_in[:, o_kv + KV_LORA:o_bg]
    w_uq = p["e_w_uq"][l]
    uq_r = w_uq[:, :, NOPE_DIM:]
    w_uv = jnp.transpose(p["e_w_uv"][l], (1, 0, 2))
    eye = jnp.eye(MLA_HEADS, dtype=F32)
    wbd = (eye[:, None, :, None] * w_uv[:, :, None, :]).reshape(MLA_HEADS * KV_LORA, MLA_HEADS * V_DIM)
    w_out = p["e_w_out"][l]
    return {
        "gpre": p["norm_pre_mix"][i][None], "gpost": p["norm_post_mix"][i][None],
        "wa": w_in[:, :o_q].astype(BF16), "wq": w_in[:, o_q:o_kv].astype(BF16),
        "wkv": jnp.concatenate([w_in[:, o_kv:o_bg], -w_kr[:, half:], w_kr[:, :half]], axis=1).astype(BF16),
        "wbg": w_in[:, o_bg:].astype(BF16),
        "gq": p["e_q_norm"][l][None], "gkv": p["e_kv_norm"][l][None],
        "wuqn": w_uq[:, :, :NOPE_DIM].reshape(Q_LORA, MLA_HEADS * NOPE_DIM).astype(BF16),
        "wuqr": uq_r.reshape(Q_LORA, MLA_HEADS * ROPE_DIM).astype(BF16),
        "wuqrot": jnp.concatenate([-uq_r[:, :, half:], uq_r[:, :, :half]], axis=2)
        .reshape(Q_LORA, MLA_HEADS * ROPE_DIM).astype(BF16),
        "wukt": jnp.transpose(p["e_w_uk"][l], (1, 2, 0)).astype(BF16),
        "conv_w": p["e_conv_w"][l], "conv_b": p["e_conv_b"][l][None],
        "ln_w": p["e_ln_w"][l][None], "ln_b": p["e_ln_b"][l][None],
        "wbd": wbd.astype(BF16), "woa": w_out[:c].astype(BF16), "wob": w_out[c:].astype(BF16),
    }


def _odd_weights(i, l, p):
    w_in = p["o_w_in"][l]
    w_dt = w_in[:, SSM_INNER + XBC_DIM:]
    a = -jnp.exp(p["o_a_log"][l])
    q = SSM_CHUNK
    tri = jnp.tril(jnp.ones((q, q), F32))
    return {
        "gpre": p["norm_pre_mix"][i][None], "gpost": p["norm_post_mix"][i][None],
        "wz": w_in[:, :SSM_INNER].astype(BF16),
        "wxbc": w_in[:, SSM_INNER:SSM_INNER + XBC_DIM].astype(BF16),
        "wdt": w_dt.astype(BF16), "wdtt": w_dt.T.astype(BF16),
        "brow": p["o_dt_bias"][l][None], "bcol": p["o_dt_bias"][l][:, None],
        "conv_w": p["o_conv_w"][l], "conv_b": p["o_conv_b"][l][None],
        "arow": a[None], "acol": a[:, None],
        "drow": jnp.repeat(p["o_d"][l], SSM_HEADDIM)[None],
        "e64": jnp.repeat(jnp.eye(SSM_HEADS, dtype=F32), SSM_HEADDIM, axis=1).astype(BF16),
        "tri": tri.astype(BF16), "trit": tri.T.astype(BF16),
        "gnorm": p["o_norm"][l][None], "wout": p["o_w_out"][l].astype(BF16),
    }


def _xattn_weights(i, p):
    return {"gprex": p["norm_pre_x"][i][None], "gpostx": p["norm_post_x"][i][None],
            "wqg": p["x_w_qg"][i].astype(BF16), "wo": p["x_w_o"][i].astype(BF16)}


def kernel(x_prompt, x_sample, mem_prompt, cache_ckv, cache_krope, page_table, state_conv_a,
           state_conv_c, state_ssm, cache_mem_k, cache_mem_v, norm_pre_mix, norm_post_mix,
           norm_pre_x, norm_post_x, norm_mem, x_w_qg, x_w_k, x_w_v, x_w_o, e_w_in, e_conv_w,
           e_conv_b, e_ln_w, e_ln_b, e_q_norm, e_kv_norm, e_w_uq, e_w_uk, e_w_uv, e_w_out,
           o_w_in, o_conv_w, o_conv_b, o_dt_bias, o_a_log, o_d, o_norm, o_w_out):
    p = dict(norm_pre_mix=norm_pre_mix, norm_post_mix=norm_post_mix, norm_pre_x=norm_pre_x,
             norm_post_x=norm_post_x, e_w_in=e_w_in, e_conv_w=e_conv_w, e_conv_b=e_conv_b,
             e_ln_w=e_ln_w, e_ln_b=e_ln_b, e_q_norm=e_q_norm, e_kv_norm=e_kv_norm, e_w_uq=e_w_uq,
             e_w_uk=e_w_uk, e_w_uv=e_w_uv, e_w_out=e_w_out, o_w_in=o_w_in, o_conv_w=o_conv_w,
             o_conv_b=o_conv_b, o_dt_bias=o_dt_bias, o_a_log=o_a_log, o_d=o_d, o_norm=o_norm,
             o_w_out=o_w_out, x_w_qg=x_w_qg, x_w_o=x_w_o)
    bp, sp, _ = x_prompt.shape
    bs, ss, _ = x_sample.shape
    depth = norm_pre_mix.shape[0]
    n_pages, page = page_table.shape[1], cache_ckv.shape[1]
    past_len = n_pages * page
    tpad = SUBLANES
    assert ss <= tpad and sp % SSM_CHUNK == 0
    np_, ns = bp * sp, bs * tpad
    tm_p, tm_s = 512, ns

    half = ROPE_DIM // 2
    inv = ROPE_THETA ** (-jnp.arange(half, dtype=F32) / half)
    inv_lanes = jnp.tile(inv, LANES // half)[None]
    cos_p, sin_p = rope_table(inv_lanes, sp, 0, sp)
    cos_s, sin_s = rope_table(inv_lanes, ns, past_len, tpad)

    xp = x_prompt.reshape(np_, D_MODEL)
    xs = jnp.pad(x_sample, ((0, 0), (0, tpad - ss), (0, 0))).reshape(ns, D_MODEL)
    mem_flat = mem_prompt.reshape(bp * MEM_LEN, D_MODEL)
    mem_k_s, mem_v_s = cache_mem_k, cache_mem_v
    mk, mv, mk4, mv4 = mem_kv(mem_flat, norm_mem[:, None], x_w_k.astype(BF16), x_w_v.astype(BF16), 512)
    mem_k_p = mk.reshape(depth, bp, MEM_LEN, D_MODEL)
    mem_v_p = mv.reshape(depth, bp, MEM_LEN, D_MODEL)

    outs = {k: [] for k in ("ckv_p", "ckv_s", "kr_p", "kr_s", "ca_p", "ca_s", "cc_p", "cc_s",
                            "ssm_p", "ssm_s", "mk", "mv")}
    for i in range(depth):
        l = i // 2
        if i % 2 == 0:
            w = _even_weights(i, l, p)
            glu, sga, sgb, qcat, kcat, ckv, kr = even_in(xp, w, cos_p, sin_p, tm_p)
            glu3 = glu.reshape(bp, sp, CONV_WIDTH)
            ya = causal_conv(glu3, jnp.zeros((bp, CONV_A_HALO, CONV_WIDTH), F32), w["conv_w"], w["conv_b"],
                             256, CONV_A_HALO, CONV_WIDTH,
                             ln=(w["ln_w"], w["ln_b"], sga.reshape(bp, sp, CONV_WIDTH)), out_dtype=BF16)
            olat = mla_prompt(qcat, kcat.reshape(bp, sp, QK_CAT), bp, sp, 256)
            yp = even_out(xp, ya.reshape(np_, CONV_WIDTH), olat, sgb, w, tm_p)
            outs["ca_p"].append(glu3[:, sp - (CONV_K - 1):])
            outs["ckv_p"].append(ckv.reshape(bp, sp, KV_LORA))
            outs["kr_p"].append(kr.reshape(bp, sp, ROPE_DIM))
            glu, sga, sgb, qcat, kcat, ckv, kr = even_in(xs, w, cos_s, sin_s, tm_s)
            glu3 = glu.reshape(bs, tpad, CONV_WIDTH)
            hist = state_conv_a[l]
            hist_pad = jnp.pad(hist, ((0, 0), (CONV_A_HALO - (CONV_K - 1), 0), (0, 0)))
            ya = causal_conv(glu3, hist_pad, w["conv_w"], w["conv_b"], tpad, CONV_A_HALO, CONV_WIDTH,
                             ln=(w["ln_w"], w["ln_b"], sga.reshape(bs, tpad, CONV_WIDTH)))
            knew = jnp.pad(kcat.reshape(bs, tpad, QK_CAT), ((0, 0), (0, LANES - tpad), (0, 0)))
            olat = mla_sample(qcat.astype(F32), knew, cache_ckv[:, :, l, :],
                              jnp.transpose(cache_krope[:, :, l, :], (0, 2, 1)), page_table, tpad, ss,
                              min(32, n_pages))
            ys = even_out(xs, ya.reshape(ns, CONV_WIDTH), olat, sgb, w, tm_s)
            outs["ca_s"].append(jnp.concatenate([hist[:, ss:], glu3[:, :ss]], axis=1))
            outs["ckv_s"].append(ckv.reshape(bs, tpad, KV_LORA)[:, :ss])
            outs["kr_s"].append(kr.reshape(bs, tpad, ROPE_DIM)[:, :ss])
        else:
            w = _odd_weights(i, l, p)
            q = SSM_CHUNK
            sz, xbc, dt, dtt = odd_in(xp, w, 256)
            xbc3 = xbc.reshape(bp, sp, XBC_DIM)
            xbc_c = causal_conv(xbc3, jnp.zeros((bp, CONV_C_HALO, XBC_DIM), F32), w["conv_w"], w["conv_b"],
                                256, CONV_C_HALO, 512)
            y, st = ssd(xbc_c, dt.reshape(bp, sp, SSM_HEADS), dtt, w, None, q, q, q, BF16)
            yp = odd_out(xp, y.reshape(np_, SSM_INNER), sz, w, tm_p)
            outs["cc_p"].append(xbc3[:, sp - (SSM_CONV - 1):])
            outs["ssm_p"].append(st.reshape(bp, SSM_HEADS, SSM_HEADDIM, SSM_STATE))
            sz, xbc, dt, dtt = odd_in(xs, w, tm_s)
            xbc3 = xbc.reshape(bs, tpad, XBC_DIM)
            hist = state_conv_c[l]
            hist_pad = jnp.pad(hist, ((0, 0), (CONV_C_HALO - (SSM_CONV - 1), 0), (0, 0)))
            xbc_c = causal_conv(xbc3, hist_pad, w["conv_w"], w["conv_b"], tpad, CONV_C_HALO, 512)
            dt_pad = jnp.pad(dt.reshape(bs, tpad, SSM_HEADS), ((0, 0), (0, q - tpad), (0, 0)))
            dtt_pad = jnp.pad(dtt.reshape(SSM_HEADS, bs, tpad), ((0, 0), (0, 0), (0, q - tpad)))
            y, st = ssd(xbc_c, dt_pad, dtt_pad.reshape(SSM_HEADS, bs * q), w,
                        state_ssm[l].reshape(bs, SSM_INNER, SSM_STATE), q, tpad, ss, F32)
            ys = odd_out(xs, y.reshape(ns, SSM_INNER), sz, w, tm_s)
            outs["cc_s"].append(jnp.concatenate([hist[:, ss:], xbc3[:, :ss]], axis=1)[:, -(SSM_CONV - 1):])
            outs["ssm_s"].append(st.reshape(bs, SSM_HEADS, SSM_HEADDIM, SSM_STATE))
        xp, xs = yp, ys
        wx = _xattn_weights(i, p)
        xp = xattn(xp.reshape(bp, sp, D_MODEL), mem_k_p, mem_v_p, i, wx, 1, 512).reshape(np_, D_MODEL)
        xs = xattn(xs.reshape(bs, tpad, D_MODEL), mem_k_s, mem_v_s, i, wx, 4, tpad).reshape(ns, D_MODEL)

    return (xp.reshape(bp, sp, D_MODEL), xs.reshape(bs, tpad, D_MODEL)[:, :ss],
            jnp.stack(outs["ckv_p"], axis=2), jnp.stack(outs["ckv_s"], axis=2),
            jnp.stack(outs["kr_p"], axis=2), jnp.stack(outs["kr_s"], axis=2),
            jnp.stack(outs["ca_p"]), jnp.stack(outs["ca_s"]),
            jnp.stack(outs["cc_p"]), jnp.stack(outs["cc_s"]),
            jnp.stack(outs["ssm_p"]), jnp.stack(outs["ssm_s"]),
            mk4.reshape(depth, bp, MEM_LEN, X_HEADS, X_HEAD_DIM),
            mv4.reshape(depth, bp, MEM_LEN, X_HEADS, X_HEAD_DIM))
```

```python
import functools

import jax
import jax.numpy as jnp
from jax import lax
from jax.experimental import pallas as pl
from jax.experimental.pallas import tpu as pltpu

F32 = jnp.float32
BF16 = jnp.bfloat16

D_MODEL = 1024
CONV_WIDTH = 512
CONV_K = 31
MLA_HEADS = 8
Q_LORA = 256
KV_LORA = 128
NOPE_DIM = 64
ROPE_DIM = 32
V_DIM = 64
QK_CAT = KV_LORA + ROPE_DIM
LOG2E = 1.4426950408889634
MLA_SCALE = (NOPE_DIM + ROPE_DIM) ** -0.5 * LOG2E
ROPE_THETA = 10000.0
SSM_INNER = 2048
SSM_HEADDIM = 64
SSM_HEADS = 32
SSM_GROUPS = 4
SSM_HPG = 8
SSM_STATE = 128
SSM_CONV = 4
SSM_CHUNK = 128
XBC_DIM = SSM_INNER + 2 * SSM_GROUPS * SSM_STATE
GROUP_WIDTH = SSM_INNER // SSM_GROUPS
MEM_LEN = 256
X_HEADS = 4
X_HEAD_DIM = 256
EPS = 1e-6

LANES = 128
SUBLANES = 8
CONV_A_HALO = 32
CONV_C_HALO = 8
VMEM_LIMIT = 56 * 1024 * 1024


def _params(*sem):
    return pltpu.CompilerParams(dimension_semantics=sem, vmem_limit_bytes=VMEM_LIMIT)


def _rms(x, w):
    return x * lax.rsqrt(jnp.mean(x * x, axis=-1, keepdims=True) + EPS) * w


def _silu(x):
    return x * jax.nn.sigmoid(x)


def _dot(a, b):
    return jnp.dot(a.astype(BF16), b.astype(BF16), preferred_element_type=F32)


def _dot_nt(a, b):
    return lax.dot_general(a.astype(BF16), b.astype(BF16), (((1,), (1,)), ((), ())),
                           preferred_element_type=F32)


def _split3(x):
    hi = x.astype(BF16)
    r = x - hi.astype(F32)
    mid = r.astype(BF16)
    lo = (r - mid.astype(F32)).astype(BF16)
    return hi, mid, lo


def _exact_dot(x, e):
    hi, mid, lo = _split3(x)
    return (jnp.dot(hi, e, preferred_element_type=F32) + jnp.dot(mid, e, preferred_element_type=F32)
            + jnp.dot(lo, e, preferred_element_type=F32))


def _exact_dot_left(e, x):
    hi, mid, lo = _split3(x)
    return (jnp.dot(e, hi, preferred_element_type=F32) + jnp.dot(e, mid, preferred_element_type=F32)
            + jnp.dot(e, lo, preferred_element_type=F32))


def _full(shape):
    nd = len(shape)
    return pl.BlockSpec(shape, lambda *_: (0,) * nd)


def _rope_table_kernel(inv_ref, cos_ref, sin_ref, *, base, period):
    row = lax.broadcasted_iota(jnp.int32, cos_ref.shape, 0)
    pos = (base + (row & (period - 1))).astype(F32)
    ang = pos * inv_ref[...]
    cos_ref[...] = jnp.cos(ang)
    sin_ref[...] = jnp.sin(ang)


def rope_table(inv_lanes, rows, base, period):
    assert period & (period - 1) == 0
    return pl.pallas_call(
        functools.partial(_rope_table_kernel, base=base, period=period),
        out_shape=(jax.ShapeDtypeStruct((rows, LANES), F32),) * 2,
        name="rope_table",
    )(inv_lanes)


def _even_in_kernel(x_ref, gpre_ref, wa_ref, wq_ref, wkv_ref, wbg_ref, gq_ref, gkv_ref,
                    wuqn_ref, wuqr_ref, wuqrot_ref, wukt_ref, cos_ref, sin_ref,
                    glu_ref, sga_ref, sgb_ref, qcat_ref, kcat_ref, ckv_ref, kr_ref):
    h = _rms(x_ref[...], gpre_ref[...]).astype(BF16)
    ua = _dot(h, wa_ref[...])
    glu_ref[...] = ua[:, :CONV_WIDTH] * jax.nn.sigmoid(ua[:, CONV_WIDTH:2 * CONV_WIDTH])
    sga_ref[...] = _silu(ua[:, 2 * CONV_WIDTH:])
    sgb_ref[...] = _silu(_dot(h, wbg_ref[...])).astype(BF16)
    cos = cos_ref[...]
    sin = sin_ref[...]
    kv = _dot(h, wkv_ref[...])
    ckv = _rms(kv[:, :KV_LORA], gkv_ref[...])
    kr = (kv[:, KV_LORA:KV_LORA + ROPE_DIM] * cos[:, :ROPE_DIM]
          + kv[:, KV_LORA + ROPE_DIM:] * sin[:, :ROPE_DIM])
    ckv_ref[...] = ckv
    kr_ref[...] = kr
    kcat_ref[:, :KV_LORA] = ckv.astype(BF16)
    kcat_ref[:, KV_LORA:] = kr.astype(BF16)
    qn = _rms(_dot(h, wq_ref[...]), gq_ref[...]).astype(BF16)
    qnope = _dot(qn, wuqn_ref[...])
    cos2 = jnp.concatenate([cos, cos], axis=1)
    sin2 = jnp.concatenate([sin, sin], axis=1)
    qr = (_dot(qn, wuqr_ref[...]) * cos2 + _dot(qn, wuqrot_ref[...]) * sin2) * MLA_SCALE
    for hd in range(MLA_HEADS):
        ql = _dot(qnope[:, hd * NOPE_DIM:(hd + 1) * NOPE_DIM], wukt_ref[hd]) * MLA_SCALE
        qcat_ref[hd, :, :KV_LORA] = ql.astype(BF16)
        qcat_ref[hd, :, KV_LORA:] = qr[:, hd * ROPE_DIM:(hd + 1) * ROPE_DIM].astype(BF16)


def even_in(x, wts, cos, sin, tm):
    n = x.shape[0]
    nblk = cos.shape[0] // tm
    row = lambda d: pl.BlockSpec((tm, d), lambda i: (i, 0))
    tab = pl.BlockSpec((tm, LANES), lambda i: (i % nblk, 0))
    weights = (wts["gpre"], wts["wa"], wts["wq"], wts["wkv"], wts["wbg"], wts["gq"], wts["gkv"],
               wts["wuqn"], wts["wuqr"], wts["wuqrot"], wts["wukt"])
    return pl.pallas_call(
        _even_in_kernel,
        grid=(n // tm,),
        in_specs=[row(D_MODEL)] + [_full(w.shape) for w in weights] + [tab, tab],
        out_specs=(row(CONV_WIDTH), row(CONV_WIDTH), row(CONV_WIDTH),
                   pl.BlockSpec((MLA_HEADS, tm, QK_CAT), lambda i: (0, i, 0)),
                   row(QK_CAT), row(KV_LORA), row(ROPE_DIM)),
        out_shape=(jax.ShapeDtypeStruct((n, CONV_WIDTH), F32),) * 2 + (
            jax.ShapeDtypeStruct((n, CONV_WIDTH), BF16),
            jax.ShapeDtypeStruct((MLA_HEADS, n, QK_CAT), BF16),
            jax.ShapeDtypeStruct((n, QK_CAT), BF16),
            jax.ShapeDtypeStruct((n, KV_LORA), F32),
            jax.ShapeDtypeStruct((n, ROPE_DIM), F32)),
        compiler_params=_params("parallel"),
        name="even_in",
    )(x, *weights, cos, sin)


def _conv_kernel(hist_ref, prev_ref, cur_ref, w_ref, b_ref, *rest, taps, halo, row_chunk, lane_chunk,
                 layernorm):
    if layernorm:
        lnw_ref, lnb_ref, gate_ref, out_ref, ext_scr, *shift_scr = rest
    else:
        out_ref, ext_scr, *shift_scr = rest
    tt, width = cur_ref.shape[1], cur_ref.shape[2]
    ext_scr[0:halo] = jnp.where(pl.program_id(1) == 0, hist_ref[0], prev_ref[0])
    ext_scr[halo:halo + tt] = cur_ref[0]
    off = halo - (taps - 1)
    if shift_scr:
        sh_scr, = shift_scr
        span = sh_scr.shape[1]
        for r in range(1, SUBLANES):
            sh_scr[r - 1] = ext_scr[r:r + span]

    def ext_rows(d, r0, cs):
        a, r = divmod(d, SUBLANES)
        if not shift_scr or r == 0:
            return ext_scr[d + r0:d + r0 + row_chunk, cs]
        return sh_scr[r - 1, SUBLANES * a + r0:SUBLANES * a + r0 + row_chunk, cs]

    for r0 in range(0, tt, row_chunk):
        for c0 in range(0, width, lane_chunk):
            cs = slice(c0, c0 + lane_chunk)
            if shift_scr:
                taps_in = [ext_rows(off + k, r0, cs) for k in range(taps)]
            else:
                win = ext_scr[r0 + halo - SUBLANES:r0 + halo + row_chunk, cs]
                taps_in = [(pltpu.roll(win, taps - 1 - k, axis=0) if k < taps - 1 else win)[SUBLANES:]
                           for k in range(taps)]
            acc = w_ref[0:1, cs] * taps_in[0]
            for k in range(1, taps):
                acc = acc + w_ref[k:k + 1, cs] * taps_in[k]
            acc = acc + b_ref[:, cs]
            if layernorm:
                xc = acc - jnp.mean(acc, axis=-1, keepdims=True)
                var = jnp.mean(xc * xc, axis=-1, keepdims=True)
                ln = xc * lax.rsqrt(var + EPS) * lnw_ref[...] + lnb_ref[...]
                out_ref[0, r0:r0 + row_chunk, :] = (_silu(ln) * gate_ref[0, r0:r0 + row_chunk, :]
                                                    ).astype(out_ref.dtype)
            else:
                out_ref[0, r0:r0 + row_chunk, cs] = _silu(acc)


def causal_conv(x, hist, w, b, tt, halo, lane_chunk, ln=None, out_dtype=F32):
    bsz, t, width = x.shape
    taps = w.shape[0]
    seq = pl.BlockSpec((1, tt, width), lambda bi, ti: (bi, ti, 0))
    first = pl.BlockSpec((1, halo, width), lambda bi, ti: (bi, 0, 0))
    if t == tt:
        prev, prev_spec = hist, first
    else:
        ratio = tt // halo
        prev = x
        prev_spec = pl.BlockSpec((1, halo, width), lambda bi, ti: (bi, jnp.maximum(ti * ratio - 1, 0), 0))
    in_specs = [first, prev_spec, seq, _full(w.shape), _full(b.shape)]
    args = [hist, prev, x, w, b]
    if ln is not None:
        in_specs += [_full(ln[0].shape), _full(ln[1].shape), seq]
        args += list(ln)
    return pl.pallas_call(
        functools.partial(_conv_kernel, taps=taps, halo=halo, row_chunk=min(32, tt),
                          lane_chunk=lane_chunk, layernorm=ln is not None),
        grid=(bsz, t // tt),
        in_specs=in_specs,
        out_specs=seq,
        out_shape=jax.ShapeDtypeStruct((bsz, t, width), out_dtype),
        scratch_shapes=[pltpu.VMEM((halo + tt, width), F32)]
        + ([pltpu.VMEM((SUBLANES - 1, halo + tt - SUBLANES, width), F32)] if taps > SUBLANES else []),
        compiler_params=_params("parallel", "arbitrary"),
        name="causal_conv_ln" if ln is not None else "causal_conv",
    )(*args)


def _mla_prompt_kernel(q_ref, k_ref, o_ref, s_scr, acc_scr, mx_scr, sum_scr, *, tq):
    i = pl.program_id(1)
    rows = MLA_HEADS * tq
    q = q_ref[...].reshape(rows, QK_CAT)

    def keys(j):
        return k_ref[0, pl.ds(pl.multiple_of(j * tq, tq), tq), :]

    def fold(x):
        parts = [x[:, c:c + LANES] for c in range(0, tq, LANES)]
        return parts

    mx_scr[...] = jnp.full(mx_scr.shape, -jnp.inf, F32)

    def pass1(j, carry):
        s = _dot_nt(q, keys(j))
        s_scr[j] = s
        mx_scr[...] = functools.reduce(jnp.maximum, fold(s), mx_scr[...])
        return carry

    lax.fori_loop(0, i, pass1, 0)
    s = _dot_nt(q, keys(i))
    qpos = lax.broadcasted_iota(jnp.int32, s.shape, 0) & (tq - 1)
    kpos = lax.broadcasted_iota(jnp.int32, s.shape, 1)
    s = jnp.where(kpos <= qpos, s, -jnp.inf)
    s_scr[i] = s
    m = jnp.max(functools.reduce(jnp.maximum, fold(s), mx_scr[...]), axis=1, keepdims=True)

    acc_scr[...] = jnp.zeros(acc_scr.shape, F32)
    sum_scr[...] = jnp.zeros(sum_scr.shape, F32)

    def pass2(j, carry):
        p = jnp.exp2(s_scr[j] - m)
        acc_scr[...] += _dot(p, keys(j)[:, :KV_LORA])
        sum_scr[...] = functools.reduce(jnp.add, fold(p), sum_scr[...])
        return carry

    lax.fori_loop(0, i + 1, pass2, 0)
    o = acc_scr[...] / jnp.sum(sum_scr[...], axis=1, keepdims=True)
    for hd in range(MLA_HEADS):
        o_ref[:, hd * KV_LORA:(hd + 1) * KV_LORA] = o[hd * tq:(hd + 1) * tq, :].astype(BF16)


def mla_prompt(qcat, kcat, bsz, t, tq):
    nq = t // tq
    rows = MLA_HEADS * tq
    return pl.pallas_call(
        functools.partial(_mla_prompt_kernel, tq=tq),
        grid=(bsz, nq),
        in_specs=[pl.BlockSpec((MLA_HEADS, tq, QK_CAT), lambda b, i: (0, b * nq + i, 0)),
                  pl.BlockSpec((1, t, QK_CAT), lambda b, i: (b, 0, 0))],
        out_specs=pl.BlockSpec((tq, MLA_HEADS * KV_LORA), lambda b, i: (b * nq + i, 0)),
        out_shape=jax.ShapeDtypeStruct((bsz * t, MLA_HEADS * KV_LORA), BF16),
        scratch_shapes=[pltpu.VMEM((nq, rows, tq), F32), pltpu.VMEM((rows, KV_LORA), F32),
                        pltpu.VMEM((rows, LANES), F32), pltpu.VMEM((rows, LANES), F32)],
        compiler_params=_params("parallel", "arbitrary"),
        name="mla_prompt",
    )(qcat, kcat)


def _mla_sample_kernel(pt_ref, q_ref, knew_ref, ckv_hbm, krt_hbm, o_ref, ckv_buf, krt_buf, sem,
                       kc_scr, krt_scr, m_scr, l_scr, acc_scr, *, pages, page, tvalid):
    b = pl.program_id(0)
    j = pl.program_id(1)
    steps = pl.num_programs(1)
    g = b * steps + j
    slot = g & 1

    def page_copies(bi, ji, sl):
        copies = []
        for i in range(pages):
            pg = pt_ref[bi, ji * pages + i]
            copies.append(pltpu.make_async_copy(ckv_hbm.at[pg], ckv_buf.at[sl, pl.ds(i * page, page)],
                                                sem.at[sl]))
            copies.append(pltpu.make_async_copy(krt_hbm.at[pg], krt_buf.at[sl, i], sem.at[sl]))
        return copies

    def start_all(copies):
        for n, cp in enumerate(copies):
            cp.start(priority=(n // 2) % 2)

    @pl.when(g == 0)
    def _():
        start_all(page_copies(b, j, slot))

    @pl.when(g + 1 < pl.num_programs(0) * steps)
    def _():
        wrap = j + 1 == steps
        start_all(page_copies(jnp.where(wrap, b + 1, b), jnp.where(wrap, 0, j + 1), 1 - slot))

    for cp in page_copies(b, j, slot):
        cp.wait()

    @pl.when(j == 0)
    def _():
        m_scr[...] = jnp.full(m_scr.shape, -jnp.inf, F32)
        l_scr[...] = jnp.zeros(l_scr.shape, F32)
        acc_scr[...] = jnp.zeros(acc_scr.shape, F32)

    q = q_ref[0]

    def update(s, v):
        m_prev = m_scr[...]
        m_new = jnp.maximum(m_prev, jnp.max(s, axis=1, keepdims=True))
        alpha = jnp.exp2(m_prev - m_new)
        p = jnp.exp2(s - m_new[:, 0:1])
        l_scr[...] = alpha * l_scr[...] + jnp.sum(p, axis=1, keepdims=True)
        m_scr[...] = m_new
        acc_scr[...] = alpha * acc_scr[...] + _dot(p, v)

    kc_scr[...] = ckv_buf[slot].astype(BF16)
    for i in range(pages):
        krt_scr[:, i * page:(i + 1) * page] = krt_buf[slot, i].astype(BF16)
    kc = kc_scr[...]
    update(_dot_nt(q[:, :KV_LORA], kc) + _dot(q[:, KV_LORA:], krt_scr[...]), kc)

    @pl.when(j == pl.num_programs(1) - 1)
    def _():
        knew = knew_ref[0]
        s = _dot_nt(q, knew)
        qpos = lax.rem(lax.broadcasted_iota(jnp.int32, s.shape, 0), tvalid)
        kpos = lax.broadcasted_iota(jnp.int32, s.shape, 1)
        s = jnp.where((kpos <= qpos) & (kpos < tvalid), s, -jnp.inf)
        update(s, knew[:, :KV_LORA])
        o_ref[0] = acc_scr[...] / l_scr[...]


def mla_sample(qcat, knew, ckv_pages, krt_pages, page_table, tpad, tvalid, pages):
    bsz, n_pages = page_table.shape
    page = ckv_pages.shape[1]
    steps = n_pages // pages
    rows = MLA_HEADS * tvalid
    q = qcat.astype(F32).reshape(MLA_HEADS, bsz, tpad, QK_CAT)[:, :, :tvalid]
    q = jnp.transpose(q, (1, 0, 2, 3)).reshape(bsz, rows, QK_CAT)

    grid_spec = pltpu.PrefetchScalarGridSpec(
        num_scalar_prefetch=1,
        grid=(bsz, steps),
        in_specs=[pl.BlockSpec((1, rows, QK_CAT), lambda b, j, pt: (b, 0, 0)),
                  pl.BlockSpec((1, knew.shape[1], QK_CAT), lambda b, j, pt: (b, 0, 0)),
                  pl.BlockSpec(memory_space=pl.ANY), pl.BlockSpec(memory_space=pl.ANY)],
        out_specs=pl.BlockSpec((1, rows, KV_LORA), lambda b, j, pt: (b, 0, 0)),
        scratch_shapes=[pltpu.VMEM((2, pages * page, KV_LORA), F32),
                        pltpu.VMEM((2, pages, ROPE_DIM, page), F32),
                        pltpu.SemaphoreType.DMA((2,)),
                        pltpu.VMEM((pages * page, KV_LORA), BF16),
                        pltpu.VMEM((ROPE_DIM, pages * page), BF16),
                        pltpu.VMEM((rows, LANES), F32), pltpu.VMEM((rows, LANES), F32),
                        pltpu.VMEM((rows, KV_LORA), F32)])
    o = pl.pallas_call(
        functools.partial(_mla_sample_kernel, pages=pages, page=page, tvalid=tvalid),
        grid_spec=grid_spec,
        out_shape=jax.ShapeDtypeStruct((bsz, rows, KV_LORA), F32),
        compiler_params=_params("arbitrary", "arbitrary"),
        name="mla_sample",
    )(page_table, q, knew, ckv_pages, krt_pages)
    o = jnp.transpose(o.reshape(bsz, MLA_HEADS, tvalid, KV_LORA), (0, 2, 1, 3))
    o = jnp.pad(o.reshape(bsz, tvalid, MLA_HEADS * KV_LORA), ((0, 0), (0, tpad - tvalid), (0, 0)))
    return o.reshape(bsz * tpad, MLA_HEADS * KV_LORA)


def _even_out_kernel(x_ref, ya_ref, olat_ref, sgb_ref, wbd_ref, woa_ref, wob_ref, gpost_ref, o_ref):
    yb = _dot(olat_ref[...], wbd_ref[...]) * sgb_ref[...]
    y = _dot(ya_ref[...], woa_ref[...]) + _dot(yb, wob_ref[...])
    o_ref[...] = x_ref[...] + _rms(y, gpost_ref[...])


def even_out(x, ya, olat, sgb, wts, tm):
    n = x.shape[0]
    row = lambda d: pl.BlockSpec((tm, d), lambda i: (i, 0))
    weights = (wts["wbd"], wts["woa"], wts["wob"], wts["gpost"])
    return pl.pallas_call(
        _even_out_kernel,
        grid=(n // tm,),
        in_specs=[row(D_MODEL), row(CONV_WIDTH), row(MLA_HEADS * KV_LORA), row(CONV_WIDTH)]
        + [_full(w.shape) for w in weights],
        out_specs=row(D_MODEL),
        out_shape=jax.ShapeDtypeStruct((n, D_MODEL), F32),
        compiler_params=_params("parallel"),
        name="even_out",
    )(x, ya, olat, sgb, *weights)


def _mem_kv_kernel(x_ref, g_ref, wk_ref, wv_ref, k_ref, v_ref, k4_ref, v4_ref):
    h = _rms(x_ref[...], g_ref[0]).astype(BF16)
    k = _dot(h, wk_ref[0])
    v = _dot(h, wv_ref[0])
    k_ref[0] = k
    v_ref[0] = v
    for hd in range(X_HEADS):
        cs = slice(hd * X_HEAD_DIM, (hd + 1) * X_HEAD_DIM)
        k4_ref[0, :, hd, :] = k[:, cs]
        v4_ref[0, :, hd, :] = v[:, cs]


def mem_kv(mem, g, wk, wv, tm):
    n = mem.shape[0]
    layers = g.shape[0]
    per_layer = lambda shape: pl.BlockSpec((1,) + shape, lambda l, i: (l,) + (0,) * len(shape))
    flat = pl.BlockSpec((1, tm, D_MODEL), lambda l, i: (l, i, 0))
    split = pl.BlockSpec((1, tm, X_HEADS, X_HEAD_DIM), lambda l, i: (l, i, 0, 0))
    return pl.pallas_call(
        _mem_kv_kernel,
        grid=(layers, n // tm),
        in_specs=[pl.BlockSpec((tm, D_MODEL), lambda l, i: (i, 0)), per_layer((1, D_MODEL)),
                  per_layer((D_MODEL, D_MODEL)), per_layer((D_MODEL, D_MODEL))],
        out_specs=(flat, flat, split, split),
        out_shape=(jax.ShapeDtypeStruct((layers, n, D_MODEL), F32),) * 2
        + (jax.ShapeDtypeStruct((layers, n, X_HEADS, X_HEAD_DIM), F32),) * 2,
        compiler_params=_params("parallel", "parallel"),
        name="mem_kv",
    )(mem, g, wk, wv)


def _xattn_body(x_ref, gpre_ref, wqg_ref, wo_ref, gpost_ref, o_ref, att_scr, mem_kv_of, bb, tq):
    x = x_ref[...].reshape(bb * tq, D_MODEL)
    qg = _dot(_rms(x, gpre_ref[...]), wqg_ref[...])
    q = qg[:, :D_MODEL] * (X_HEAD_DIM ** -0.5)
    for b in range(bb):
        for hd in range(X_HEADS):
            cs = slice(hd * X_HEAD_DIM, (hd + 1) * X_HEAD_DIM)
            mk, mv = mem_kv_of(b, hd)
            s = _dot_nt(q[b * tq:(b + 1) * tq, cs], mk)
            p = jnp.exp(s - jnp.max(s, axis=1, keepdims=True))
            o = _dot(p, mv) / jnp.sum(p, axis=1, keepdims=True)
            att_scr[b * tq:(b + 1) * tq, cs] = o
    y = _dot(att_scr[...] * _silu(qg[:, D_MODEL:]), wo_ref[...])
    o_ref[...] = (x + _rms(y, gpost_ref[...])).reshape(bb, tq, D_MODEL)


def _xattn_kernel(x_ref, mk_ref, mv_ref, gpre_ref, wqg_ref, wo_ref, gpost_ref, o_ref, att_scr, *, bb, tq):
    def mem_kv_of(b, hd):
        cs = slice(hd * X_HEAD_DIM, (hd + 1) * X_HEAD_DIM)
        return mk_ref[b, :, cs], mv_ref[b, :, cs]

    _xattn_body(x_ref, gpre_ref, wqg_ref, wo_ref, gpost_ref, o_ref, att_scr, mem_kv_of, bb, tq)


def xattn(x, mk, mv, layer, wts, bb, tq):
    bsz, t, _ = x.shape
    seq = pl.BlockSpec((bb, tq, D_MODEL), lambda b, i: (b, i, 0))
    mem = pl.BlockSpec((None, bb, MEM_LEN, D_MODEL), lambda b, i: (layer, b, 0, 0))
    weights = (wts["gprex"], wts["wqg"], wts["wo"], wts["gpostx"])
    return pl.pallas_call(
        functools.partial(_xattn_kernel, bb=bb, tq=tq),
        grid=(bsz // bb, t // tq),
        in_specs=[seq, mem, mem] + [_full(w.shape) for w in weights],
        out_specs=seq,
        out_shape=jax.ShapeDtypeStruct(x.shape, F32),
        scratch_shapes=[pltpu.VMEM((bb * tq, D_MODEL), F32)],
        compiler_params=_params("parallel", "arbitrary"),
        name="xattn",
    )(x, mk, mv, *weights)


def _xattn_cached_kernel(x_ref, mk_hbm, mv_hbm, gpre_ref, wqg_ref, wo_ref, gpost_ref, o_ref, att_scr,
                         k_buf, v_buf, sem, *, bb, tq, layer):
    g = pl.program_id(0)
    slot = g & 1

    def head_copies(gi, sl):
        copies = []
        for b in range(bb):
            for hd in range(X_HEADS):
                src = (layer, gi * bb + b, slice(None), hd, slice(None))
                copies.append(pltpu.make_async_copy(mk_hbm.at[src], k_buf.at[sl, b, hd], sem.at[sl]))
                copies.append(pltpu.make_async_copy(mv_hbm.at[src], v_buf.at[sl, b, hd], sem.at[sl]))
        return copies

    @pl.when(g == 0)
    def _():
        for cp in head_copies(g, slot):
            cp.start()

    @pl.when(g + 1 < pl.num_programs(0))
    def _():
        for cp in head_copies(g + 1, 1 - slot):
            cp.start()

    for cp in head_copies(g, slot):
        cp.wait()

    _xattn_body(x_ref, gpre_ref, wqg_ref, wo_ref, gpost_ref, o_ref, att_scr,
                lambda b, hd: (k_buf[slot, b, hd], v_buf[slot, b, hd]), bb, tq)


def xattn_cached(x, cache_k, cache_v, layer, wts, bb):
    bsz, tq, _ = x.shape
    seq = pl.BlockSpec((bb, tq, D_MODEL), lambda g: (g, 0, 0))
    hbm = pl.BlockSpec(memory_space=pl.ANY)
    weights = (wts["gprex"], wts["wqg"], wts["wo"], wts["gpostx"])
    buf = pltpu.VMEM((2, bb, X_HEADS, MEM_LEN, X_HEAD_DIM), F32)
    return pl.pallas_call(
        functools.partial(_xattn_cached_kernel, bb=bb, tq=tq, layer=layer),
        grid=(bsz // bb,),
        in_specs=[seq, hbm, hbm] + [_full(w.shape) for w in weights],
        out_specs=seq,
        out_shape=jax.ShapeDtypeStruct(x.shape, F32),
        scratch_shapes=[pltpu.VMEM((bb * tq, D_MODEL), F32), buf, buf, pltpu.SemaphoreType.DMA((2,))],
        compiler_params=_params("arbitrary"),
        name="xattn_cached",
    )(x, cache_k, cache_v, *weights)


def _odd_in_kernel(x_ref, gpre_ref, wz_ref, wxbc_ref, wdt_ref, wdtt_ref, brow_ref, bcol_ref,
                   sz_ref, xbc_ref, dt_ref, dtt_ref):
    h = _rms(x_ref[...], gpre_ref[...]).astype(BF16)
    sz_ref[...] = _silu(_dot(h, wz_ref[...])).astype(BF16)
    xbc_ref[...] = _dot(h, wxbc_ref[...])
    dt_ref[...] = jax.nn.softplus(_dot(h, wdt_ref[...]) + brow_ref[...])
    dtt_ref[...] = jax.nn.softplus(_dot_nt(wdtt_ref[...], h) + bcol_ref[...])


def odd_in(x, wts, tm):
    n = x.shape[0]
    row = lambda d: pl.BlockSpec((tm, d), lambda i: (i, 0))
    weights = (wts["gpre"], wts["wz"], wts["wxbc"], wts["wdt"], wts["wdtt"], wts["brow"], wts["bcol"])
    return pl.pallas_call(
        _odd_in_kernel,
        grid=(n // tm,),
        in_specs=[row(D_MODEL)] + [_full(w.shape) for w in weights],
        out_specs=(row(SSM_INNER), row(XBC_DIM), row(SSM_HEADS),
                   pl.BlockSpec((SSM_HEADS, tm), lambda i: (0, i))),
        out_shape=(jax.ShapeDtypeStruct((n, SSM_INNER), BF16), jax.ShapeDtypeStruct((n, XBC_DIM), F32),
                   jax.ShapeDtypeStruct((n, SSM_HEADS), F32), jax.ShapeDtypeStruct((SSM_HEADS, n), F32)),
        compiler_params=_params("parallel"),
        name="odd_in",
    )(x, *weights)


def _ssd_kernel(*refs, q, t_in, tvalid, has_init):
    if has_init:
        (xs_ref, bm_ref, cm_ref, dt_ref, dtt_ref, arow_ref, acol_ref, drow_ref, e64_ref, tri_ref,
         trit_ref, init_ref, y_ref, st_ref, state_scr) = refs
    else:
        (xs_ref, bm_ref, cm_ref, dt_ref, dtt_ref, arow_ref, acol_ref, drow_ref, e64_ref, tri_ref,
         trit_ref, y_ref, st_ref, state_scr) = refs
    c = pl.program_id(1)

    @pl.when(c == 0)
    def _():
        if has_init:
            state_scr[...] = init_ref[0].T
        else:
            state_scr[...] = jnp.zeros(state_scr.shape, F32)

    def rows(ref):
        v = ref[0]
        if t_in < q:
            v = jnp.concatenate([v, jnp.zeros((q - t_in, v.shape[1]), F32)], axis=0)
        return v

    xs, bm, cm = rows(xs_ref), rows(bm_ref), rows(cm_ref)
    dt = dt_ref[0]
    dtt = dtt_ref[...]
    if tvalid < q:
        dt = jnp.where(lax.broadcasted_iota(jnp.int32, dt.shape, 0) < tvalid, dt, 0.0)
        dtt = jnp.where(lax.broadcasted_iota(jnp.int32, dtt.shape, 1) < tvalid, dtt, 0.0)
    cum = _exact_dot_left(tri_ref[...], dt * arow_ref[...])
    cumt = _exact_dot(dtt * acol_ref[...], trit_ref[...])
    cum_last = cum[q - 1:q, :]
    causal = (lax.broadcasted_iota(jnp.int32, (q, q), 1) <= lax.broadcasted_iota(jnp.int32, (q, q), 0))
    lane = lax.broadcasted_iota(jnp.int32, (1, LANES), 1)
    left = lane < SSM_HEADDIM

    state = state_scr[...]
    y_parts = []
    for g in range(SSM_GROUPS):
        gs = slice(g * SSM_STATE, (g + 1) * SSM_STATE)
        cm_g = cm[:, gs]
        cb = _dot_nt(cm_g, bm[:, gs])
        for pr in range(SSM_HPG // 2):
            pair = g * (SSM_HPG // 2) + pr
            ps = slice(pair * LANES, (pair + 1) * LANES)
            rhs = jnp.concatenate([xs[:, ps], state[:, ps]], axis=0)
            y_pair = None
            for side in range(2):
                hd = 2 * pair + side
                ccol = jnp.broadcast_to(cum[:, hd:hd + 1], (q, q))
                seg = ccol - cumt[hd:hd + 1, :]
                w = cb * jnp.exp(jnp.where(causal, seg, -jnp.inf)) * dtt[hd:hd + 1, :]
                lhs = jnp.concatenate([w, cm_g * jnp.exp(ccol)], axis=1)
                keep = left if side == 0 else jnp.logical_not(left)
                part = _dot(lhs, jnp.where(keep, rhs, 0.0))
                y_pair = part if y_pair is None else y_pair + part
            y_parts.append(y_pair)
    y = jnp.concatenate(y_parts, axis=1) + drow_ref[...] * xs
    y_ref[0] = (y[:t_in] if t_in < q else y).astype(y_ref.dtype)

    to_end = jnp.exp(cum_last - cum) * dt
    toxs = _dot(to_end, e64_ref[...]) * xs
    dec = _exact_dot(jnp.broadcast_to(jnp.exp(cum_last), (SUBLANES, SSM_HEADS)), e64_ref[...])[0:1, :]
    new_parts = []
    for g in range(SSM_GROUPS):
        gs = slice(g * SSM_STATE, (g + 1) * SSM_STATE)
        hs = slice(g * GROUP_WIDTH, (g + 1) * GROUP_WIDTH)
        new_parts.append(_dot(bm[:, gs].T, toxs[:, hs]))
    new_state = state * dec + jnp.concatenate(new_parts, axis=1)
    state_scr[...] = new_state

    @pl.when(c == pl.num_programs(1) - 1)
    def _():
        st_ref[0] = new_state.T


def ssd(xbc, dt, dtt, wts, init, q, t_in, tvalid, y_dtype):
    bsz, t, _ = xbc.shape
    nc = max(t // q, 1)
    nxb = SSM_INNER // (SSM_GROUPS * SSM_STATE)
    in_specs = [pl.BlockSpec((1, t_in, SSM_INNER), lambda b, c: (b, c, 0)),
                pl.BlockSpec((1, t_in, SSM_GROUPS * SSM_STATE), lambda b, c: (b, c, nxb)),
                pl.BlockSpec((1, t_in, SSM_GROUPS * SSM_STATE), lambda b, c: (b, c, nxb + 1)),
                pl.BlockSpec((1, q, SSM_HEADS), lambda b, c: (b, c, 0)),
                pl.BlockSpec((SSM_HEADS, q), lambda b, c: (0, b * nc + c))]
    consts = (wts["arow"], wts["acol"], wts["drow"], wts["e64"], wts["tri"], wts["trit"])
    in_specs += [_full(w.shape) for w in consts]
    args = [xbc, xbc, xbc, dt, dtt, *consts]
    if init is not None:
        in_specs.append(pl.BlockSpec((1, SSM_INNER, SSM_STATE), lambda b, c: (b, 0, 0)))
        args.append(init)
    return pl.pallas_call(
        functools.partial(_ssd_kernel, q=q, t_in=t_in, tvalid=tvalid, has_init=init is not None),
        grid=(bsz, nc),
        in_specs=in_specs,
        out_specs=(pl.BlockSpec((1, t_in, SSM_INNER), lambda b, c: (b, c, 0)),
                   pl.BlockSpec((1, SSM_INNER, SSM_STATE), lambda b, c: (b, 0, 0))),
        out_shape=(jax.ShapeDtypeStruct((bsz, t, SSM_INNER), y_dtype),
                   jax.ShapeDtypeStruct((bsz, SSM_INNER, SSM_STATE), F32)),
        scratch_shapes=[pltpu.VMEM((SSM_STATE, SSM_INNER), F32)],
        compiler_params=_params("parallel", "arbitrary"),
        name="ssd",
    )(*args)


def _odd_out_kernel(x_ref, y_ref, sz_ref, gn_ref, wout_ref, gpost_ref, o_ref):
    v = y_ref[...].astype(F32) * sz_ref[...].astype(F32)
    gn = gn_ref[...]
    parts = []
    for g in range(SSM_GROUPS):
        gs = slice(g * GROUP_WIDTH, (g + 1) * GROUP_WIDTH)
        parts.append(_rms(v[:, gs], gn[:, gs]).astype(BF16))
    y = _dot(jnp.concatenate(parts, axis=1), wout_ref[...])
    o_ref[...] = x_ref[...] + _rms(y, gpost_ref[...])


def odd_out(x, y, sz, wts, tm):
    n = x.shape[0]
    row = lambda d: pl.BlockSpec((tm, d), lambda i: (i, 0))
    weights = (wts["gnorm"], wts["wout"], wts["gpost"])
    return pl.pallas_call(
        _odd_out_kernel,
        grid=(n // tm,),
        in_specs=[row(D_MODEL), row(SSM_INNER), row(SSM_INNER)] + [_full(w.shape) for w in weights],
        out_specs=row(D_MODEL),
        out_shape=jax.ShapeDtypeStruct((n, D_MODEL), F32),
        compiler_params=_params("parallel"),
        name="odd_out",
    )(x, y, sz, *weights)


def _even_weights(i, l, p):
    w_in = p["e_w_in"][l]
    c = CONV_WIDTH
    o_q, o_kv, o_bg = 3 * c, 3 * c + Q_LORA, 3 * c + Q_LORA + KV_LORA + ROPE_DIM
    half = ROPE_DIM // 2
    w_kr = w_in[:, o_kv + KV_LORA:o_bg]
    w_uq = p["e_w_uq"][l]
    uq_r = w_uq[:, :, NOPE_DIM:]
    w_uv = jnp.transpose(p["e_w_uv"][l], (1, 0, 2))
    eye = jnp.eye(MLA_HEADS, dtype=F32)
    wbd = (eye[:, None, :, None] * w_uv[:, :, None, :]).reshape(MLA_HEADS * KV_LORA, MLA_HEADS * V_DIM)
    w_out = p["e_w_out"][l]
    return {
        "gpre": p["norm_pre_mix"][i][None], "gpost": p["norm_post_mix"][i][None],
        "wa": w_in[:, :o_q].astype(BF16), "wq": w_in[:, o_q:o_kv].astype(BF16),
        "wkv": jnp.concatenate([w_in[:, o_kv:o_bg], -w_kr[:, half:], w_kr[:, :half]], axis=1).astype(BF16),
        "wbg": w_in[:, o_bg:].astype(BF16),
        "gq": p["e_q_norm"][l][None], "gkv": p["e_kv_norm"][l][None],
        "wuqn": w_uq[:, :, :NOPE_DIM].reshape(Q_LORA, MLA_HEADS * NOPE_DIM).astype(BF16),
        "wuqr": uq_r.reshape(Q_LORA, MLA_HEADS * ROPE_DIM).astype(BF16),
        "wuqrot": jnp.concatenate([-uq_r[:, :, half:], uq_r[:, :, :half]], axis=2)
        .reshape(Q_LORA, MLA_HEADS * ROPE_DIM).astype(BF16),
        "wukt": jnp.transpose(p["e_w_uk"][l], (1, 2, 0)).astype(BF16),
        "conv_w": p["e_conv_w"][l], "conv_b": p["e_conv_b"][l][None],
        "ln_w": p["e_ln_w"][l][None], "ln_b": p["e_ln_b"][l][None],
        "wbd": wbd.astype(BF16), "woa": w_out[:c].astype(BF16), "wob": w_out[c:].astype(BF16),
    }


def _odd_weights(i, l, p):
    w_in = p["o_w_in"][l]
    w_dt = w_in[:, SSM_INNER + XBC_DIM:]
    a = -jnp.exp(p["o_a_log"][l])
    q = SSM_CHUNK
    tri = jnp.tril(jnp.ones((q, q), F32))
    return {
        "gpre": p["norm_pre_mix"][i][None], "gpost": p["norm_post_mix"][i][None],
        "wz": w_in[:, :SSM_INNER].astype(BF16),
        "wxbc": w_in[:, SSM_INNER:SSM_INNER + XBC_DIM].astype(BF16),
        "wdt": w_dt.astype(BF16), "wdtt": w_dt.T.astype(BF16),
        "brow": p["o_dt_bias"][l][None], "bcol": p["o_dt_bias"][l][:, None],
        "conv_w": p["o_conv_w"][l], "conv_b": p["o_conv_b"][l][None],
        "arow": a[None], "acol": a[:, None],
        "drow": jnp.repeat(p["o_d"][l], SSM_HEADDIM)[None],
        "e64": jnp.repeat(jnp.eye(SSM_HEADS, dtype=F32), SSM_HEADDIM, axis=1).astype(BF16),
        "tri": tri.astype(BF16), "trit": tri.T.astype(BF16),
        "gnorm": p["o_norm"][l][None], "wout": p["o_w_out"][l].astype(BF16),
    }


def _xattn_weights(i, p):
    return {"gprex": p["norm_pre_x"][i][None], "gpostx": p["norm_post_x"][i][None],
            "wqg": p["x_w_qg"][i].astype(BF16), "wo": p["x_w_o"][i].astype(BF16)}


def kernel(x_prompt, x_sample, mem_prompt, cache_ckv, cache_krope, page_table, state_conv_a,
           state_conv_c, state_ssm, cache_mem_k, cache_mem_v, norm_pre_mix, norm_post_mix,
           norm_pre_x, norm_post_x, norm_mem, x_w_qg, x_w_k, x_w_v, x_w_o, e_w_in, e_conv_w,
           e_conv_b, e_ln_w, e_ln_b, e_q_norm, e_kv_norm, e_w_uq, e_w_uk, e_w_uv, e_w_out,
           o_w_in, o_conv_w, o_conv_b, o_dt_bias, o_a_log, o_d, o_norm, o_w_out):
    p = dict(norm_pre_mix=norm_pre_mix, norm_post_mix=norm_post_mix, norm_pre_x=norm_pre_x,
             norm_post_x=norm_post_x, e_w_in=e_w_in, e_conv_w=e_conv_w, e_conv_b=e_conv_b,
             e_ln_w=e_ln_w, e_ln_b=e_ln_b, e_q_norm=e_q_norm, e_kv_norm=e_kv_norm, e_w_uq=e_w_uq,
             e_w_uk=e_w_uk, e_w_uv=e_w_uv, e_w_out=e_w_out, o_w_in=o_w_in, o_conv_w=o_conv_w,
             o_conv_b=o_conv_b, o_dt_bias=o_dt_bias, o_a_log=o_a_log, o_d=o_d, o_norm=o_norm,
             o_w_out=o_w_out, x_w_qg=x_w_qg, x_w_o=x_w_o)
    bp, sp, _ = x_prompt.shape
    bs, ss, _ = x_sample.shape
    depth = norm_pre_mix.shape[0]
    n_pages, page = page_table.shape[1], cache_ckv.shape[1]
    past_len = n_pages * page
    tpad = SUBLANES
    assert ss <= tpad and sp % SSM_CHUNK == 0
    np_, ns = bp * sp, bs * tpad
    tm_p, tm_s = 512, ns

    half = ROPE_DIM // 2
    inv = ROPE_THETA ** (-jnp.arange(half, dtype=F32) / half)
    inv_lanes = jnp.tile(inv, LANES // half)[None]
    cos_p, sin_p = rope_table(inv_lanes, sp, 0, sp)
    cos_s, sin_s = rope_table(inv_lanes, ns, past_len, tpad)

    xp = x_prompt.reshape(np_, D_MODEL)
    xs = jnp.pad(x_sample, ((0, 0), (0, tpad - ss), (0, 0))).reshape(ns, D_MODEL)
    mem_flat = mem_prompt.reshape(bp * MEM_LEN, D_MODEL)
    mk, mv, mk4, mv4 = mem_kv(mem_flat, norm_mem[:, None], x_w_k.astype(BF16), x_w_v.astype(BF16), 512)
    mem_k_p = mk.reshape(depth, bp, MEM_LEN, D_MODEL)
    mem_v_p = mv.reshape(depth, bp, MEM_LEN, D_MODEL)

    outs = {k: [] for k in ("ckv_p", "ckv_s", "kr_p", "kr_s", "ca_p", "ca_s", "cc_p", "cc_s",
                            "ssm_p", "ssm_s", "mk", "mv")}
    for i in range(depth):
        l = i // 2
        if i % 2 == 0:
            w = _even_weights(i, l, p)
            glu, sga, sgb, qcat, kcat, ckv, kr = even_in(xp, w, cos_p, sin_p, tm_p)
            glu3 = glu.reshape(bp, sp, CONV_WIDTH)
            ya = causal_conv(glu3, jnp.zeros((bp, CONV_A_HALO, CONV_WIDTH), F32), w["conv_w"], w["conv_b"],
                             256, CONV_A_HALO, CONV_WIDTH,
                             ln=(w["ln_w"], w["ln_b"], sga.reshape(bp, sp, CONV_WIDTH)), out_dtype=BF16)
            olat = mla_prompt(qcat, kcat.reshape(bp, sp, QK_CAT), bp, sp, 256)
            yp = even_out(xp, ya.reshape(np_, CONV_WIDTH), olat, sgb, w, tm_p)
            outs["ca_p"].append(glu3[:, sp - (CONV_K - 1):])
            outs["ckv_p"].append(ckv.reshape(bp, sp, KV_LORA))
            outs["kr_p"].append(kr.reshape(bp, sp, ROPE_DIM))
            glu, sga, sgb, qcat, kcat, ckv, kr = even_in(xs, w, cos_s, sin_s, tm_s)
            glu3 = glu.reshape(bs, tpad, CONV_WIDTH)
            hist = state_conv_a[l]
            hist_pad = jnp.pad(hist, ((0, 0), (CONV_A_HALO - (CONV_K - 1), 0), (0, 0)))
            ya = causal_conv(glu3, hist_pad, w["conv_w"], w["conv_b"], tpad, CONV_A_HALO, CONV_WIDTH,
                             ln=(w["ln_w"], w["ln_b"], sga.reshape(bs, tpad, CONV_WIDTH)))
            knew = jnp.pad(kcat.reshape(bs, tpad, QK_CAT), ((0, 0), (0, LANES - tpad), (0, 0)))
            olat = mla_sample(qcat, knew, cache_ckv[:, :, l, :],
                              jnp.transpose(cache_krope[:, :, l, :], (0, 2, 1)), page_table, tpad, ss,
                              min(32, n_pages))
            ys = even_out(xs, ya.reshape(ns, CONV_WIDTH), olat, sgb, w, tm_s)
            outs["ca_s"].append(jnp.concatenate([hist[:, ss:], glu3[:, :ss]], axis=1))
            outs["ckv_s"].append(ckv.reshape(bs, tpad, KV_LORA)[:, :ss])
            outs["kr_s"].append(kr.reshape(bs, tpad, ROPE_DIM)[:, :ss])
        else:
            w = _odd_weights(i, l, p)
            q = SSM_CHUNK
            sz, xbc, dt, dtt = odd_in(xp, w, 256)
            xbc3 = xbc.reshape(bp, sp, XBC_DIM)
            xbc_c = causal_conv(xbc3, jnp.zeros((bp, CONV_C_HALO, XBC_DIM), F32), w["conv_w"], w["conv_b"],
                                256, CONV_C_HALO, 512)
            y, st = ssd(xbc_c, dt.reshape(bp, sp, SSM_HEADS), dtt, w, None, q, q, q, BF16)
            yp = odd_out(xp, y.reshape(np_, SSM_INNER), sz, w, tm_p)
            outs["cc_p"].append(xbc3[:, sp - (SSM_CONV - 1):])
            outs["ssm_p"].append(st.reshape(bp, SSM_HEADS, SSM_HEADDIM, SSM_STATE))
            sz, xbc, dt, dtt = odd_in(xs, w, tm_s)
            xbc3 = xbc.reshape(bs, tpad, XBC_DIM)
            hist = state_conv_c[l]
            hist_pad = jnp.pad(hist, ((0, 0), (CONV_C_HALO - (SSM_CONV - 1), 0), (0, 0)))
            xbc_c = causal_conv(xbc3, hist_pad, w["conv_w"], w["conv_b"], tpad, CONV_C_HALO, 512)
            dt_pad = jnp.pad(dt.reshape(bs, tpad, SSM_HEADS), ((0, 0), (0, q - tpad), (0, 0)))
            dtt_pad = jnp.pad(dtt.reshape(SSM_HEADS, bs, tpad), ((0, 0), (0, 0), (0, q - tpad)))
            y, st = ssd(xbc_c, dt_pad, dtt_pad.reshape(SSM_HEADS, bs * q), w,
                        state_ssm[l].reshape(bs, SSM_INNER, SSM_STATE), q, tpad, ss, F32)
            ys = odd_out(xs, y.reshape(ns, SSM_INNER), sz, w, tm_s)
            outs["cc_s"].append(jnp.concatenate([hist[:, ss:], xbc3[:, :ss]], axis=1)[:, -(SSM_CONV - 1):])
            outs["ssm_s"].append(st.reshape(bs, SSM_HEADS, SSM_HEADDIM, SSM_STATE))
        xp, xs = yp, ys
        wx = _xattn_weights(i, p)
        xp = xattn(xp.reshape(bp, sp, D_MODEL), mem_k_p, mem_v_p, i, wx, 1, 512).reshape(np_, D_MODEL)
        xs = xattn_cached(xs.reshape(bs, tpad, D_MODEL), cache_mem_k, cache_mem_v, i, wx, 4).reshape(ns, D_MODEL)

    return (xp.reshape(bp, sp, D_MODEL), xs.reshape(bs, tpad, D_MODEL)[:, :ss],
            jnp.stack(outs["ckv_p"], axis=2), jnp.stack(outs["ckv_s"], axis=2),
            jnp.stack(outs["kr_p"], axis=2), jnp.stack(outs["kr_s"], axis=2),
            jnp.stack(outs["ca_p"]), jnp.stack(outs["ca_s"]),
            jnp.stack(outs["cc_p"]), jnp.stack(outs["cc_s"]),
            jnp.stack(outs["ssm_p"]), jnp.stack(outs["ssm_s"]),
            mk4.reshape(depth, bp, MEM_LEN, X_HEADS, X_HEAD_DIM),
            mv4.reshape(depth, bp, MEM_LEN, X_HEADS, X_HEAD_DIM))
```

```python
import functools

import jax
import jax.numpy as jnp
from jax import lax
from jax.experimental import pallas as pl
from jax.experimental.pallas import tpu as pltpu

F32 = jnp.float32
BF16 = jnp.bfloat16

D_MODEL = 1024
CONV_WIDTH = 512
CONV_K = 31
MLA_HEADS = 8
Q_LORA = 256
KV_LORA = 128
NOPE_DIM = 64
ROPE_DIM = 32
V_DIM = 64
QK_CAT = KV_LORA + ROPE_DIM
LOG2E = 1.4426950408889634
MLA_SCALE = (NOPE_DIM + ROPE_DIM) ** -0.5 * LOG2E
ROPE_THETA = 10000.0
SSM_INNER = 2048
SSM_HEADDIM = 64
SSM_HEADS = 32
SSM_GROUPS = 4
SSM_HPG = 8
SSM_STATE = 128
SSM_CONV = 4
SSM_CHUNK = 128
XBC_DIM = SSM_INNER + 2 * SSM_GROUPS * SSM_STATE
GROUP_WIDTH = SSM_INNER // SSM_GROUPS
MEM_LEN = 256
X_HEADS = 4
X_HEAD_DIM = 256
EPS = 1e-6

LANES = 128
SUBLANES = 8
CONV_A_HALO = 32
CONV_C_HALO = 8
VMEM_LIMIT = 56 * 1024 * 1024


def _params(*sem):
    return pltpu.CompilerParams(dimension_semantics=sem, vmem_limit_bytes=VMEM_LIMIT)


def _rms(x, w):
    return x * lax.rsqrt(jnp.mean(x * x, axis=-1, keepdims=True) + EPS) * w


def _silu(x):
    return x * jax.nn.sigmoid(x)


def _dot(a, b):
    return jnp.dot(a.astype(BF16), b.astype(BF16), preferred_element_type=F32)


def _dot_nt(a, b):
    return lax.dot_general(a.astype(BF16), b.astype(BF16), (((1,), (1,)), ((), ())),
                           preferred_element_type=F32)


def _split3(x):
    hi = x.astype(BF16)
    r = x - hi.astype(F32)
    mid = r.astype(BF16)
    lo = (r - mid.astype(F32)).astype(BF16)
    return hi, mid, lo


def _exact_dot(x, e):
    hi, mid, lo = _split3(x)
    return (jnp.dot(hi, e, preferred_element_type=F32) + jnp.dot(mid, e, preferred_element_type=F32)
            + jnp.dot(lo, e, preferred_element_type=F32))


def _exact_dot_left(e, x):
    hi, mid, lo = _split3(x)
    return (jnp.dot(e, hi, preferred_element_type=F32) + jnp.dot(e, mid, preferred_element_type=F32)
            + jnp.dot(e, lo, preferred_element_type=F32))


def _full(shape):
    nd = len(shape)
    return pl.BlockSpec(shape, lambda *_: (0,) * nd)


def _rope_table_kernel(inv_ref, cos_ref, sin_ref, *, base, period):
    row = lax.broadcasted_iota(jnp.int32, cos_ref.shape, 0)
    pos = (base + (row & (period - 1))).astype(F32)
    ang = pos * inv_ref[...]
    cos_ref[...] = jnp.cos(ang)
    sin_ref[...] = jnp.sin(ang)


def rope_table(inv_lanes, rows, base, period):
    assert period & (period - 1) == 0
    return pl.pallas_call(
        functools.partial(_rope_table_kernel, base=base, period=period),
        out_shape=(jax.ShapeDtypeStruct((rows, LANES), F32),) * 2,
        name="rope_table",
    )(inv_lanes)


def _even_in_kernel(x_ref, gpre_ref, wa_ref, wq_ref, wkv_ref, wbg_ref, gq_ref, gkv_ref,
                    wuqn_ref, wuqr_ref, wuqrot_ref, wukt_ref, cos_ref, sin_ref,
                    glu_ref, sga_ref, sgb_ref, qcat_ref, kcat_ref, ckv_ref, kr_ref):
    h = _rms(x_ref[...], gpre_ref[...]).astype(BF16)
    ua = _dot(h, wa_ref[...])
    glu_ref[...] = ua[:, :CONV_WIDTH] * jax.nn.sigmoid(ua[:, CONV_WIDTH:2 * CONV_WIDTH])
    sga_ref[...] = _silu(ua[:, 2 * CONV_WIDTH:])
    sgb_ref[...] = _silu(_dot(h, wbg_ref[...])).astype(BF16)
    cos = cos_ref[...]
    sin = sin_ref[...]
    kv = _dot(h, wkv_ref[...])
    ckv = _rms(kv[:, :KV_LORA], gkv_ref[...])
    kr = (kv[:, KV_LORA:KV_LORA + ROPE_DIM] * cos[:, :ROPE_DIM]
          + kv[:, KV_LORA + ROPE_DIM:] * sin[:, :ROPE_DIM])
    ckv_ref[...] = ckv
    kr_ref[...] = kr
    kcat_ref[:, :KV_LORA] = ckv.astype(BF16)
    kcat_ref[:, KV_LORA:] = kr.astype(BF16)
    qn = _rms(_dot(h, wq_ref[...]), gq_ref[...]).astype(BF16)
    qnope = _dot(qn, wuqn_ref[...])
    cos2 = jnp.concatenate([cos, cos], axis=1)
    sin2 = jnp.concatenate([sin, sin], axis=1)
    qr = (_dot(qn, wuqr_ref[...]) * cos2 + _dot(qn, wuqrot_ref[...]) * sin2) * MLA_SCALE
    for hd in range(MLA_HEADS):
        ql = _dot(qnope[:, hd * NOPE_DIM:(hd + 1) * NOPE_DIM], wukt_ref[hd]) * MLA_SCALE
        qcat_ref[hd, :, :KV_LORA] = ql.astype(BF16)
        qcat_ref[hd, :, KV_LORA:] = qr[:, hd * ROPE_DIM:(hd + 1) * ROPE_DIM].astype(BF16)


def even_in(x, wts, cos, sin, tm):
    n = x.shape[0]
    nblk = cos.shape[0] // tm
    row = lambda d: pl.BlockSpec((tm, d), lambda i: (i, 0))
    tab = pl.BlockSpec((tm, LANES), lambda i: (i % nblk, 0))
    weights = (wts["gpre"], wts["wa"], wts["wq"], wts["wkv"], wts["wbg"], wts["gq"], wts["gkv"],
               wts["wuqn"], wts["wuqr"], wts["wuqrot"], wts["wukt"])
    return pl.pallas_call(
        _even_in_kernel,
        grid=(n // tm,),
        in_specs=[row(D_MODEL)] + [_full(w.shape) for w in weights] + [tab, tab],
        out_specs=(row(CONV_WIDTH), row(CONV_WIDTH), row(CONV_WIDTH),
                   pl.BlockSpec((MLA_HEADS, tm, QK_CAT), lambda i: (0, i, 0)),
                   row(QK_CAT), row(KV_LORA), row(ROPE_DIM)),
        out_shape=(jax.ShapeDtypeStruct((n, CONV_WIDTH), F32),) * 2 + (
            jax.ShapeDtypeStruct((n, CONV_WIDTH), BF16),
            jax.ShapeDtypeStruct((MLA_HEADS, n, QK_CAT), BF16),
            jax.ShapeDtypeStruct((n, QK_CAT), BF16),
            jax.ShapeDtypeStruct((n, KV_LORA), F32),
            jax.ShapeDtypeStruct((n, ROPE_DIM), F32)),
        compiler_params=_params("parallel"),
        name="even_in",
    )(x, *weights, cos, sin)


def _conv_kernel(hist_ref, prev_ref, cur_ref, w_ref, b_ref, *rest, taps, halo, row_chunk, lane_chunk,
                 layernorm):
    if layernorm:
        lnw_ref, lnb_ref, gate_ref, out_ref, ext_scr, *shift_scr = rest
    else:
        out_ref, ext_scr, *shift_scr = rest
    nb, tt, width = cur_ref.shape
    off = halo - (taps - 1)

    def ext_rows(d, r0, cs):
        a, r = divmod(d, SUBLANES)
        if not shift_scr or r == 0:
            return ext_scr[d + r0:d + r0 + row_chunk, cs]
        return shift_scr[0][r - 1, SUBLANES * a + r0:SUBLANES * a + r0 + row_chunk, cs]

    for sq in range(nb):
        ext_scr[0:halo] = jnp.where(pl.program_id(1) == 0, hist_ref[sq], prev_ref[sq])
        ext_scr[halo:halo + tt] = cur_ref[sq]
        if shift_scr:
            sh_scr, = shift_scr
            span = sh_scr.shape[1]
            ext = ext_scr[...]
            for r in range(1, SUBLANES):
                sh_scr[r - 1] = pltpu.roll(ext, halo + tt - r, axis=0)[:span]

        for r0 in range(0, tt, row_chunk):
            for c0 in range(0, width, lane_chunk):
                cs = slice(c0, c0 + lane_chunk)
                if shift_scr:
                    taps_in = [ext_rows(off + k, r0, cs) for k in range(taps)]
                else:
                    win = ext_scr[r0 + halo - SUBLANES:r0 + halo + row_chunk, cs]
                    taps_in = [(pltpu.roll(win, taps - 1 - k, axis=0) if k < taps - 1 else win)[SUBLANES:]
                               for k in range(taps)]
                acc = w_ref[0:1, cs] * taps_in[0]
                for k in range(1, taps):
                    acc = acc + w_ref[k:k + 1, cs] * taps_in[k]
                acc = acc + b_ref[:, cs]
                if layernorm:
                    xc = acc - jnp.mean(acc, axis=-1, keepdims=True)
                    var = jnp.mean(xc * xc, axis=-1, keepdims=True)
                    ln = xc * lax.rsqrt(var + EPS) * lnw_ref[...] + lnb_ref[...]
                    out_ref[sq, r0:r0 + row_chunk, :] = (_silu(ln) * gate_ref[sq, r0:r0 + row_chunk, :]
                                                         ).astype(out_ref.dtype)
                else:
                    out_ref[sq, r0:r0 + row_chunk, cs] = _silu(acc)


def causal_conv(x, hist, w, b, tt, halo, lane_chunk, ln=None, out_dtype=F32, nb=1):
    bsz, t, width = x.shape
    taps = w.shape[0]
    seq = pl.BlockSpec((nb, tt, width), lambda bi, ti: (bi, ti, 0))
    first = pl.BlockSpec((nb, halo, width), lambda bi, ti: (bi, 0, 0))
    if t == tt:
        prev, prev_spec = hist, first
    else:
        ratio = tt // halo
        prev = x
        prev_spec = pl.BlockSpec((nb, halo, width), lambda bi, ti: (bi, jnp.maximum(ti * ratio - 1, 0), 0))
    in_specs = [first, prev_spec, seq, _full(w.shape), _full(b.shape)]
    args = [hist, prev, x, w, b]
    if ln is not None:
        in_specs += [_full(ln[0].shape), _full(ln[1].shape), seq]
        args += list(ln)
    return pl.pallas_call(
        functools.partial(_conv_kernel, taps=taps, halo=halo, row_chunk=min(32, tt),
                          lane_chunk=lane_chunk, layernorm=ln is not None),
        grid=(bsz // nb, t // tt),
        in_specs=in_specs,
        out_specs=seq,
        out_shape=jax.ShapeDtypeStruct((bsz, t, width), out_dtype),
        scratch_shapes=[pltpu.VMEM((halo + tt, width), F32)]
        + ([pltpu.VMEM((SUBLANES - 1, halo + tt - SUBLANES, width), F32)] if taps > SUBLANES else []),
        compiler_params=_params("parallel", "arbitrary"),
        name="causal_conv_ln" if ln is not None else "causal_conv",
    )(*args)


def _mla_prompt_kernel(q_ref, k_ref, o_ref, s_scr, acc_scr, mx_scr, sum_scr, *, tq):
    i = pl.program_id(1)
    rows = MLA_HEADS * tq
    q = q_ref[...].reshape(rows, QK_CAT)

    def keys(j):
        return k_ref[0, pl.ds(pl.multiple_of(j * tq, tq), tq), :]

    def fold(x):
        parts = [x[:, c:c + LANES] for c in range(0, tq, LANES)]
        return parts

    mx_scr[...] = jnp.full(mx_scr.shape, -jnp.inf, F32)

    def pass1(j, carry):
        s = _dot_nt(q, keys(j))
        s_scr[j] = s
        mx_scr[...] = functools.reduce(jnp.maximum, fold(s), mx_scr[...])
        return carry

    lax.fori_loop(0, i, pass1, 0)
    s = _dot_nt(q, keys(i))
    qpos = lax.broadcasted_iota(jnp.int32, s.shape, 0) & (tq - 1)
    kpos = lax.broadcasted_iota(jnp.int32, s.shape, 1)
    s = jnp.where(kpos <= qpos, s, -jnp.inf)
    s_scr[i] = s
    m = jnp.max(functools.reduce(jnp.maximum, fold(s), mx_scr[...]), axis=1, keepdims=True)

    acc_scr[...] = jnp.zeros(acc_scr.shape, F32)
    sum_scr[...] = jnp.zeros(sum_scr.shape, F32)

    def pass2(j, carry):
        p = jnp.exp2(s_scr[j] - m)
        acc_scr[...] += _dot(p, keys(j)[:, :KV_LORA])
        sum_scr[...] = functools.reduce(jnp.add, fold(p), sum_scr[...])
        return carry

    lax.fori_loop(0, i + 1, pass2, 0)
    o = acc_scr[...] / jnp.sum(sum_scr[...], axis=1, keepdims=True)
    for hd in range(MLA_HEADS):
        o_ref[:, hd * KV_LORA:(hd + 1) * KV_LORA] = o[hd * tq:(hd + 1) * tq, :].astype(BF16)


def mla_prompt(qcat, kcat, bsz, t, tq):
    nq = t // tq
    rows = MLA_HEADS * tq
    return pl.pallas_call(
        functools.partial(_mla_prompt_kernel, tq=tq),
        grid=(bsz, nq),
        in_specs=[pl.BlockSpec((MLA_HEADS, tq, QK_CAT), lambda b, i: (0, b * nq + i, 0)),
                  pl.BlockSpec((1, t, QK_CAT), lambda b, i: (b, 0, 0))],
        out_specs=pl.BlockSpec((tq, MLA_HEADS * KV_LORA), lambda b, i: (b * nq + i, 0)),
        out_shape=jax.ShapeDtypeStruct((bsz * t, MLA_HEADS * KV_LORA), BF16),
        scratch_shapes=[pltpu.VMEM((nq, rows, tq), F32), pltpu.VMEM((rows, KV_LORA), F32),
                        pltpu.VMEM((rows, LANES), F32), pltpu.VMEM((rows, LANES), F32)],
        compiler_params=_params("parallel", "arbitrary"),
        name="mla_prompt",
    )(qcat, kcat)


def _mla_sample_kernel(pt_ref, q_ref, knew_ref, ckv_hbm, krt_hbm, o_ref, ckv_buf, krt_buf, sem,
                       kc_scr, krt_scr, m_scr, l_scr, acc_scr, *, pages, page, tvalid):
    b = pl.program_id(0)
    j = pl.program_id(1)
    steps = pl.num_programs(1)
    g = b * steps + j
    slot = g & 1

    def page_copies(bi, ji, sl):
        copies = []
        for i in range(pages):
            pg = pt_ref[bi, ji * pages + i]
            copies.append(pltpu.make_async_copy(ckv_hbm.at[pg], ckv_buf.at[sl, pl.ds(i * page, page)],
                                                sem.at[sl]))
            copies.append(pltpu.make_async_copy(krt_hbm.at[pg], krt_buf.at[sl, i], sem.at[sl]))
        return copies

    def start_all(copies):
        for n, cp in enumerate(copies):
            cp.start(priority=(n // 2) % 2)

    @pl.when(g == 0)
    def _():
        start_all(page_copies(b, j, slot))

    @pl.when(g + 1 < pl.num_programs(0) * steps)
    def _():
        wrap = j + 1 == steps
        start_all(page_copies(jnp.where(wrap, b + 1, b), jnp.where(wrap, 0, j + 1), 1 - slot))

    for cp in page_copies(b, j, slot):
        cp.wait()

    @pl.when(j == 0)
    def _():
        m_scr[...] = jnp.full(m_scr.shape, -jnp.inf, F32)
        l_scr[...] = jnp.zeros(l_scr.shape, F32)
        acc_scr[...] = jnp.zeros(acc_scr.shape, F32)

    q = q_ref[0]

    def update(s, v):
        m_prev = m_scr[...]
        m_new = jnp.maximum(m_prev, jnp.max(s, axis=1, keepdims=True))
        alpha = jnp.exp2(m_prev - m_new)
        p = jnp.exp2(s - m_new[:, 0:1])
        l_scr[...] = alpha * l_scr[...] + jnp.sum(p, axis=1, keepdims=True)
        m_scr[...] = m_new
        acc_scr[...] = alpha * acc_scr[...] + _dot(p, v)

    kc_scr[...] = ckv_buf[slot].astype(BF16)
    for i in range(pages):
        krt_scr[:, i * page:(i + 1) * page] = krt_buf[slot, i].astype(BF16)
    kc = kc_scr[...]
    update(_dot_nt(q[:, :KV_LORA], kc) + _dot(q[:, KV_LORA:], krt_scr[...]), kc)

    @pl.when(j == pl.num_programs(1) - 1)
    def _():
        knew = knew_ref[0]
        s = _dot_nt(q, knew)
        qpos = lax.rem(lax.broadcasted_iota(jnp.int32, s.shape, 0), tvalid)
        kpos = lax.broadcasted_iota(jnp.int32, s.shape, 1)
        s = jnp.where((kpos <= qpos) & (kpos < tvalid), s, -jnp.inf)
        update(s, knew[:, :KV_LORA])
        o_ref[0] = acc_scr[...] / l_scr[...]


def mla_sample(qcat, knew, ckv_pages, krt_pages, page_table, tpad, tvalid, pages):
    bsz, n_pages = page_table.shape
    page = ckv_pages.shape[1]
    steps = n_pages // pages
    rows = MLA_HEADS * tvalid
    q = qcat.astype(F32).reshape(MLA_HEADS, bsz, tpad, QK_CAT)[:, :, :tvalid]
    q = jnp.transpose(q, (1, 0, 2, 3)).reshape(bsz, rows, QK_CAT)

    grid_spec = pltpu.PrefetchScalarGridSpec(
        num_scalar_prefetch=1,
        grid=(bsz, steps),
        in_specs=[pl.BlockSpec((1, rows, QK_CAT), lambda b, j, pt: (b, 0, 0)),
                  pl.BlockSpec((1, knew.shape[1], QK_CAT), lambda b, j, pt: (b, 0, 0)),
                  pl.BlockSpec(memory_space=pl.ANY), pl.BlockSpec(memory_space=pl.ANY)],
        out_specs=pl.BlockSpec((1, rows, KV_LORA), lambda b, j, pt: (b, 0, 0)),
        scratch_shapes=[pltpu.VMEM((2, pages * page, KV_LORA), F32),
                        pltpu.VMEM((2, pages, ROPE_DIM, page), F32),
                        pltpu.SemaphoreType.DMA((2,)),
                        pltpu.VMEM((pages * page, KV_LORA), BF16),
                        pltpu.VMEM((ROPE_DIM, pages * page), BF16),
                        pltpu.VMEM((rows, LANES), F32), pltpu.VMEM((rows, LANES), F32),
                        pltpu.VMEM((rows, KV_LORA), F32)])
    o = pl.pallas_call(
        functools.partial(_mla_sample_kernel, pages=pages, page=page, tvalid=tvalid),
        grid_spec=grid_spec,
        out_shape=jax.ShapeDtypeStruct((bsz, rows, KV_LORA), F32),
        compiler_params=_params("arbitrary", "arbitrary"),
        name="mla_sample",
    )(page_table, q, knew, ckv_pages, krt_pages)
    o = jnp.transpose(o.reshape(bsz, MLA_HEADS, tvalid, KV_LORA), (0, 2, 1, 3))
    o = jnp.pad(o.reshape(bsz, tvalid, MLA_HEADS * KV_LORA), ((0, 0), (0, tpad - tvalid), (0, 0)))
    return o.reshape(bsz * tpad, MLA_HEADS * KV_LORA)


def _even_out_kernel(x_ref, ya_ref, olat_ref, sgb_ref, wbd_ref, woa_ref, wob_ref, gpost_ref, o_ref):
    yb = _dot(olat_ref[...], wbd_ref[...]) * sgb_ref[...]
    y = _dot(ya_ref[...], woa_ref[...]) + _dot(yb, wob_ref[...])
    o_ref[...] = x_ref[...] + _rms(y, gpost_ref[...])


def even_out(x, ya, olat, sgb, wts, tm):
    n = x.shape[0]
    row = lambda d: pl.BlockSpec((tm, d), lambda i: (i, 0))
    weights = (wts["wbd"], wts["woa"], wts["wob"], wts["gpost"])
    return pl.pallas_call(
        _even_out_kernel,
        grid=(n // tm,),
        in_specs=[row(D_MODEL), row(CONV_WIDTH), row(MLA_HEADS * KV_LORA), row(CONV_WIDTH)]
        + [_full(w.shape) for w in weights],
        out_specs=row(D_MODEL),
        out_shape=jax.ShapeDtypeStruct((n, D_MODEL), F32),
        compiler_params=_params("parallel"),
        name="even_out",
    )(x, ya, olat, sgb, *weights)


def _mem_kv_kernel(x_ref, g_ref, wk_ref, wv_ref, k_ref, v_ref, k4_ref, v4_ref):
    h = _rms(x_ref[...], g_ref[0]).astype(BF16)
    k = _dot(h, wk_ref[0])
    v = _dot(h, wv_ref[0])
    k_ref[0] = k
    v_ref[0] = v
    for hd in range(X_HEADS):
        cs = slice(hd * X_HEAD_DIM, (hd + 1) * X_HEAD_DIM)
        k4_ref[0, :, hd, :] = k[:, cs]
        v4_ref[0, :, hd, :] = v[:, cs]


def mem_kv(mem, g, wk, wv, tm):
    n = mem.shape[0]
    layers = g.shape[0]
    per_layer = lambda shape: pl.BlockSpec((1,) + shape, lambda l, i: (l,) + (0,) * len(shape))
    flat = pl.BlockSpec((1, tm, D_MODEL), lambda l, i: (l, i, 0))
    split = pl.BlockSpec((1, tm, X_HEADS, X_HEAD_DIM), lambda l, i: (l, i, 0, 0))
    return pl.pallas_call(
        _mem_kv_kernel,
        grid=(layers, n // tm),
        in_specs=[pl.BlockSpec((tm, D_MODEL), lambda l, i: (i, 0)), per_layer((1, D_MODEL)),
                  per_layer((D_MODEL, D_MODEL)), per_layer((D_MODEL, D_MODEL))],
        out_specs=(flat, flat, split, split),
        out_shape=(jax.ShapeDtypeStruct((layers, n, D_MODEL), F32),) * 2
        + (jax.ShapeDtypeStruct((layers, n, X_HEADS, X_HEAD_DIM), F32),) * 2,
        compiler_params=_params("parallel", "parallel"),
        name="mem_kv",
    )(mem, g, wk, wv)


def _xattn_body(x_ref, gpre_ref, wqg_ref, wo_ref, gpost_ref, o_ref, att_scr, mem_kv_of, bb, tq):
    x = x_ref[...].reshape(bb * tq, D_MODEL)
    qg = _dot(_rms(x, gpre_ref[...]), wqg_ref[...])
    q = qg[:, :D_MODEL] * (X_HEAD_DIM ** -0.5)
    for b in range(bb):
        for hd in range(X_HEADS):
            cs = slice(hd * X_HEAD_DIM, (hd + 1) * X_HEAD_DIM)
            mk, mv = mem_kv_of(b, hd)
            s = _dot_nt(q[b * tq:(b + 1) * tq, cs], mk)
            p = jnp.exp(s - jnp.max(s, axis=1, keepdims=True))
            o = _dot(p, mv) / jnp.sum(p, axis=1, keepdims=True)
            att_scr[b * tq:(b + 1) * tq, cs] = o
    y = _dot(att_scr[...] * _silu(qg[:, D_MODEL:]), wo_ref[...])
    o_ref[...] = (x + _rms(y, gpost_ref[...])).reshape(bb, tq, D_MODEL)


def _xattn_kernel(x_ref, mk_ref, mv_ref, gpre_ref, wqg_ref, wo_ref, gpost_ref, o_ref, att_scr, *, bb, tq):
    def mem_kv_of(b, hd):
        cs = slice(hd * X_HEAD_DIM, (hd + 1) * X_HEAD_DIM)
        return mk_ref[b, :, cs], mv_ref[b, :, cs]

    _xattn_body(x_ref, gpre_ref, wqg_ref, wo_ref, gpost_ref, o_ref, att_scr, mem_kv_of, bb, tq)


def xattn(x, mk, mv, layer, wts, bb, tq):
    bsz, t, _ = x.shape
    seq = pl.BlockSpec((bb, tq, D_MODEL), lambda b, i: (b, i, 0))
    mem = pl.BlockSpec((None, bb, MEM_LEN, D_MODEL), lambda b, i: (layer, b, 0, 0))
    weights = (wts["gprex"], wts["wqg"], wts["wo"], wts["gpostx"])
    return pl.pallas_call(
        functools.partial(_xattn_kernel, bb=bb, tq=tq),
        grid=(bsz // bb, t // tq),
        in_specs=[seq, mem, mem] + [_full(w.shape) for w in weights],
        out_specs=seq,
        out_shape=jax.ShapeDtypeStruct(x.shape, F32),
        scratch_shapes=[pltpu.VMEM((bb * tq, D_MODEL), F32)],
        compiler_params=_params("parallel", "arbitrary"),
        name="xattn",
    )(x, mk, mv, *weights)


def _xattn_cached_kernel(x_ref, mk_hbm, mv_hbm, gpre_ref, wqg_ref, wo_ref, gpost_ref, o_ref, att_scr,
                         k_buf, v_buf, sem, *, bb, tq, layer):
    g = pl.program_id(0)
    slot = g & 1

    def head_copies(gi, sl):
        copies = []
        for b in range(bb):
            for hd in range(X_HEADS):
                src = (layer, gi * bb + b, slice(None), hd, slice(None))
                copies.append(pltpu.make_async_copy(mk_hbm.at[src], k_buf.at[sl, b, hd], sem.at[sl]))
                copies.append(pltpu.make_async_copy(mv_hbm.at[src], v_buf.at[sl, b, hd], sem.at[sl]))
        return copies

    @pl.when(g == 0)
    def _():
        for cp in head_copies(g, slot):
            cp.start()

    @pl.when(g + 1 < pl.num_programs(0))
    def _():
        for cp in head_copies(g + 1, 1 - slot):
            cp.start()

    for cp in head_copies(g, slot):
        cp.wait()

    _xattn_body(x_ref, gpre_ref, wqg_ref, wo_ref, gpost_ref, o_ref, att_scr,
                lambda b, hd: (k_buf[slot, b, hd], v_buf[slot, b, hd]), bb, tq)


def xattn_cached(x, cache_k, cache_v, layer, wts, bb):
    bsz, tq, _ = x.shape
    seq = pl.BlockSpec((bb, tq, D_MODEL), lambda g: (g, 0, 0))
    hbm = pl.BlockSpec(memory_space=pl.ANY)
    weights = (wts["gprex"], wts["wqg"], wts["wo"], wts["gpostx"])
    buf = pltpu.VMEM((2, bb, X_HEADS, MEM_LEN, X_HEAD_DIM), F32)
    return pl.pallas_call(
        functools.partial(_xattn_cached_kernel, bb=bb, tq=tq, layer=layer),
        grid=(bsz // bb,),
        in_specs=[seq, hbm, hbm] + [_full(w.shape) for w in weights],
        out_specs=seq,
        out_shape=jax.ShapeDtypeStruct(x.shape, F32),
        scratch_shapes=[pltpu.VMEM((bb * tq, D_MODEL), F32), buf, buf, pltpu.SemaphoreType.DMA((2,))],
        compiler_params=_params("arbitrary"),
        name="xattn_cached",
    )(x, cache_k, cache_v, *weights)


def _odd_in_kernel(x_ref, gpre_ref, wz_ref, wxbc_ref, wdt_ref, wdtt_ref, brow_ref, bcol_ref,
                   sz_ref, xbc_ref, dt_ref, dtt_ref):
    h = _rms(x_ref[...], gpre_ref[...]).astype(BF16)
    sz_ref[...] = _silu(_dot(h, wz_ref[...])).astype(BF16)
    xbc_ref[...] = _dot(h, wxbc_ref[...])
    dt_ref[...] = jax.nn.softplus(_dot(h, wdt_ref[...]) + brow_ref[...])
    dtt_ref[...] = jax.nn.softplus(_dot_nt(wdtt_ref[...], h) + bcol_ref[...])


def odd_in(x, wts, tm):
    n = x.shape[0]
    row = lambda d: pl.BlockSpec((tm, d), lambda i: (i, 0))
    weights = (wts["gpre"], wts["wz"], wts["wxbc"], wts["wdt"], wts["wdtt"], wts["brow"], wts["bcol"])
    return pl.pallas_call(
        _odd_in_kernel,
        grid=(n // tm,),
        in_specs=[row(D_MODEL)] + [_full(w.shape) for w in weights],
        out_specs=(row(SSM_INNER), row(XBC_DIM), row(SSM_HEADS),
                   pl.BlockSpec((SSM_HEADS, tm), lambda i: (0, i))),
        out_shape=(jax.ShapeDtypeStruct((n, SSM_INNER), BF16), jax.ShapeDtypeStruct((n, XBC_DIM), F32),
                   jax.ShapeDtypeStruct((n, SSM_HEADS), F32), jax.ShapeDtypeStruct((SSM_HEADS, n), F32)),
        compiler_params=_params("parallel"),
        name="odd_in",
    )(x, *weights)


def _ssd_kernel(*refs, q, t_in, tvalid, has_init):
    if has_init:
        (xs_ref, bm_ref, cm_ref, dt_ref, dtt_ref, arow_ref, acol_ref, drow_ref, e64_ref, tri_ref,
         trit_ref, init_ref, y_ref, st_ref, state_scr) = refs
    else:
        (xs_ref, bm_ref, cm_ref, dt_ref, dtt_ref, arow_ref, acol_ref, drow_ref, e64_ref, tri_ref,
         trit_ref, y_ref, st_ref, state_scr) = refs
    c = pl.program_id(1)

    @pl.when(c == 0)
    def _():
        if has_init:
            state_scr[...] = init_ref[0].T
        else:
            state_scr[...] = jnp.zeros(state_scr.shape, F32)

    def rows(ref):
        v = ref[0]
        if t_in < q:
            v = jnp.concatenate([v, jnp.zeros((q - t_in, v.shape[1]), F32)], axis=0)
        return v

    xs, bm, cm = rows(xs_ref), rows(bm_ref), rows(cm_ref)
    dt = dt_ref[0]
    dtt = dtt_ref[...]
    if tvalid < q:
        dt = jnp.where(lax.broadcasted_iota(jnp.int32, dt.shape, 0) < tvalid, dt, 0.0)
        dtt = jnp.where(lax.broadcasted_iota(jnp.int32, dtt.shape, 1) < tvalid, dtt, 0.0)
    cum = _exact_dot_left(tri_ref[...], dt * arow_ref[...])
    cumt = _exact_dot(dtt * acol_ref[...], trit_ref[...])
    cum_last = cum[q - 1:q, :]
    causal = (lax.broadcasted_iota(jnp.int32, (t_in, q), 1) <= lax.broadcasted_iota(jnp.int32, (t_in, q), 0))
    lane = lax.broadcasted_iota(jnp.int32, (1, LANES), 1)
    left = lane < SSM_HEADDIM

    state = state_scr[...]
    y_parts = []
    for g in range(SSM_GROUPS):
        gs = slice(g * SSM_STATE, (g + 1) * SSM_STATE)
        cm_g = cm[:t_in, gs]
        cb = _dot_nt(cm_g, bm[:, gs])
        for pr in range(SSM_HPG // 2):
            pair = g * (SSM_HPG // 2) + pr
            ps = slice(pair * LANES, (pair + 1) * LANES)
            rhs = jnp.concatenate([xs[:, ps], state[:, ps]], axis=0)
            y_pair = None
            for side in range(2):
                hd = 2 * pair + side
                ccol = jnp.broadcast_to(cum[:t_in, hd:hd + 1], (t_in, q))
                seg = ccol - cumt[hd:hd + 1, :]
                w = cb * jnp.exp(jnp.where(causal, seg, -jnp.inf)) * dtt[hd:hd + 1, :]
                lhs = jnp.concatenate([w, cm_g * jnp.exp(ccol)], axis=1)
                keep = left if side == 0 else jnp.logical_not(left)
                part = _dot(lhs, jnp.where(keep, rhs, 0.0))
                y_pair = part if y_pair is None else y_pair + part
            y_parts.append(y_pair)
    y = jnp.concatenate(y_parts, axis=1) + drow_ref[...] * xs[:t_in]
    y_ref[0] = y.astype(y_ref.dtype)

    to_end = jnp.exp(cum_last - cum) * dt
    toxs = _dot(to_end, e64_ref[...]) * xs
    dec = _exact_dot(jnp.broadcast_to(jnp.exp(cum_last), (SUBLANES, SSM_HEADS)), e64_ref[...])[0:1, :]
    new_parts = []
    for g in range(SSM_GROUPS):
        gs = slice(g * SSM_STATE, (g + 1) * SSM_STATE)
        hs = slice(g * GROUP_WIDTH, (g + 1) * GROUP_WIDTH)
        new_parts.append(_dot(bm[:, gs].T, toxs[:, hs]))
    new_state = state * dec + jnp.concatenate(new_parts, axis=1)
    state_scr[...] = new_state

    @pl.when(c == pl.num_programs(1) - 1)
    def _():
        st_ref[0] = new_state.T


def ssd(xbc, dt, dtt, wts, init, q, t_in, tvalid, y_dtype):
    bsz, t, _ = xbc.shape
    nc = max(t // q, 1)
    nxb = SSM_INNER // (SSM_GROUPS * SSM_STATE)
    in_specs = [pl.BlockSpec((1, t_in, SSM_INNER), lambda b, c: (b, c, 0)),
                pl.BlockSpec((1, t_in, SSM_GROUPS * SSM_STATE), lambda b, c: (b, c, nxb)),
                pl.BlockSpec((1, t_in, SSM_GROUPS * SSM_STATE), lambda b, c: (b, c, nxb + 1)),
                pl.BlockSpec((1, q, SSM_HEADS), lambda b, c: (b, c, 0)),
                pl.BlockSpec((SSM_HEADS, q), lambda b, c: (0, b * nc + c))]
    consts = (wts["arow"], wts["acol"], wts["drow"], wts["e64"], wts["tri"], wts["trit"])
    in_specs += [_full(w.shape) for w in consts]
    args = [xbc, xbc, xbc, dt, dtt, *consts]
    if init is not None:
        in_specs.append(pl.BlockSpec((1, SSM_INNER, SSM_STATE), lambda b, c: (b, 0, 0)))
        args.append(init)
    return pl.pallas_call(
        functools.partial(_ssd_kernel, q=q, t_in=t_in, tvalid=tvalid, has_init=init is not None),
        grid=(bsz, nc),
        in_specs=in_specs,
        out_specs=(pl.BlockSpec((1, t_in, SSM_INNER), lambda b, c: (b, c, 0)),
                   pl.BlockSpec((1, SSM_INNER, SSM_STATE), lambda b, c: (b, 0, 0))),
        out_shape=(jax.ShapeDtypeStruct((bsz, t, SSM_INNER), y_dtype),
                   jax.ShapeDtypeStruct((bsz, SSM_INNER, SSM_STATE), F32)),
        scratch_shapes=[pltpu.VMEM((SSM_STATE, SSM_INNER), F32)],
        compiler_params=_params("parallel", "arbitrary"),
        name="ssd",
    )(*args)


def _odd_out_kernel(x_ref, y_ref, sz_ref, gn_ref, wout_ref, gpost_ref, o_ref):
    v = y_ref[...].astype(F32) * sz_ref[...].astype(F32)
    gn = gn_ref[...]
    parts = []
    for g in range(SSM_GROUPS):
        gs = slice(g * GROUP_WIDTH, (g + 1) * GROUP_WIDTH)
        parts.append(_rms(v[:, gs], gn[:, gs]).astype(BF16))
    y = _dot(jnp.concatenate(parts, axis=1), wout_ref[...])
    o_ref[...] = x_ref[...] + _rms(y, gpost_ref[...])


def odd_out(x, y, sz, wts, tm):
    n = x.shape[0]
    row = lambda d: pl.BlockSpec((tm, d), lambda i: (i, 0))
    weights = (wts["gnorm"], wts["wout"], wts["gpost"])
    return pl.pallas_call(
        _odd_out_kernel,
        grid=(n // tm,),
        in_specs=[row(D_MODEL), row(SSM_INNER), row(SSM_INNER)] + [_full(w.shape) for w in weights],
        out_specs=row(D_MODEL),
        out_shape=jax.ShapeDtypeStruct((n, D_MODEL), F32),
        compiler_params=_params("parallel"),
        name="odd_out",
    )(x, y, sz, *weights)


def _even_weights(i, l, p):
    w_in = p["e_w_in"][l]
    c = CONV_WIDTH
    o_q, o_kv, o_bg = 3 * c, 3 * c + Q_LORA, 3 * c + Q_LORA + KV_LORA + ROPE_DIM
    half = ROPE_DIM // 2
    w_kr = w_in[:, o_kv + KV_LORA:o_bg]
    w_uq = p["e_w_uq"][l]
    uq_r = w_uq[:, :, NOPE_DIM:]
    w_uv = jnp.transpose(p["e_w_uv"][l], (1, 0, 2))
    eye = jnp.eye(MLA_HEADS, dtype=F32)
    wbd = (eye[:, None, :, None] * w_uv[:, :, None, :]).reshape(MLA_HEADS * KV_LORA, MLA_HEADS * V_DIM)
    w_out = p["e_w_out"][l]
    return {
        "gpre": p["norm_pre_mix"][i][None], "gpost": p["norm_post_mix"][i][None],
        "wa": w_in[:, :o_q].astype(BF16), "wq": w_in[:, o_q:o_kv].astype(BF16),
        "wkv": jnp.concatenate([w_in[:, o_kv:o_bg], -w_kr[:, half:], w_kr[:, :half]], axis=1).astype(BF16),
        "wbg": w_in[:, o_bg:].astype(BF16),
        "gq": p["e_q_norm"][l][None], "gkv": p["e_kv_norm"][l][None],
        "wuqn": w_uq[:, :, :NOPE_DIM].reshape(Q_LORA, MLA_HEADS * NOPE_DIM).astype(BF16),
        "wuqr": uq_r.reshape(Q_LORA, MLA_HEADS * ROPE_DIM).astype(BF16),
        "wuqrot": jnp.concatenate([-uq_r[:, :, half:], uq_r[:, :, :half]], axis=2)
        .reshape(Q_LORA, MLA_HEADS * ROPE_DIM).astype(BF16),
        "wukt": jnp.transpose(p["e_w_uk"][l], (1, 2, 0)).astype(BF16),
        "conv_w": p["e_conv_w"][l], "conv_b": p["e_conv_b"][l][None],
        "ln_w": p["e_ln_w"][l][None], "ln_b": p["e_ln_b"][l][None],
        "wbd": wbd.astype(BF16), "woa": w_out[:c].astype(BF16), "wob": w_out[c:].astype(BF16),
    }


def _odd_weights(i, l, p):
    w_in = p["o_w_in"][l]
    w_dt = w_in[:, SSM_INNER + XBC_DIM:]
    a = -jnp.exp(p["o_a_log"][l])
    q = SSM_CHUNK
    tri = jnp.tril(jnp.ones((q, q), F32))
    return {
        "gpre": p["norm_pre_mix"][i][None], "gpost": p["norm_post_mix"][i][None],
        "wz": w_in[:, :SSM_INNER].astype(BF16),
        "wxbc": w_in[:, SSM_INNER:SSM_INNER + XBC_DIM].astype(BF16),
        "wdt": w_dt.astype(BF16), "wdtt": w_dt.T.astype(BF16),
        "brow": p["o_dt_bias"][l][None], "bcol": p["o_dt_bias"][l][:, None],
        "conv_w": p["o_conv_w"][l], "conv_b": p["o_conv_b"][l][None],
        "arow": a[None], "acol": a[:, None],
        "drow": jnp.repeat(p["o_d"][l], SSM_HEADDIM)[None],
        "e64": jnp.repeat(jnp.eye(SSM_HEADS, dtype=F32), SSM_HEADDIM, axis=1).astype(BF16),
        "tri": tri.astype(BF16), "trit": tri.T.astype(BF16),
        "gnorm": p["o_norm"][l][None], "wout": p["o_w_out"][l].astype(BF16),
    }


def _xattn_weights(i, p):
    return {"gprex": p["norm_pre_x"][i][None], "gpostx": p["norm_post_x"][i][None],
            "wqg": p["x_w_qg"][i].astype(BF16), "wo": p["x_w_o"][i].astype(BF16)}


def kernel(x_prompt, x_sample, mem_prompt, cache_ckv, cache_krope, page_table, state_conv_a,
           state_conv_c, state_ssm, cache_mem_k, cache_mem_v, norm_pre_mix, norm_post_mix,
           norm_pre_x, norm_post_x, norm_mem, x_w_qg, x_w_k, x_w_v, x_w_o, e_w_in, e_conv_w,
           e_conv_b, e_ln_w, e_ln_b, e_q_norm, e_kv_norm, e_w_uq, e_w_uk, e_w_uv, e_w_out,
           o_w_in, o_conv_w, o_conv_b, o_dt_bias, o_a_log, o_d, o_norm, o_w_out):
    p = dict(norm_pre_mix=norm_pre_mix, norm_post_mix=norm_post_mix, norm_pre_x=norm_pre_x,
             norm_post_x=norm_post_x, e_w_in=e_w_in, e_conv_w=e_conv_w, e_conv_b=e_conv_b,
             e_ln_w=e_ln_w, e_ln_b=e_ln_b, e_q_norm=e_q_norm, e_kv_norm=e_kv_norm, e_w_uq=e_w_uq,
             e_w_uk=e_w_uk, e_w_uv=e_w_uv, e_w_out=e_w_out, o_w_in=o_w_in, o_conv_w=o_conv_w,
             o_conv_b=o_conv_b, o_dt_bias=o_dt_bias, o_a_log=o_a_log, o_d=o_d, o_norm=o_norm,
             o_w_out=o_w_out, x_w_qg=x_w_qg, x_w_o=x_w_o)
    bp, sp, _ = x_prompt.shape
    bs, ss, _ = x_sample.shape
    depth = norm_pre_mix.shape[0]
    n_pages, page = page_table.shape[1], cache_ckv.shape[1]
    past_len = n_pages * page
    tpad = SUBLANES
    assert ss <= tpad and sp % SSM_CHUNK == 0
    np_, ns = bp * sp, bs * tpad
    tm_p, tm_s = 512, ns
    nb_s = 8 if bs % 8 == 0 else 1

    half = ROPE_DIM // 2
    inv = ROPE_THETA ** (-jnp.arange(half, dtype=F32) / half)
    inv_lanes = jnp.tile(inv, LANES // half)[None]
    cos_p, sin_p = rope_table(inv_lanes, sp, 0, sp)
    cos_s, sin_s = rope_table(inv_lanes, ns, past_len, tpad)

    xp = x_prompt.reshape(np_, D_MODEL)
    xs = jnp.pad(x_sample, ((0, 0), (0, tpad - ss), (0, 0))).reshape(ns, D_MODEL)
    mem_flat = mem_prompt.reshape(bp * MEM_LEN, D_MODEL)
    mk, mv, mk4, mv4 = mem_kv(mem_flat, norm_mem[:, None], x_w_k.astype(BF16), x_w_v.astype(BF16), 512)
    mem_k_p = mk.reshape(depth, bp, MEM_LEN, D_MODEL)
    mem_v_p = mv.reshape(depth, bp, MEM_LEN, D_MODEL)

    outs = {k: [] for k in ("ckv_p", "ckv_s", "kr_p", "kr_s", "ca_p", "ca_s", "cc_p", "cc_s",
                            "ssm_p", "ssm_s", "mk", "mv")}
    for i in range(depth):
        l = i // 2
        if i % 2 == 0:
            w = _even_weights(i, l, p)
            glu, sga, sgb, qcat, kcat, ckv, kr = even_in(xp, w, cos_p, sin_p, tm_p)
            glu3 = glu.reshape(bp, sp, CONV_WIDTH)
            ya = causal_conv(glu3, jnp.zeros((bp, CONV_A_HALO, CONV_WIDTH), F32), w["conv_w"], w["conv_b"],
                             256, CONV_A_HALO, CONV_WIDTH,
                             ln=(w["ln_w"], w["ln_b"], sga.reshape(bp, sp, CONV_WIDTH)), out_dtype=BF16)
            olat = mla_prompt(qcat, kcat.reshape(bp, sp, QK_CAT), bp, sp, 256)
            yp = even_out(xp, ya.reshape(np_, CONV_WIDTH), olat, sgb, w, tm_p)
            outs["ca_p"].append(glu3[:, sp - (CONV_K - 1):])
            outs["ckv_p"].append(ckv.reshape(bp, sp, KV_LORA))
            outs["kr_p"].append(kr.reshape(bp, sp, ROPE_DIM))
            glu, sga, sgb, qcat, kcat, ckv, kr = even_in(xs, w, cos_s, sin_s, tm_s)
            glu3 = glu.reshape(bs, tpad, CONV_WIDTH)
            hist = state_conv_a[l]
            hist_pad = jnp.pad(hist, ((0, 0), (CONV_A_HALO - (CONV_K - 1), 0), (0, 0)))
            ya = causal_conv(glu3, hist_pad, w["conv_w"], w["conv_b"], tpad, CONV_A_HALO, CONV_WIDTH,
                             ln=(w["ln_w"], w["ln_b"], sga.reshape(bs, tpad, CONV_WIDTH)), nb=nb_s)
            knew =jnp.pad(kcat.reshape(bs, tpad, QK_CAT), ((0, 0), (0, LANES - tpad), (0, 0)))
            olat = mla_sample(qcat, knew, cache_ckv[:, :, l, :],
                              jnp.transpose(cache_krope[:, :, l, :], (0, 2, 1)), page_table, tpad, ss,
                              min(32, n_pages))
            ys = even_out(xs, ya.reshape(ns, CONV_WIDTH), olat, sgb, w, tm_s)
            outs["ca_s"].append(jnp.concatenate([hist[:, ss:], glu3[:, :ss]], axis=1))
            outs["ckv_s"].append(ckv.reshape(bs, tpad, KV_LORA)[:, :ss])
            outs["kr_s"].append(kr.reshape(bs, tpad, ROPE_DIM)[:, :ss])
        else:
            w = _odd_weights(i, l, p)
            q = SSM_CHUNK
            sz, xbc, dt, dtt = odd_in(xp, w, 256)
            xbc3 = xbc.reshape(bp, sp, XBC_DIM)
            xbc_c = causal_conv(xbc3, jnp.zeros((bp, CONV_C_HALO, XBC_DIM), F32), w["conv_w"], w["conv_b"],
                                256, CONV_C_HALO, 512)
            y, st = ssd(xbc_c, dt.reshape(bp, sp, SSM_HEADS), dtt, w, None, q, q, q, BF16)
            yp = odd_out(xp, y.reshape(np_, SSM_INNER), sz, w, tm_p)
            outs["cc_p"].append(xbc3[:, sp - (SSM_CONV - 1):])
            outs["ssm_p"].append(st.reshape(bp, SSM_HEADS, SSM_HEADDIM, SSM_STATE))
            sz, xbc, dt, dtt = odd_in(xs, w, tm_s)
            xbc3 = xbc.reshape(bs, tpad, XBC_DIM)
            hist = state_conv_c[l]
            hist_pad = jnp.pad(hist, ((0, 0), (CONV_C_HALO - (SSM_CONV - 1), 0), (0, 0)))
            xbc_c = causal_conv(xbc3, hist_pad, w["conv_w"], w["conv_b"], tpad, CONV_C_HALO, 512, nb=nb_s)
            dt_pad = jnp.pad(dt.reshape(bs, tpad, SSM_HEADS), ((0, 0), (0, q - tpad), (0, 0)))
            dtt_pad = jnp.pad(dtt.reshape(SSM_HEADS, bs, tpad), ((0, 0), (0, 0), (0, q - tpad)))
            y, st = ssd(xbc_c, dt_pad, dtt_pad.reshape(SSM_HEADS, bs * q), w,
                        state_ssm[l].reshape(bs, SSM_INNER, SSM_STATE), q, tpad, ss, F32)
            ys = odd_out(xs, y.reshape(ns, SSM_INNER), sz, w, tm_s)
            outs["cc_s"].append(jnp.concatenate([hist[:, ss:], xbc3[:, :ss]], axis=1)[:, -(SSM_CONV - 1):])
            outs["ssm_s"].append(st.reshape(bs, SSM_HEADS, SSM_HEADDIM, SSM_STATE))
        xp, xs = yp, ys
        wx = _xattn_weights(i, p)
        xp = xattn(xp.reshape(bp, sp, D_MODEL), mem_k_p, mem_v_p, i, wx, 1, 512).reshape(np_, D_MODEL)
        xs = xattn_cached(xs.reshape(bs, tpad, D_MODEL), cache_mem_k, cache_mem_v, i, wx, 4).reshape(ns, D_MODEL)

    return (xp.reshape(bp, sp, D_MODEL), xs.reshape(bs, tpad, D_MODEL)[:, :ss],
            jnp.stack(outs["ckv_p"], axis=2), jnp.stack(outs["ckv_s"], axis=2),
            jnp.stack(outs["kr_p"], axis=2), jnp.stack(outs["kr_s"], axis=2),
            jnp.stack(outs["ca_p"]), jnp.stack(outs["ca_s"]),
            jnp.stack(outs["cc_p"]), jnp.stack(outs["cc_s"]),
            jnp.stack(outs["ssm_p"]), jnp.stack(outs["ssm_s"]),
            mk4.reshape(depth, bp, MEM_LEN, X_HEADS, X_HEAD_DIM),
            mv4.reshape(depth, bp, MEM_LEN, X_HEADS, X_HEAD_DIM))
```

```python
import functools

import jax
import jax.numpy as jnp
from jax import lax
from jax.experimental import pallas as pl
from jax.experimental.pallas import tpu as pltpu

F32 = jnp.float32
BF16 = jnp.bfloat16

D_MODEL = 1024
CONV_WIDTH = 512
CONV_K = 31
MLA_HEADS = 8
Q_LORA = 256
KV_LORA = 128
NOPE_DIM = 64
ROPE_DIM = 32
V_DIM = 64
QK_CAT = KV_LORA + ROPE_DIM
LOG2E = 1.4426950408889634
MLA_SCALE = (NOPE_DIM + ROPE_DIM) ** -0.5 * LOG2E
ROPE_THETA = 10000.0
SSM_INNER = 2048
SSM_HEADDIM = 64
SSM_HEADS = 32
SSM_GROUPS = 4
SSM_HPG = 8
SSM_STATE = 128
SSM_CONV = 4
SSM_CHUNK = 128
XBC_DIM = SSM_INNER + 2 * SSM_GROUPS * SSM_STATE
GROUP_WIDTH = SSM_INNER // SSM_GROUPS
MEM_LEN = 256
X_HEADS = 4
X_HEAD_DIM = 256
EPS = 1e-6

LANES = 128
SUBLANES = 8
CONV_A_HALO = 32
CONV_C_HALO = 8
VMEM_LIMIT = 56 * 1024 * 1024


def _params(*sem):
    return pltpu.CompilerParams(dimension_semantics=sem, vmem_limit_bytes=VMEM_LIMIT)


def _rms(x, w):
    return x * lax.rsqrt(jnp.mean(x * x, axis=-1, keepdims=True) + EPS) * w


def _silu(x):
    return x * jax.nn.sigmoid(x)


def _dot(a, b):
    return jnp.dot(a.astype(BF16), b.astype(BF16), preferred_element_type=F32)


def _dot_nt(a, b):
    return lax.dot_general(a.astype(BF16), b.astype(BF16), (((1,), (1,)), ((), ())),
                           preferred_element_type=F32)


def _split3(x):
    hi = x.astype(BF16)
    r = x - hi.astype(F32)
    mid = r.astype(BF16)
    lo = (r - mid.astype(F32)).astype(BF16)
    return hi, mid, lo


def _exact_dot(x, e):
    hi, mid, lo = _split3(x)
    return (jnp.dot(hi, e, preferred_element_type=F32) + jnp.dot(mid, e, preferred_element_type=F32)
            + jnp.dot(lo, e, preferred_element_type=F32))


def _exact_dot_left(e, x):
    hi, mid, lo = _split3(x)
    return (jnp.dot(e, hi, preferred_element_type=F32) + jnp.dot(e, mid, preferred_element_type=F32)
            + jnp.dot(e, lo, preferred_element_type=F32))


def _full(shape):
    nd = len(shape)
    return pl.BlockSpec(shape, lambda *_: (0,) * nd)


def _rope_table_kernel(inv_ref, cos_ref, sin_ref, *, base, period):
    row = lax.broadcasted_iota(jnp.int32, cos_ref.shape, 0)
    pos = (base + (row & (period - 1))).astype(F32)
    ang = pos * inv_ref[...]
    cos_ref[...] = jnp.cos(ang)
    sin_ref[...] = jnp.sin(ang)


def rope_table(inv_lanes, rows, base, period):
    assert period & (period - 1) == 0
    return pl.pallas_call(
        functools.partial(_rope_table_kernel, base=base, period=period),
        out_shape=(jax.ShapeDtypeStruct((rows, LANES), F32),) * 2,
        name="rope_table",
    )(inv_lanes)


def _even_in_kernel(x_ref, gpre_ref, wa_ref, wq_ref, wkv_ref, wbg_ref, gq_ref, gkv_ref,
                    wuqn_ref, wuqr_ref, wuqrot_ref, wukt_ref, cos_ref, sin_ref,
                    glu_ref, sga_ref, sgb_ref, qcat_ref, kcat_ref, ckv_ref, kr_ref):
    h = _rms(x_ref[...], gpre_ref[...]).astype(BF16)
    ua = _dot(h, wa_ref[...])
    glu_ref[...] = ua[:, :CONV_WIDTH] * jax.nn.sigmoid(ua[:, CONV_WIDTH:2 * CONV_WIDTH])
    sga_ref[...] = _silu(ua[:, 2 * CONV_WIDTH:])
    sgb_ref[...] = _silu(_dot(h, wbg_ref[...])).astype(BF16)
    cos = cos_ref[...]
    sin = sin_ref[...]
    kv = _dot(h, wkv_ref[...])
    ckv = _rms(kv[:, :KV_LORA], gkv_ref[...])
    kr = (kv[:, KV_LORA:KV_LORA + ROPE_DIM] * cos[:, :ROPE_DIM]
          + kv[:, KV_LORA + ROPE_DIM:] * sin[:, :ROPE_DIM])
    ckv_ref[...] = ckv
    kr_ref[...] = kr
    kcat_ref[:, :KV_LORA] = ckv.astype(BF16)
    kcat_ref[:, KV_LORA:] = kr.astype(BF16)
    qn = _rms(_dot(h, wq_ref[...]), gq_ref[...]).astype(BF16)
    qnope = _dot(qn, wuqn_ref[...])
    cos2 = jnp.concatenate([cos, cos], axis=1)
    sin2 = jnp.concatenate([sin, sin], axis=1)
    qr = (_dot(qn, wuqr_ref[...]) * cos2 + _dot(qn, wuqrot_ref[...]) * sin2) * MLA_SCALE
    for hd in range(MLA_HEADS):
        ql = _dot(qnope[:, hd * NOPE_DIM:(hd + 1) * NOPE_DIM], wukt_ref[hd]) * MLA_SCALE
        qcat_ref[hd, :, :KV_LORA] = ql.astype(BF16)
        qcat_ref[hd, :, KV_LORA:] = qr[:, hd * ROPE_DIM:(hd + 1) * ROPE_DIM].astype(BF16)


def even_in(x, wts, cos, sin, tm):
    n = x.shape[0]
    nblk = cos.shape[0] // tm
    row = lambda d: pl.BlockSpec((tm, d), lambda i: (i, 0))
    tab = pl.BlockSpec((tm, LANES), lambda i: (i % nblk, 0))
    weights = (wts["gpre"], wts["wa"], wts["wq"], wts["wkv"], wts["wbg"], wts["gq"], wts["gkv"],
               wts["wuqn"], wts["wuqr"], wts["wuqrot"], wts["wukt"])
    return pl.pallas_call(
        _even_in_kernel,
        grid=(n // tm,),
        in_specs=[row(D_MODEL)] + [_full(w.shape) for w in weights] + [tab, tab],
        out_specs=(row(CONV_WIDTH), row(CONV_WIDTH), row(CONV_WIDTH),
                   pl.BlockSpec((MLA_HEADS, tm, QK_CAT), lambda i: (0, i, 0)),
                   row(QK_CAT), row(KV_LORA), row(ROPE_DIM)),
        out_shape=(jax.ShapeDtypeStruct((n, CONV_WIDTH), F32),) * 2 + (
            jax.ShapeDtypeStruct((n, CONV_WIDTH), BF16),
            jax.ShapeDtypeStruct((MLA_HEADS, n, QK_CAT), BF16),
            jax.ShapeDtypeStruct((n, QK_CAT), BF16),
            jax.ShapeDtypeStruct((n, KV_LORA), F32),
            jax.ShapeDtypeStruct((n, ROPE_DIM), F32)),
        compiler_params=_params("parallel"),
        name="even_in",
    )(x, *weights, cos, sin)


def _conv_kernel(hist_ref, prev_ref, cur_ref, w_ref, b_ref, *rest, taps, halo, row_chunk, lane_chunk,
                 layernorm):
    if layernorm:
        lnw_ref, lnb_ref, gate_ref, out_ref, ext_scr, *shift_scr = rest
    else:
        out_ref, ext_scr, *shift_scr = rest
    nb, tt, width = cur_ref.shape
    off = halo - (taps - 1)

    def ext_rows(d, r0, cs):
        a, r = divmod(d, SUBLANES)
        if not shift_scr or r == 0:
            return ext_scr[d + r0:d + r0 + row_chunk, cs]
        return shift_scr[0][r - 1, SUBLANES * a + r0:SUBLANES * a + r0 + row_chunk, cs]

    for sq in range(nb):
        ext_scr[0:halo] = jnp.where(pl.program_id(1) == 0, hist_ref[sq], prev_ref[sq])
        ext_scr[halo:halo + tt] = cur_ref[sq]
        if shift_scr:
            sh_scr, = shift_scr
            span = sh_scr.shape[1]
            ext = ext_scr[...]
            for r in range(1, SUBLANES):
                sh_scr[r - 1] = pltpu.roll(ext, halo + tt - r, axis=0)[:span]

        for r0 in range(0, tt, row_chunk):
            for c0 in range(0, width, lane_chunk):
                cs = slice(c0, c0 + lane_chunk)
                if shift_scr:
                    taps_in = [ext_rows(off + k, r0, cs) for k in range(taps)]
                else:
                    win = ext_scr[r0 + halo - SUBLANES:r0 + halo + row_chunk, cs]
                    taps_in = [(pltpu.roll(win, taps - 1 - k, axis=0) if k < taps - 1 else win)[SUBLANES:]
                               for k in range(taps)]
                acc = w_ref[0:1, cs] * taps_in[0]
                for k in range(1, taps):
                    acc = acc + w_ref[k:k + 1, cs] * taps_in[k]
                acc = acc + b_ref[:, cs]
                if layernorm:
                    xc = acc - jnp.mean(acc, axis=-1, keepdims=True)
                    var = jnp.mean(xc * xc, axis=-1, keepdims=True)
                    ln = xc * lax.rsqrt(var + EPS) * lnw_ref[...] + lnb_ref[...]
                    out_ref[sq, r0:r0 + row_chunk, :] = (_silu(ln) * gate_ref[sq, r0:r0 + row_chunk, :]
                                                         ).astype(out_ref.dtype)
                else:
                    out_ref[sq, r0:r0 + row_chunk, cs] = _silu(acc)


def causal_conv(x, hist, w, b, tt, halo, lane_chunk, ln=None, out_dtype=F32, nb=1):
    bsz, t, width = x.shape
    taps = w.shape[0]
    seq = pl.BlockSpec((nb, tt, width), lambda bi, ti: (bi, ti, 0))
    first = pl.BlockSpec((nb, halo, width), lambda bi, ti: (bi, 0, 0))
    if t == tt:
        prev, prev_spec = hist, first
    else:
        ratio = tt // halo
        prev = x
        prev_spec = pl.BlockSpec((nb, halo, width), lambda bi, ti: (bi, jnp.maximum(ti * ratio - 1, 0), 0))
    in_specs = [first, prev_spec, seq, _full(w.shape), _full(b.shape)]
    args = [hist, prev, x, w, b]
    if ln is not None:
        in_specs += [_full(ln[0].shape), _full(ln[1].shape), seq]
        args += list(ln)
    return pl.pallas_call(
        functools.partial(_conv_kernel, taps=taps, halo=halo, row_chunk=min(32, tt),
                          lane_chunk=lane_chunk, layernorm=ln is not None),
        grid=(bsz // nb, t // tt),
        in_specs=in_specs,
        out_specs=seq,
        out_shape=jax.ShapeDtypeStruct((bsz, t, width), out_dtype),
        scratch_shapes=[pltpu.VMEM((halo + tt, width), F32)]
        + ([pltpu.VMEM((SUBLANES - 1, halo + tt - SUBLANES, width), F32)] if taps > SUBLANES else []),
        compiler_params=_params("parallel", "arbitrary"),
        name="causal_conv_ln" if ln is not None else "causal_conv",
    )(*args)


def _mla_prompt_kernel(q_ref, k_ref, o_ref, s_scr, acc_scr, mx_scr, sum_scr, *, tq):
    i = pl.program_id(1)
    rows = MLA_HEADS * tq
    q = q_ref[...].reshape(rows, QK_CAT)

    def keys(j):
        return k_ref[0, pl.ds(pl.multiple_of(j * tq, tq), tq), :]

    def fold(x):
        parts = [x[:, c:c + LANES] for c in range(0, tq, LANES)]
        return parts

    mx_scr[...] = jnp.full(mx_scr.shape, -jnp.inf, F32)

    def pass1(j, carry):
        s = _dot_nt(q, keys(j))
        s_scr[j] = s
        mx_scr[...] = functools.reduce(jnp.maximum, fold(s), mx_scr[...])
        return carry

    lax.fori_loop(0, i, pass1, 0)
    s = _dot_nt(q, keys(i))
    qpos = lax.broadcasted_iota(jnp.int32, s.shape, 0) & (tq - 1)
    kpos = lax.broadcasted_iota(jnp.int32, s.shape, 1)
    s = jnp.where(kpos <= qpos, s, -jnp.inf)
    s_scr[i] = s
    m = jnp.max(functools.reduce(jnp.maximum, fold(s), mx_scr[...]), axis=1, keepdims=True)

    acc_scr[...] = jnp.zeros(acc_scr.shape, F32)
    sum_scr[...] = jnp.zeros(sum_scr.shape, F32)

    def pass2(j, carry):
        p = jnp.exp2(s_scr[j] - m)
        acc_scr[...] += _dot(p, keys(j)[:, :KV_LORA])
        sum_scr[...] = functools.reduce(jnp.add, fold(p), sum_scr[...])
        return carry

    lax.fori_loop(0, i + 1, pass2, 0)
    o = acc_scr[...] / jnp.sum(sum_scr[...], axis=1, keepdims=True)
    for hd in range(MLA_HEADS):
        o_ref[:, hd * KV_LORA:(hd + 1) * KV_LORA] = o[hd * tq:(hd + 1) * tq, :].astype(BF16)


def mla_prompt(qcat, kcat, bsz, t, tq):
    nq = t // tq
    rows = MLA_HEADS * tq
    return pl.pallas_call(
        functools.partial(_mla_prompt_kernel, tq=tq),
        grid=(bsz, nq),
        in_specs=[pl.BlockSpec((MLA_HEADS, tq, QK_CAT), lambda b, i: (0, b * nq + i, 0)),
                  pl.BlockSpec((1, t, QK_CAT), lambda b, i: (b, 0, 0))],
        out_specs=pl.BlockSpec((tq, MLA_HEADS * KV_LORA), lambda b, i: (b * nq + i, 0)),
        out_shape=jax.ShapeDtypeStruct((bsz * t, MLA_HEADS * KV_LORA), BF16),
        scratch_shapes=[pltpu.VMEM((nq, rows, tq), F32), pltpu.VMEM((rows, KV_LORA), F32),
                        pltpu.VMEM((rows, LANES), F32), pltpu.VMEM((rows, LANES), F32)],
        compiler_params=_params("parallel", "arbitrary"),
        name="mla_prompt",
    )(qcat, kcat)


def _mla_sample_kernel(pt_ref, q_ref, knew_ref, ckv_hbm, krt_hbm, o_ref, ckv_buf, krt_buf, sem,
                       kc_scr, krt_scr, m_scr, l_scr, acc_scr, *, pages, page, tvalid):
    b = pl.program_id(0)
    j = pl.program_id(1)
    steps = pl.num_programs(1)
    g = b * steps + j
    slot = g & 1

    def page_copies(bi, ji, sl):
        copies = []
        for i in range(pages):
            pg = pt_ref[bi, ji * pages + i]
            copies.append(pltpu.make_async_copy(ckv_hbm.at[pg], ckv_buf.at[sl, pl.ds(i * page, page)],
                                                sem.at[sl]))
            copies.append(pltpu.make_async_copy(krt_hbm.at[pg], krt_buf.at[sl, i], sem.at[sl]))
        return copies

    def start_all(copies):
        for n, cp in enumerate(copies):
            cp.start(priority=(n // 2) % 2)

    @pl.when(g == 0)
    def _():
        start_all(page_copies(b, j, slot))

    @pl.when(g + 1 < pl.num_programs(0) * steps)
    def _():
        wrap = j + 1 == steps
        start_all(page_copies(jnp.where(wrap, b + 1, b), jnp.where(wrap, 0, j + 1), 1 - slot))

    for cp in page_copies(b, j, slot):
        cp.wait()

    @pl.when(j == 0)
    def _():
        m_scr[...] = jnp.full(m_scr.shape, -jnp.inf, F32)
        l_scr[...] = jnp.zeros(l_scr.shape, F32)
        acc_scr[...] = jnp.zeros(acc_scr.shape, F32)

    q = q_ref[0]

    def update(s, v):
        m_prev = m_scr[...]
        m_new = jnp.maximum(m_prev, jnp.max(s, axis=1, keepdims=True))
        alpha = jnp.exp2(m_prev - m_new)
        p = jnp.exp2(s - m_new[:, 0:1])
        l_scr[...] = alpha * l_scr[...] + jnp.sum(p, axis=1, keepdims=True)
        m_scr[...] = m_new
        acc_scr[...] = alpha * acc_scr[...] + _dot(p, v)

    kc_scr[...] = ckv_buf[slot].astype(BF16)
    for i in range(pages):
        krt_scr[:, i * page:(i + 1) * page] = krt_buf[slot, i].astype(BF16)
    kc = kc_scr[...]
    update(_dot_nt(q[:, :KV_LORA], kc) + _dot(q[:, KV_LORA:], krt_scr[...]), kc)

    @pl.when(j == pl.num_programs(1) - 1)
    def _():
        knew = knew_ref[0]
        s = _dot_nt(q, knew)
        qpos = lax.rem(lax.broadcasted_iota(jnp.int32, s.shape, 0), tvalid)
        kpos = lax.broadcasted_iota(jnp.int32, s.shape, 1)
        s = jnp.where((kpos <= qpos) & (kpos < tvalid), s, -jnp.inf)
        update(s, knew[:, :KV_LORA])
        o_ref[0] = acc_scr[...] / l_scr[...]


def mla_sample(qcat, knew, ckv_pages, krt_pages, page_table, tpad, tvalid, pages):
    bsz, n_pages = page_table.shape
    page = ckv_pages.shape[1]
    steps = n_pages // pages
    rows = MLA_HEADS * tvalid
    q = qcat.astype(F32).reshape(MLA_HEADS, bsz, tpad, QK_CAT)[:, :, :tvalid]
    q = jnp.transpose(q, (1, 0, 2, 3)).reshape(bsz, rows, QK_CAT)

    grid_spec = pltpu.PrefetchScalarGridSpec(
        num_scalar_prefetch=1,
        grid=(bsz, steps),
        in_specs=[pl.BlockSpec((1, rows, QK_CAT), lambda b, j, pt: (b, 0, 0)),
                  pl.BlockSpec((1, knew.shape[1], QK_CAT), lambda b, j, pt: (b, 0, 0)),
                  pl.BlockSpec(memory_space=pl.ANY), pl.BlockSpec(memory_space=pl.ANY)],
        out_specs=pl.BlockSpec((1, rows, KV_LORA), lambda b, j, pt: (b, 0, 0)),
        scratch_shapes=[pltpu.VMEM((2, pages * page, KV_LORA), F32),
                        pltpu.VMEM((2, pages, ROPE_DIM, page), F32),
                        pltpu.SemaphoreType.DMA((2,)),
                        pltpu.VMEM((pages * page, KV_LORA), BF16),
                        pltpu.VMEM((ROPE_DIM, pages * page), BF16),
                        pltpu.VMEM((rows, LANES), F32), pltpu.VMEM((rows, LANES), F32),
                        pltpu.VMEM((rows, KV_LORA), F32)])
    o = pl.pallas_call(
        functools.partial(_mla_sample_kernel, pages=pages, page=page, tvalid=tvalid),
        grid_spec=grid_spec,
        out_shape=jax.ShapeDtypeStruct((bsz, rows, KV_LORA), F32),
        compiler_params=_params("arbitrary", "arbitrary"),
        name="mla_sample",
    )(page_table, q, knew, ckv_pages, krt_pages)
    o = jnp.transpose(o.reshape(bsz, MLA_HEADS, tvalid, KV_LORA), (0, 2, 1, 3))
    o = jnp.pad(o.reshape(bsz, tvalid, MLA_HEADS * KV_LORA), ((0, 0), (0, tpad - tvalid), (0, 0)))
    return o.reshape(bsz * tpad, MLA_HEADS * KV_LORA)


def _even_out_kernel(x_ref, ya_ref, olat_ref, sgb_ref, wbd_ref, woa_ref, wob_ref, gpost_ref, o_ref):
    yb = _dot(olat_ref[...], wbd_ref[...]) * sgb_ref[...]
    y = _dot(ya_ref[...], woa_ref[...]) + _dot(yb, wob_ref[...])
    o_ref[...] = x_ref[...] + _rms(y, gpost_ref[...])


def even_out(x, ya, olat, sgb, wts, tm):
    n = x.shape[0]
    row = lambda d: pl.BlockSpec((tm, d), lambda i: (i, 0))
    weights = (wts["wbd"], wts["woa"], wts["wob"], wts["gpost"])
    return pl.pallas_call(
        _even_out_kernel,
        grid=(n // tm,),
        in_specs=[row(D_MODEL), row(CONV_WIDTH), row(MLA_HEADS * KV_LORA), row(CONV_WIDTH)]
        + [_full(w.shape) for w in weights],
        out_specs=row(D_MODEL),
        out_shape=jax.ShapeDtypeStruct((n, D_MODEL), F32),
        compiler_params=_params("parallel"),
        name="even_out",
    )(x, ya, olat, sgb, *weights)


def _mem_kv_kernel(x_ref, g_ref, wk_ref, wv_ref, k_ref, v_ref, k4_ref, v4_ref):
    h = _rms(x_ref[...], g_ref[0]).astype(BF16)
    k = _dot(h, wk_ref[0])
    v = _dot(h, wv_ref[0])
    k_ref[0] = k
    v_ref[0] = v
    for hd in range(X_HEADS):
        cs = slice(hd * X_HEAD_DIM, (hd + 1) * X_HEAD_DIM)
        k4_ref[0, :, hd, :] = k[:, cs]
        v4_ref[0, :, hd, :] = v[:, cs]


def mem_kv(mem, g, wk, wv, tm):
    n = mem.shape[0]
    layers = g.shape[0]
    per_layer = lambda shape: pl.BlockSpec((1,) + shape, lambda l, i: (l,) + (0,) * len(shape))
    flat = pl.BlockSpec((1, tm, D_MODEL), lambda l, i: (l, i, 0))
    split = pl.BlockSpec((1, tm, X_HEADS, X_HEAD_DIM), lambda l, i: (l, i, 0, 0))
    return pl.pallas_call(
        _mem_kv_kernel,
        grid=(layers, n // tm),
        in_specs=[pl.BlockSpec((tm, D_MODEL), lambda l, i: (i, 0)), per_layer((1, D_MODEL)),
                  per_layer((D_MODEL, D_MODEL)), per_layer((D_MODEL, D_MODEL))],
        out_specs=(flat, flat, split, split),
        out_shape=(jax.ShapeDtypeStruct((layers, n, D_MODEL), F32),) * 2
        + (jax.ShapeDtypeStruct((layers, n, X_HEADS, X_HEAD_DIM), F32),) * 2,
        compiler_params=_params("parallel", "parallel"),
        name="mem_kv",
    )(mem, g, wk, wv)


def _xattn_body(x_ref, gpre_ref, wqg_ref, wo_ref, gpost_ref, o_ref, att_scr, mem_kv_of, bb, tq):
    x = x_ref[...].reshape(bb * tq, D_MODEL)
    qg = _dot(_rms(x, gpre_ref[...]), wqg_ref[...])
    q = qg[:, :D_MODEL] * (X_HEAD_DIM ** -0.5)
    for b in range(bb):
        for hd in range(X_HEADS):
            cs = slice(hd * X_HEAD_DIM, (hd + 1) * X_HEAD_DIM)
            mk, mv = mem_kv_of(b, hd)
            s = _dot_nt(q[b * tq:(b + 1) * tq, cs], mk)
            p = jnp.exp(s - jnp.max(s, axis=1, keepdims=True))
            o = _dot(p, mv) / jnp.sum(p, axis=1, keepdims=True)
            att_scr[b * tq:(b + 1) * tq, cs] = o
    y = _dot(att_scr[...] * _silu(qg[:, D_MODEL:]), wo_ref[...])
    o_ref[...] = (x + _rms(y, gpost_ref[...])).reshape(bb, tq, D_MODEL)


def _xattn_kernel(x_ref, mk_ref, mv_ref, gpre_ref, wqg_ref, wo_ref, gpost_ref, o_ref, att_scr, *, bb, tq):
    def mem_kv_of(b, hd):
        cs = slice(hd * X_HEAD_DIM, (hd + 1) * X_HEAD_DIM)
        return mk_ref[b, :, cs], mv_ref[b, :, cs]

    _xattn_body(x_ref, gpre_ref, wqg_ref, wo_ref, gpost_ref, o_ref, att_scr, mem_kv_of, bb, tq)


def xattn(x, mk, mv, layer, wts, bb, tq):
    bsz, t, _ = x.shape
    seq = pl.BlockSpec((bb, tq, D_MODEL), lambda b, i: (b, i, 0))
    mem = pl.BlockSpec((None, bb, MEM_LEN, D_MODEL), lambda b, i: (layer, b, 0, 0))
    weights = (wts["gprex"], wts["wqg"], wts["wo"], wts["gpostx"])
    return pl.pallas_call(
        functools.partial(_xattn_kernel, bb=bb, tq=tq),
        grid=(bsz // bb, t // tq),
        in_specs=[seq, mem, mem] + [_full(w.shape) for w in weights],
        out_specs=seq,
        out_shape=jax.ShapeDtypeStruct(x.shape, F32),
        scratch_shapes=[pltpu.VMEM((bb * tq, D_MODEL), F32)],
        compiler_params=_params("parallel", "arbitrary"),
        name="xattn",
    )(x, mk, mv, *weights)


def _xattn_cached_kernel(x_ref, mk_hbm, mv_hbm, gpre_ref, wqg_ref, wo_ref, gpost_ref, o_ref, att_scr,
                         k_buf, v_buf, sem, *, bb, tq, layer):
    g = pl.program_id(0)
    slot = g & 1

    def head_copies(gi, sl):
        copies = []
        for b in range(bb):
            for hd in range(X_HEADS):
                src = (layer, gi * bb + b, slice(None), hd, slice(None))
                copies.append(pltpu.make_async_copy(mk_hbm.at[src], k_buf.at[sl, b, hd], sem.at[sl]))
                copies.append(pltpu.make_async_copy(mv_hbm.at[src], v_buf.at[sl, b, hd], sem.at[sl]))
        return copies

    @pl.when(g == 0)
    def _():
        for cp in head_copies(g, slot):
            cp.start()

    @pl.when(g + 1 < pl.num_programs(0))
    def _():
        for cp in head_copies(g + 1, 1 - slot):
            cp.start()

    for cp in head_copies(g, slot):
        cp.wait()

    _xattn_body(x_ref, gpre_ref, wqg_ref, wo_ref, gpost_ref, o_ref, att_scr,
                lambda b, hd: (k_buf[slot, b, hd], v_buf[slot, b, hd]), bb, tq)


def xattn_cached(x, cache_k, cache_v, layer, wts, bb):
    bsz, tq, _ = x.shape
    seq = pl.BlockSpec((bb, tq, D_MODEL), lambda g: (g, 0, 0))
    hbm = pl.BlockSpec(memory_space=pl.ANY)
    weights = (wts["gprex"], wts["wqg"], wts["wo"], wts["gpostx"])
    buf = pltpu.VMEM((2, bb, X_HEADS, MEM_LEN, X_HEAD_DIM), F32)
    return pl.pallas_call(
        functools.partial(_xattn_cached_kernel, bb=bb, tq=tq, layer=layer),
        grid=(bsz // bb,),
        in_specs=[seq, hbm, hbm] + [_full(w.shape) for w in weights],
        out_specs=seq,
        out_shape=jax.ShapeDtypeStruct(x.shape, F32),
        scratch_shapes=[pltpu.VMEM((bb * tq, D_MODEL), F32), buf, buf, pltpu.SemaphoreType.DMA((2,))],
        compiler_params=_params("arbitrary"),
        name="xattn_cached",
    )(x, cache_k, cache_v, *weights)


def _odd_in_kernel(x_ref, gpre_ref, wz_ref, wxbc_ref, wdt_ref, wdtt_ref, brow_ref, bcol_ref,
                   sz_ref, xbc_ref, dt_ref, dtt_ref):
    h = _rms(x_ref[...], gpre_ref[...]).astype(BF16)
    sz_ref[...] = _silu(_dot(h, wz_ref[...])).astype(BF16)
    xbc_ref[...] = _dot(h, wxbc_ref[...])
    dt_ref[...] = jax.nn.softplus(_dot(h, wdt_ref[...]) + brow_ref[...])
    dtt_ref[...] = jax.nn.softplus(_dot_nt(wdtt_ref[...], h) + bcol_ref[...])


def odd_in(x, wts, tm):
    n = x.shape[0]
    row = lambda d: pl.BlockSpec((tm, d), lambda i: (i, 0))
    weights = (wts["gpre"], wts["wz"], wts["wxbc"], wts["wdt"], wts["wdtt"], wts["brow"], wts["bcol"])
    return pl.pallas_call(
        _odd_in_kernel,
        grid=(n // tm,),
        in_specs=[row(D_MODEL)] + [_full(w.shape) for w in weights],
        out_specs=(row(SSM_INNER), row(XBC_DIM), row(SSM_HEADS),
                   pl.BlockSpec((SSM_HEADS, tm), lambda i: (0, i))),
        out_shape=(jax.ShapeDtypeStruct((n, SSM_INNER), BF16), jax.ShapeDtypeStruct((n, XBC_DIM), F32),
                   jax.ShapeDtypeStruct((n, SSM_HEADS), F32), jax.ShapeDtypeStruct((SSM_HEADS, n), F32)),
        compiler_params=_params("parallel"),
        name="odd_in",
    )(x, *weights)


def _ssd_kernel(*refs, q, t_in, tvalid, has_init):
    if has_init:
        (xs_ref, bm_ref, cm_ref, dt_ref, dtt_ref, arow_ref, acol_ref, drow_ref, e64_ref, tri_ref,
         trit_ref, init_ref, y_ref, st_ref, state_scr) = refs
    else:
        (xs_ref, bm_ref, cm_ref, dt_ref, dtt_ref, arow_ref, acol_ref, drow_ref, e64_ref, tri_ref,
         trit_ref, y_ref, st_ref, state_scr) = refs
    c = pl.program_id(1)

    @pl.when(c == 0)
    def _():
        if has_init:
            state_scr[...] = init_ref[0].T
        else:
            state_scr[...] = jnp.zeros(state_scr.shape, F32)

    def rows(ref):
        v = ref[0]
        if t_in < q:
            v = jnp.concatenate([v, jnp.zeros((q - t_in, v.shape[1]), F32)], axis=0)
        return v

    xs, bm, cm = rows(xs_ref), rows(bm_ref), rows(cm_ref)
    dt = dt_ref[0]
    dtt = dtt_ref[...]
    if tvalid < q:
        dt = jnp.where(lax.broadcasted_iota(jnp.int32, dt.shape, 0) < tvalid, dt, 0.0)
        dtt = jnp.where(lax.broadcasted_iota(jnp.int32, dtt.shape, 1) < tvalid, dtt, 0.0)
    cum = _exact_dot_left(tri_ref[...], dt * arow_ref[...])
    cumt = _exact_dot(dtt * acol_ref[...], trit_ref[...])
    cum_last = cum[q - 1:q, :]
    causal = (lax.broadcasted_iota(jnp.int32, (t_in, q), 1) <= lax.broadcasted_iota(jnp.int32, (t_in, q), 0))
    lane = lax.broadcasted_iota(jnp.int32, (1, LANES), 1)
    left = lane < SSM_HEADDIM

    state = state_scr[...]
    y_parts = []
    for g in range(SSM_GROUPS):
        gs = slice(g * SSM_STATE, (g + 1) * SSM_STATE)
        cm_g = cm[:t_in, gs]
        cb = _dot_nt(cm_g, bm[:, gs])
        for pr in range(SSM_HPG // 2):
            pair = g * (SSM_HPG // 2) + pr
            ps = slice(pair * LANES, (pair + 1) * LANES)
            rhs = jnp.concatenate([xs[:, ps], state[:, ps]], axis=0)
            y_pair = None
            for side in range(2):
                hd = 2 * pair + side
                ccol = jnp.broadcast_to(cum[:t_in, hd:hd + 1], (t_in, q))
                seg = ccol - cumt[hd:hd + 1, :]
                w = cb * jnp.exp(jnp.where(causal, seg, -jnp.inf)) * dtt[hd:hd + 1, :]
                lhs = jnp.concatenate([w, cm_g * jnp.exp(ccol)], axis=1)
                keep = left if side == 0 else jnp.logical_not(left)
                part = _dot(lhs, jnp.where(keep, rhs, 0.0))
                y_pair = part if y_pair is None else y_pair + part
            y_parts.append(y_pair)
    y = jnp.concatenate(y_parts, axis=1) + drow_ref[...] * xs[:t_in]
    y_ref[0] = y.astype(y_ref.dtype)

    to_end = jnp.exp(cum_last - cum) * dt
    toxs = _dot(to_end, e64_ref[...]) * xs
    dec = _exact_dot(jnp.broadcast_to(jnp.exp(cum_last), (SUBLANES, SSM_HEADS)), e64_ref[...])[0:1, :]
    new_parts = []
    for g in range(SSM_GROUPS):
        gs = slice(g * SSM_STATE, (g + 1) * SSM_STATE)
        hs = slice(g * GROUP_WIDTH, (g + 1) * GROUP_WIDTH)
        new_parts.append(_dot(bm[:, gs].T, toxs[:, hs]))
    new_state = state * dec + jnp.concatenate(new_parts, axis=1)
    state_scr[...] = new_state

    @pl.when(c == pl.num_programs(1) - 1)
    def _():
        st_ref[0] = new_state.T


def ssd(xbc, dt, dtt, wts, init, q, t_in, tvalid, y_dtype):
    bsz, t, _ = xbc.shape
    nc = max(t // q, 1)
    nxb = SSM_INNER // (SSM_GROUPS * SSM_STATE)
    in_specs = [pl.BlockSpec((1, t_in, SSM_INNER), lambda b, c: (b, c, 0)),
                pl.BlockSpec((1, t_in, SSM_GROUPS * SSM_STATE), lambda b, c: (b, c, nxb)),
                pl.BlockSpec((1, t_in, SSM_GROUPS * SSM_STATE), lambda b, c: (b, c, nxb + 1)),
                pl.BlockSpec((1, q, SSM_HEADS), lambda b, c: (b, c, 0)),
                pl.BlockSpec((SSM_HEADS, q), lambda b, c: (0, b * nc + c))]
    consts = (wts["arow"], wts["acol"], wts["drow"], wts["e64"], wts["tri"], wts["trit"])
    in_specs += [_full(w.shape) for w in consts]
    args = [xbc, xbc, xbc, dt, dtt, *consts]
    if init is not None:
        in_specs.append(pl.BlockSpec((1, SSM_INNER, SSM_STATE), lambda b, c: (b, 0, 0)))
        args.append(init)
    return pl.pallas_call(
        functools.partial(_ssd_kernel, q=q, t_in=t_in, tvalid=tvalid, has_init=init is not None),
        grid=(bsz, nc),
        in_specs=in_specs,
        out_specs=(pl.BlockSpec((1, t_in, SSM_INNER), lambda b, c: (b, c, 0)),
                   pl.BlockSpec((1, SSM_INNER, SSM_STATE), lambda b, c: (b, 0, 0))),
        out_shape=(jax.ShapeDtypeStruct((bsz, t, SSM_INNER), y_dtype),
                   jax.ShapeDtypeStruct((bsz, SSM_INNER, SSM_STATE), F32)),
        scratch_shapes=[pltpu.VMEM((SSM_STATE, SSM_INNER), F32)],
        compiler_params=_params("parallel", "arbitrary"),
        name="ssd",
    )(*args)


def _odd_out_kernel(x_ref, y_ref, sz_ref, gn_ref, wout_ref, gpost_ref, o_ref):
    v = y_ref[...].astype(F32) * sz_ref[...].astype(F32)
    gn = gn_ref[...]
    parts = []
    for g in range(SSM_GROUPS):
        gs = slice(g * GROUP_WIDTH, (g + 1) * GROUP_WIDTH)
        parts.append(_rms(v[:, gs], gn[:, gs]).astype(BF16))
    y = _dot(jnp.concatenate(parts, axis=1), wout_ref[...])
    o_ref[...] = x_ref[...] + _rms(y, gpost_ref[...])


def odd_out(x, y, sz, wts, tm):
    n = x.shape[0]
    row = lambda d: pl.BlockSpec((tm, d), lambda i: (i, 0))
    weights = (wts["gnorm"], wts["wout"], wts["gpost"])
    return pl.pallas_call(
        _odd_out_kernel,
        grid=(n // tm,),
        in_specs=[row(D_MODEL), row(SSM_INNER), row(SSM_INNER)] + [_full(w.shape) for w in weights],
        out_specs=row(D_MODEL),
        out_shape=jax.ShapeDtypeStruct((n, D_MODEL), F32),
        compiler_params=_params("parallel"),
        name="odd_out",
    )(x, y, sz, *weights)


def _even_weights(i, l, p):
    w_in = p["e_w_in"][l]
    c = CONV_WIDTH
    o_q, o_kv, o_bg = 3 * c, 3 * c + Q_LORA, 3 * c + Q_LORA + KV_LORA + ROPE_DIM
    half = ROPE_DIM // 2
    w_kr = w_in[:, o_kv + KV_LORA:o_bg]
    w_uq = p["e_w_uq"][l]
    uq_r = w_uq[:, :, NOPE_DIM:]
    w_uv = jnp.transpose(p["e_w_uv"][l], (1, 0, 2))
    eye = jnp.eye(MLA_HEADS, dtype=F32)
    wbd = (eye[:, None, :, None] * w_uv[:, :, None, :]).reshape(MLA_HEADS * KV_LORA, MLA_HEADS * V_DIM)
    w_out = p["e_w_out"][l]
    return {
        "gpre": p["norm_pre_mix"][i][None], "gpost": p["norm_post_mix"][i][None],
        "wa": w_in[:, :o_q].astype(BF16), "wq": w_in[:, o_q:o_kv].astype(BF16),
        "wkv": jnp.concatenate([w_in[:, o_kv:o_bg], -w_kr[:, half:], w_kr[:, :half]], axis=1).astype(BF16),
        "wbg": w_in[:, o_bg:].astype(BF16),
        "gq": p["e_q_norm"][l][None], "gkv": p["e_kv_norm"][l][None],
        "wuqn": w_uq[:, :, :NOPE_DIM].reshape(Q_LORA, MLA_HEADS * NOPE_DIM).astype(BF16),
        "wuqr": uq_r.reshape(Q_LORA, MLA_HEADS * ROPE_DIM).astype(BF16),
        "wuqrot": jnp.concatenate([-uq_r[:, :, half:], uq_r[:, :, :half]], axis=2)
        .reshape(Q_LORA, MLA_HEADS * ROPE_DIM).astype(BF16),
        "wukt": jnp.transpose(p["e_w_uk"][l], (1, 2, 0)).astype(BF16),
        "conv_w": p["e_conv_w"][l], "conv_b": p["e_conv_b"][l][None],
        "ln_w": p["e_ln_w"][l][None], "ln_b": p["e_ln_b"][l][None],
        "wbd": wbd.astype(BF16), "woa": w_out[:c].astype(BF16), "wob": w_out[c:].astype(BF16),
    }


def _odd_weights(i, l, p):
    w_in = p["o_w_in"][l]
    w_dt = w_in[:, SSM_INNER + XBC_DIM:]
    a = -jnp.exp(p["o_a_log"][l])
    q = SSM_CHUNK
    tri = jnp.tril(jnp.ones((q, q), F32))
    return {
        "gpre": p["norm_pre_mix"][i][None], "gpost": p["norm_post_mix"][i][None],
        "wz": w_in[:, :SSM_INNER].astype(BF16),
        "wxbc": w_in[:, SSM_INNER:SSM_INNER + XBC_DIM].astype(BF16),
        "wdt": w_dt.astype(BF16), "wdtt": w_dt.T.astype(BF16),
        "brow": p["o_dt_bias"][l][None], "bcol": p["o_dt_bias"][l][:, None],
        "conv_w": p["o_conv_w"][l], "conv_b": p["o_conv_b"][l][None],
        "arow": a[None], "acol": a[:, None],
        "drow": jnp.repeat(p["o_d"][l], SSM_HEADDIM)[None],
        "e64": jnp.repeat(jnp.eye(SSM_HEADS, dtype=F32), SSM_HEADDIM, axis=1).astype(BF16),
        "tri": tri.astype(BF16), "trit": tri.T.astype(BF16),
        "gnorm": p["o_norm"][l][None], "wout": p["o_w_out"][l].astype(BF16),
    }


def _xattn_weights(i, p):
    return {"gprex": p["norm_pre_x"][i][None], "gpostx": p["norm_post_x"][i][None],
            "wqg": p["x_w_qg"][i].astype(BF16), "wo": p["x_w_o"][i].astype(BF16)}


def kernel(x_prompt, x_sample, mem_prompt, cache_ckv, cache_krope, page_table, state_conv_a,
           state_conv_c, state_ssm, cache_mem_k, cache_mem_v, norm_pre_mix, norm_post_mix,
           norm_pre_x, norm_post_x, norm_mem, x_w_qg, x_w_k, x_w_v, x_w_o, e_w_in, e_conv_w,
           e_conv_b, e_ln_w, e_ln_b, e_q_norm, e_kv_norm, e_w_uq, e_w_uk, e_w_uv, e_w_out,
           o_w_in, o_conv_w, o_conv_b, o_dt_bias, o_a_log, o_d, o_norm, o_w_out):
    p = dict(norm_pre_mix=norm_pre_mix, norm_post_mix=norm_post_mix, norm_pre_x=norm_pre_x,
             norm_post_x=norm_post_x, e_w_in=e_w_in, e_conv_w=e_conv_w, e_conv_b=e_conv_b,
             e_ln_w=e_ln_w, e_ln_b=e_ln_b, e_q_norm=e_q_norm, e_kv_norm=e_kv_norm, e_w_uq=e_w_uq,
             e_w_uk=e_w_uk, e_w_uv=e_w_uv, e_w_out=e_w_out, o_w_in=o_w_in, o_conv_w=o_conv_w,
             o_conv_b=o_conv_b, o_dt_bias=o_dt_bias, o_a_log=o_a_log, o_d=o_d, o_norm=o_norm,
             o_w_out=o_w_out, x_w_qg=x_w_qg, x_w_o=x_w_o)
    bp, sp, _ = x_prompt.shape
    bs, ss, _ = x_sample.shape
    depth = norm_pre_mix.shape[0]
    n_pages, page = page_table.shape[1], cache_ckv.shape[1]
    past_len = n_pages * page
    tpad = SUBLANES
    assert ss <= tpad and sp % SSM_CHUNK == 0
    np_, ns = bp * sp, bs * tpad
    tm_p, tm_s = 512, ns
    nb_s = 8 if bs % 8 == 0 else 1

    half = ROPE_DIM // 2
    inv = ROPE_THETA ** (-jnp.arange(half, dtype=F32) / half)
    inv_lanes = jnp.tile(inv, LANES // half)[None]
    cos_p, sin_p = rope_table(inv_lanes, sp, 0, sp)
    cos_s, sin_s = rope_table(inv_lanes, ns, past_len, tpad)

    xp = x_prompt.reshape(np_, D_MODEL)
    xs = jnp.pad(x_sample, ((0, 0), (0, tpad - ss), (0, 0))).reshape(ns, D_MODEL)
    mem_flat = mem_prompt.reshape(bp * MEM_LEN, D_MODEL)
    mk, mv, mk4, mv4 = mem_kv(mem_flat, norm_mem[:, None], x_w_k.astype(BF16), x_w_v.astype(BF16), 512)
    mem_k_p = mk.reshape(depth, bp, MEM_LEN, D_MODEL)
    mem_v_p = mv.reshape(depth, bp, MEM_LEN, D_MODEL)

    outs = {k: [] for k in ("ckv_p", "ckv_s", "kr_p", "kr_s", "ca_p", "ca_s", "cc_p", "cc_s",
                            "ssm_p", "ssm_s", "mk", "mv")}
    for i in range(depth):
        l = i // 2
        if i % 2 == 0:
            w = _even_weights(i, l, p)
            glu, sga, sgb, qcat, kcat, ckv, kr = even_in(xp, w, cos_p, sin_p, tm_p)
            glu3 = glu.reshape(bp, sp, CONV_WIDTH)
            ya = causal_conv(glu3, jnp.zeros((bp, CONV_A_HALO, CONV_WIDTH), F32), w["conv_w"], w["conv_b"],
                             512, CONV_A_HALO, CONV_WIDTH,
                             ln=(w["ln_w"], w["ln_b"], sga.reshape(bp, sp, CONV_WIDTH)), out_dtype=BF16)
            olat = mla_prompt(qcat, kcat.reshape(bp, sp, QK_CAT), bp, sp, 256)
            yp = even_out(xp, ya.reshape(np_, CONV_WIDTH), olat, sgb, w, tm_p)
            outs["ca_p"].append(glu3[:, sp - (CONV_K - 1):])
            outs["ckv_p"].append(ckv.reshape(bp, sp, KV_LORA))
            outs["kr_p"].append(kr.reshape(bp, sp, ROPE_DIM))
            glu, sga, sgb, qcat, kcat, ckv, kr = even_in(xs, w, cos_s, sin_s, tm_s)
            glu3 = glu.reshape(bs, tpad, CONV_WIDTH)
            hist = state_conv_a[l]
            hist_pad = jnp.pad(hist, ((0, 0), (CONV_A_HALO - (CONV_K - 1), 0), (0, 0)))
            ya = causal_conv(glu3, hist_pad, w["conv_w"], w["conv_b"], tpad, CONV_A_HALO, CONV_WIDTH,
                             ln=(w["ln_w"], w["ln_b"], sga.reshape(bs, tpad, CONV_WIDTH)), nb=nb_s)
            knew =jnp.pad(kcat.reshape(bs, tpad, QK_CAT), ((0, 0), (0, LANES - tpad), (0, 0)))
            olat = mla_sample(qcat, knew, cache_ckv[:, :, l, :],
                              jnp.transpose(cache_krope[:, :, l, :], (0, 2, 1)), page_table, tpad, ss,
                              min(64, n_pages))
            ys = even_out(xs, ya.reshape(ns, CONV_WIDTH), olat, sgb, w, tm_s)
            outs["ca_s"].append(jnp.concatenate([hist[:, ss:], glu3[:, :ss]], axis=1))
            outs["ckv_s"].append(ckv.reshape(bs, tpad, KV_LORA)[:, :ss])
            outs["kr_s"].append(kr.reshape(bs, tpad, ROPE_DIM)[:, :ss])
        else:
            w = _odd_weights(i, l, p)
            q = SSM_CHUNK
            sz, xbc, dt, dtt = odd_in(xp, w, 256)
            xbc3 = xbc.reshape(bp, sp, XBC_DIM)
            xbc_c = causal_conv(xbc3, jnp.zeros((bp, CONV_C_HALO, XBC_DIM), F32), w["conv_w"], w["conv_b"],
                                512, CONV_C_HALO, 512)
            y, st = ssd(xbc_c, dt.reshape(bp, sp, SSM_HEADS), dtt, w, None, q, q, q, BF16)
            yp = odd_out(xp, y.reshape(np_, SSM_INNER), sz, w, tm_p)
            outs["cc_p"].append(xbc3[:, sp - (SSM_CONV - 1):])
            outs["ssm_p"].append(st.reshape(bp, SSM_HEADS, SSM_HEADDIM, SSM_STATE))
            sz, xbc, dt, dtt = odd_in(xs, w, tm_s)
            xbc3 = xbc.reshape(bs, tpad, XBC_DIM)
            hist = state_conv_c[l]
            hist_pad = jnp.pad(hist, ((0, 0), (CONV_C_HALO - (SSM_CONV - 1), 0), (0, 0)))
            xbc_c = causal_conv(xbc3, hist_pad, w["conv_w"], w["conv_b"], tpad, CONV_C_HALO, 512, nb=nb_s)
            dt_pad = jnp.pad(dt.reshape(bs, tpad, SSM_HEADS), ((0, 0), (0, q - tpad), (0, 0)))
            dtt_pad = jnp.pad(dtt.reshape(SSM_HEADS, bs, tpad), ((0, 0), (0, 0), (0, q - tpad)))
            y, st = ssd(xbc_c, dt_pad, dtt_pad.reshape(SSM_HEADS, bs * q), w,
                        state_ssm[l].reshape(bs, SSM_INNER, SSM_STATE), q, tpad, ss, F32)
            ys = odd_out(xs, y.reshape(ns, SSM_INNER), sz, w, tm_s)
            outs["cc_s"].append(jnp.concatenate([hist[:, ss:], xbc3[:, :ss]], axis=1)[:, -(SSM_CONV - 1):])
            outs["ssm_s"].append(st.reshape(bs, SSM_HEADS, SSM_HEADDIM, SSM_STATE))
        xp, xs = yp, ys
        wx = _xattn_weights(i, p)
        xp = xattn(xp.reshape(bp, sp, D_MODEL), mem_k_p, mem_v_p, i, wx, 1, 512).reshape(np_, D_MODEL)
        xs = xattn_cached(xs.reshape(bs, tpad, D_MODEL), cache_mem_k, cache_mem_v, i, wx, 4).reshape(ns, D_MODEL)

    return (xp.reshape(bp, sp, D_MODEL), xs.reshape(bs, tpad, D_MODEL)[:, :ss],
            jnp.stack(outs["ckv_p"], axis=2), jnp.stack(outs["ckv_s"], axis=2),
            jnp.stack(outs["kr_p"], axis=2), jnp.stack(outs["kr_s"], axis=2),
            jnp.stack(outs["ca_p"]), jnp.stack(outs["ca_s"]),
            jnp.stack(outs["cc_p"]), jnp.stack(outs["cc_s"]),
            jnp.stack(outs["ssm_p"]), jnp.stack(outs["ssm_s"]),
            mk4.reshape(depth, bp, MEM_LEN, X_HEADS, X_HEAD_DIM),
            mv4.reshape(depth, bp, MEM_LEN, X_HEADS, X_HEAD_DIM))
```

```python
import functools

import jax
import jax.numpy as jnp
from jax import lax
from jax.experimental import pallas as pl
from jax.experimental.pallas import tpu as pltpu

F32 = jnp.float32
BF16 = jnp.bfloat16

D_MODEL = 1024
CONV_WIDTH = 512
CONV_K = 31
MLA_HEADS = 8
Q_LORA = 256
KV_LORA = 128
NOPE_DIM = 64
ROPE_DIM = 32
V_DIM = 64
QK_CAT = KV_LORA + ROPE_DIM
LOG2E = 1.4426950408889634
MLA_SCALE = (NOPE_DIM + ROPE_DIM) ** -0.5 * LOG2E
ROPE_THETA = 10000.0
SSM_INNER = 2048
SSM_HEADDIM = 64
SSM_HEADS = 32
SSM_GROUPS = 4
SSM_HPG = 8
SSM_STATE = 128
SSM_CONV = 4
SSM_CHUNK = 128
XBC_DIM = SSM_INNER + 2 * SSM_GROUPS * SSM_STATE
GROUP_WIDTH = SSM_INNER // SSM_GROUPS
MEM_LEN = 256
X_HEADS = 4
X_HEAD_DIM = 256
EPS = 1e-6

LANES = 128
SUBLANES = 8
CONV_A_HALO = 32
CONV_C_HALO = 8
VMEM_LIMIT = 56 * 1024 * 1024


def _params(*sem):
    return pltpu.CompilerParams(dimension_semantics=sem, vmem_limit_bytes=VMEM_LIMIT)


def _rms(x, w):
    return x * lax.rsqrt(jnp.mean(x * x, axis=-1, keepdims=True) + EPS) * w


def _silu(x):
    return x * jax.nn.sigmoid(x)


def _dot(a, b):
    return jnp.dot(a.astype(BF16), b.astype(BF16), preferred_element_type=F32)


def _dot_nt(a, b):
    return lax.dot_general(a.astype(BF16), b.astype(BF16), (((1,), (1,)), ((), ())),
                           preferred_element_type=F32)


def _split3(x):
    hi = x.astype(BF16)
    r = x - hi.astype(F32)
    mid = r.astype(BF16)
    lo = (r - mid.astype(F32)).astype(BF16)
    return hi, mid, lo


def _exact_dot(x, e):
    hi, mid, lo = _split3(x)
    return (jnp.dot(hi, e, preferred_element_type=F32) + jnp.dot(mid, e, preferred_element_type=F32)
            + jnp.dot(lo, e, preferred_element_type=F32))


def _exact_dot_left(e, x):
    hi, mid, lo = _split3(x)
    return (jnp.dot(e, hi, preferred_element_type=F32) + jnp.dot(e, mid, preferred_element_type=F32)
            + jnp.dot(e, lo, preferred_element_type=F32))


def _full(shape):
    nd = len(shape)
    return pl.BlockSpec(shape, lambda *_: (0,) * nd)


def _rope_table_kernel(inv_ref, cos_ref, sin_ref, *, base, period):
    row = lax.broadcasted_iota(jnp.int32, cos_ref.shape, 0)
    pos = (base + (row & (period - 1))).astype(F32)
    ang = pos * inv_ref[...]
    cos_ref[...] = jnp.cos(ang)
    sin_ref[...] = jnp.sin(ang)


def rope_table(inv_lanes, rows, base, period):
    assert period & (period - 1) == 0
    return pl.pallas_call(
        functools.partial(_rope_table_kernel, base=base, period=period),
        out_shape=(jax.ShapeDtypeStruct((rows, LANES), F32),) * 2,
        name="rope_table",
    )(inv_lanes)


def _even_in_kernel(x_ref, gpre_ref, wa_ref, wq_ref, wkv_ref, wbg_ref, gq_ref, gkv_ref,
                    wuqn_ref, wuqr_ref, wuqrot_ref, wukt_ref, cos_ref, sin_ref,
                    glu_ref, sga_ref, sgb_ref, qcat_ref, kcat_ref, ckv_ref, kr_ref):
    h = _rms(x_ref[...], gpre_ref[...]).astype(BF16)
    ua = _dot(h, wa_ref[...])
    glu_ref[...] = ua[:, :CONV_WIDTH] * jax.nn.sigmoid(ua[:, CONV_WIDTH:2 * CONV_WIDTH])
    sga_ref[...] = _silu(ua[:, 2 * CONV_WIDTH:])
    sgb_ref[...] = _silu(_dot(h, wbg_ref[...])).astype(BF16)
    cos = cos_ref[...]
    sin = sin_ref[...]
    kv = _dot(h, wkv_ref[...])
    ckv = _rms(kv[:, :KV_LORA], gkv_ref[...])
    kr = (kv[:, KV_LORA:KV_LORA + ROPE_DIM] * cos[:, :ROPE_DIM]
          + kv[:, KV_LORA + ROPE_DIM:] * sin[:, :ROPE_DIM])
    ckv_ref[...] = ckv
    kr_ref[...] = kr
    kcat_ref[:, :KV_LORA] = ckv.astype(BF16)
    kcat_ref[:, KV_LORA:] = kr.astype(BF16)
    qn = _rms(_dot(h, wq_ref[...]), gq_ref[...]).astype(BF16)
    qnope = _dot(qn, wuqn_ref[...])
    cos2 = jnp.concatenate([cos, cos], axis=1)
    sin2 = jnp.concatenate([sin, sin], axis=1)
    qr = (_dot(qn, wuqr_ref[...]) * cos2 + _dot(qn, wuqrot_ref[...]) * sin2) * MLA_SCALE
    for hd in range(MLA_HEADS):
        ql = _dot(qnope[:, hd * NOPE_DIM:(hd + 1) * NOPE_DIM], wukt_ref[hd]) * MLA_SCALE
        qcat_ref[hd, :, :KV_LORA] = ql.astype(BF16)
        qcat_ref[hd, :, KV_LORA:] = qr[:, hd * ROPE_DIM:(hd + 1) * ROPE_DIM].astype(BF16)


def even_in(x, wts, cos, sin, tm):
    n = x.shape[0]
    nblk = cos.shape[0] // tm
    row = lambda d: pl.BlockSpec((tm, d), lambda i: (i, 0))
    tab = pl.BlockSpec((tm, LANES), lambda i: (i % nblk, 0))
    weights = (wts["gpre"], wts["wa"], wts["wq"], wts["wkv"], wts["wbg"], wts["gq"], wts["gkv"],
               wts["wuqn"], wts["wuqr"], wts["wuqrot"], wts["wukt"])
    return pl.pallas_call(
        _even_in_kernel,
        grid=(n // tm,),
        in_specs=[row(D_MODEL)] + [_full(w.shape) for w in weights] + [tab, tab],
        out_specs=(row(CONV_WIDTH), row(CONV_WIDTH), row(CONV_WIDTH),
                   pl.BlockSpec((MLA_HEADS, tm, QK_CAT), lambda i: (0, i, 0)),
                   row(QK_CAT), row(KV_LORA), row(ROPE_DIM)),
        out_shape=(jax.ShapeDtypeStruct((n, CONV_WIDTH), F32),) * 2 + (
            jax.ShapeDtypeStruct((n, CONV_WIDTH), BF16),
            jax.ShapeDtypeStruct((MLA_HEADS, n, QK_CAT), BF16),
            jax.ShapeDtypeStruct((n, QK_CAT), BF16),
            jax.ShapeDtypeStruct((n, KV_LORA), F32),
            jax.ShapeDtypeStruct((n, ROPE_DIM), F32)),
        compiler_params=_params("parallel"),
        name="even_in",
    )(x, *weights, cos, sin)


def _conv_kernel(hist_ref, prev_ref, cur_ref, w_ref, b_ref, *rest, taps, halo, row_chunk, lane_chunk,
                 layernorm):
    if layernorm:
        lnw_ref, lnb_ref, gate_ref, out_ref, ext_scr, *shift_scr = rest
    else:
        out_ref, ext_scr, *shift_scr = rest
    nb, tt, width = cur_ref.shape
    off = halo - (taps - 1)

    def ext_rows(d, r0, cs):
        a, r = divmod(d, SUBLANES)
        if not shift_scr or r == 0:
            return ext_scr[d + r0:d + r0 + row_chunk, cs]
        return shift_scr[0][r - 1, SUBLANES * a + r0:SUBLANES * a + r0 + row_chunk, cs]

    for sq in range(nb):
        ext_scr[0:halo] = jnp.where(pl.program_id(1) == 0, hist_ref[sq], prev_ref[sq])
        ext_scr[halo:halo + tt] = cur_ref[sq]
        if shift_scr:
            sh_scr, = shift_scr
            span = sh_scr.shape[1]
            ext = ext_scr[...]
            for r in range(1, SUBLANES):
                sh_scr[r - 1] = pltpu.roll(ext, halo + tt - r, axis=0)[:span]

        for r0 in range(0, tt, row_chunk):
            for c0 in range(0, width, lane_chunk):
                cs = slice(c0, c0 + lane_chunk)
                if shift_scr:
                    taps_in = [ext_rows(off + k, r0, cs) for k in range(taps)]
                else:
                    win = ext_scr[r0 + halo - SUBLANES:r0 + halo + row_chunk, cs]
                    taps_in = [(pltpu.roll(win, taps - 1 - k, axis=0) if k < taps - 1 else win)[SUBLANES:]
                               for k in range(taps)]
                acc = w_ref[0:1, cs] * taps_in[0]
                for k in range(1, taps):
                    acc = acc + w_ref[k:k + 1, cs] * taps_in[k]
                acc = acc + b_ref[:, cs]
                if layernorm:
                    xc = acc - jnp.mean(acc, axis=-1, keepdims=True)
                    var = jnp.mean(xc * xc, axis=-1, keepdims=True)
                    ln = xc * lax.rsqrt(var + EPS) * lnw_ref[...] + lnb_ref[...]
                    out_ref[sq, r0:r0 + row_chunk, :] = (_silu(ln) * gate_ref[sq, r0:r0 + row_chunk, :]
                                                         ).astype(out_ref.dtype)
                else:
                    out_ref[sq, r0:r0 + row_chunk, cs] = _silu(acc)


def causal_conv(x, hist, w, b, tt, halo, lane_chunk, ln=None, out_dtype=F32, nb=1):
    bsz, t, width = x.shape
    taps = w.shape[0]
    seq = pl.BlockSpec((nb, tt, width), lambda bi, ti: (bi, ti, 0))
    first = pl.BlockSpec((nb, halo, width), lambda bi, ti: (bi, 0, 0))
    if t == tt:
        prev, prev_spec = hist, first
    else:
        ratio = tt // halo
        prev = x
        prev_spec = pl.BlockSpec((nb, halo, width), lambda bi, ti: (bi, jnp.maximum(ti * ratio - 1, 0), 0))
    in_specs = [first, prev_spec, seq, _full(w.shape), _full(b.shape)]
    args = [hist, prev, x, w, b]
    if ln is not None:
        in_specs += [_full(ln[0].shape), _full(ln[1].shape), seq]
        args += list(ln)
    return pl.pallas_call(
        functools.partial(_conv_kernel, taps=taps, halo=halo, row_chunk=min(32, tt),
                          lane_chunk=lane_chunk, layernorm=ln is not None),
        grid=(bsz // nb, t // tt),
        in_specs=in_specs,
        out_specs=seq,
        out_shape=jax.ShapeDtypeStruct((bsz, t, width), out_dtype),
        scratch_shapes=[pltpu.VMEM((halo + tt, width), F32)]
        + ([pltpu.VMEM((SUBLANES - 1, halo + tt - SUBLANES, width), F32)] if taps > SUBLANES else []),
        compiler_params=_params("parallel", "arbitrary"),
        name="causal_conv_ln" if ln is not None else "causal_conv",
    )(*args)


def _mla_prompt_kernel(q_ref, k_ref, o_ref, s_scr, acc_scr, mx_scr, sum_scr, *, tq):
    i = pl.program_id(1)
    rows = MLA_HEADS * tq
    q = q_ref[...].reshape(rows, QK_CAT)

    def keys(j):
        return k_ref[0, pl.ds(pl.multiple_of(j * tq, tq), tq), :]

    def fold(x):
        parts = [x[:, c:c + LANES] for c in range(0, tq, LANES)]
        return parts

    mx_scr[...] = jnp.full(mx_scr.shape, -jnp.inf, F32)

    def pass1(j, carry):
        s = _dot_nt(q, keys(j))
        s_scr[j] = s
        mx_scr[...] = functools.reduce(jnp.maximum, fold(s), mx_scr[...])
        return carry

    lax.fori_loop(0, i, pass1, 0)
    s = _dot_nt(q, keys(i))
    qpos = lax.broadcasted_iota(jnp.int32, s.shape, 0) & (tq - 1)
    kpos = lax.broadcasted_iota(jnp.int32, s.shape, 1)
    s = jnp.where(kpos <= qpos, s, -jnp.inf)
    s_scr[i] = s
    m = jnp.max(functools.reduce(jnp.maximum, fold(s), mx_scr[...]), axis=1, keepdims=True)

    acc_scr[...] = jnp.zeros(acc_scr.shape, F32)
    sum_scr[...] = jnp.zeros(sum_scr.shape, F32)

    def pass2(j, carry):
        p = jnp.exp2(s_scr[j] - m)
        acc_scr[...] += _dot(p, keys(j)[:, :KV_LORA])
        sum_scr[...] = functools.reduce(jnp.add, fold(p), sum_scr[...])
        return carry

    lax.fori_loop(0, i + 1, pass2, 0)
    o = acc_scr[...] / jnp.sum(sum_scr[...], axis=1, keepdims=True)
    for hd in range(MLA_HEADS):
        o_ref[:, hd * KV_LORA:(hd + 1) * KV_LORA] = o[hd * tq:(hd + 1) * tq, :].astype(BF16)


def mla_prompt(qcat, kcat, bsz, t, tq):
    nq = t // tq
    rows = MLA_HEADS * tq
    return pl.pallas_call(
        functools.partial(_mla_prompt_kernel, tq=tq),
        grid=(bsz, nq),
        in_specs=[pl.BlockSpec((MLA_HEADS, tq, QK_CAT), lambda b, i: (0, b * nq + i, 0)),
                  pl.BlockSpec((1, t, QK_CAT), lambda b, i: (b, 0, 0))],
        out_specs=pl.BlockSpec((tq, MLA_HEADS * KV_LORA), lambda b, i: (b * nq + i, 0)),
        out_shape=jax.ShapeDtypeStruct((bsz * t, MLA_HEADS * KV_LORA), BF16),
        scratch_shapes=[pltpu.VMEM((nq, rows, tq), F32), pltpu.VMEM((rows, KV_LORA), F32),
                        pltpu.VMEM((rows, LANES), F32), pltpu.VMEM((rows, LANES), F32)],
        compiler_params=_params("parallel", "arbitrary"),
        name="mla_prompt",
    )(qcat, kcat)


def _mla_sample_kernel(pt_ref, q_ref, knew_ref, ckv_hbm, krt_hbm, o_ref, ckv_buf, krt_buf, sem,
                       kc_scr, krt_scr, m_scr, l_scr, acc_scr, *, pages, page, tvalid):
    b = pl.program_id(0)
    j = pl.program_id(1)
    steps = pl.num_programs(1)
    g = b * steps + j
    slot = g & 1

    def page_copies(bi, ji, sl):
        copies = []
        for i in range(pages):
            pg = pt_ref[bi, ji * pages + i]
            copies.append(pltpu.make_async_copy(ckv_hbm.at[pg], ckv_buf.at[sl, pl.ds(i * page, page)],
                                                sem.at[sl]))
            copies.append(pltpu.make_async_copy(krt_hbm.at[pg], krt_buf.at[sl, i], sem.at[sl]))
        return copies

    def start_all(copies):
        for n, cp in enumerate(copies):
            cp.start(priority=(n // 2) % 2)

    @pl.when(g == 0)
    def _():
        start_all(page_copies(b, j, slot))

    @pl.when(g + 1 < pl.num_programs(0) * steps)
    def _():
        wrap = j + 1 == steps
        start_all(page_copies(jnp.where(wrap, b + 1, b), jnp.where(wrap, 0, j + 1), 1 - slot))

    for cp in page_copies(b, j, slot):
        cp.wait()

    @pl.when(j == 0)
    def _():
        m_scr[...] = jnp.full(m_scr.shape, -jnp.inf, F32)
        l_scr[...] = jnp.zeros(l_scr.shape, F32)
        acc_scr[...] = jnp.zeros(acc_scr.shape, F32)

    q = q_ref[0]

    def update(s, v):
        m_prev = m_scr[...]
        m_new = jnp.maximum(m_prev, jnp.max(s, axis=1, keepdims=True))
        alpha = jnp.exp2(m_prev - m_new)
        p = jnp.exp2(s - m_new[:, 0:1])
        l_scr[...] = alpha * l_scr[...] + jnp.sum(p, axis=1, keepdims=True)
        m_scr[...] = m_new
        acc_scr[...] = alpha * acc_scr[...] + _dot(p, v)

    kc_scr[...] = ckv_buf[slot].astype(BF16)
    for i in range(pages):
        krt_scr[:, i * page:(i + 1) * page] = krt_buf[slot, i].astype(BF16)
    kc = kc_scr[...]
    update(_dot_nt(q[:, :KV_LORA], kc) + _dot(q[:, KV_LORA:], krt_scr[...]), kc)

    @pl.when(j == pl.num_programs(1) - 1)
    def _():
        knew = knew_ref[0]
        s = _dot_nt(q, knew)
        qpos = lax.rem(lax.broadcasted_iota(jnp.int32, s.shape, 0), tvalid)
        kpos = lax.broadcasted_iota(jnp.int32, s.shape, 1)
        s = jnp.where((kpos <= qpos) & (kpos < tvalid), s, -jnp.inf)
        update(s, knew[:, :KV_LORA])
        o_ref[0] = acc_scr[...] / l_scr[...]


def mla_sample(qcat, knew, ckv_pages, krt_pages, page_table, tpad, tvalid, pages):
    bsz, n_pages = page_table.shape
    page = ckv_pages.shape[1]
    steps = n_pages // pages
    rows = MLA_HEADS * tvalid
    q = qcat.astype(F32).reshape(MLA_HEADS, bsz, tpad, QK_CAT)[:, :, :tvalid]
    q = jnp.transpose(q, (1, 0, 2, 3)).reshape(bsz, rows, QK_CAT)

    grid_spec = pltpu.PrefetchScalarGridSpec(
        num_scalar_prefetch=1,
        grid=(bsz, steps),
        in_specs=[pl.BlockSpec((1, rows, QK_CAT), lambda b, j, pt: (b, 0, 0)),
                  pl.BlockSpec((1, knew.shape[1], QK_CAT), lambda b, j, pt: (b, 0, 0)),
                  pl.BlockSpec(memory_space=pl.ANY), pl.BlockSpec(memory_space=pl.ANY)],
        out_specs=pl.BlockSpec((1, rows, KV_LORA), lambda b, j, pt: (b, 0, 0)),
        scratch_shapes=[pltpu.VMEM((2, pages * page, KV_LORA), F32),
                        pltpu.VMEM((2, pages, ROPE_DIM, page), F32),
                        pltpu.SemaphoreType.DMA((2,)),
                        pltpu.VMEM((pages * page, KV_LORA), BF16),
                        pltpu.VMEM((ROPE_DIM, pages * page), BF16),
                        pltpu.VMEM((rows, LANES), F32), pltpu.VMEM((rows, LANES), F32),
                        pltpu.VMEM((rows, KV_LORA), F32)])
    o = pl.pallas_call(
        functools.partial(_mla_sample_kernel, pages=pages, page=page, tvalid=tvalid),
        grid_spec=grid_spec,
        out_shape=jax.ShapeDtypeStruct((bsz, rows, KV_LORA), F32),
        compiler_params=_params("arbitrary", "arbitrary"),
        name="mla_sample",
    )(page_table, q, knew, ckv_pages, krt_pages)
    o = jnp.transpose(o.reshape(bsz, MLA_HEADS, tvalid, KV_LORA), (0, 2, 1, 3))
    o = jnp.pad(o.reshape(bsz, tvalid, MLA_HEADS * KV_LORA), ((0, 0), (0, tpad - tvalid), (0, 0)))
    return o.reshape(bsz * tpad, MLA_HEADS * KV_LORA)


def _even_out_kernel(x_ref, ya_ref, olat_ref, sgb_ref, wbd_ref, woa_ref, wob_ref, gpost_ref, o_ref):
    yb = _dot(olat_ref[...], wbd_ref[...]) * sgb_ref[...]
    y = _dot(ya_ref[...], woa_ref[...]) + _dot(yb, wob_ref[...])
    o_ref[...] = x_ref[...] + _rms(y, gpost_ref[...])


def even_out(x, ya, olat, sgb, wts, tm):
    n = x.shape[0]
    row = lambda d: pl.BlockSpec((tm, d), lambda i: (i, 0))
    weights = (wts["wbd"], wts["woa"], wts["wob"], wts["gpost"])
    return pl.pallas_call(
        _even_out_kernel,
        grid=(n // tm,),
        in_specs=[row(D_MODEL), row(CONV_WIDTH), row(MLA_HEADS * KV_LORA), row(CONV_WIDTH)]
        + [_full(w.shape) for w in weights],
        out_specs=row(D_MODEL),
        out_shape=jax.ShapeDtypeStruct((n, D_MODEL), F32),
        compiler_params=_params("parallel"),
        name="even_out",
    )(x, ya, olat, sgb, *weights)


def _mem_kv_kernel(x_ref, g_ref, wk_ref, wv_ref, k_ref, v_ref, k4_ref, v4_ref):
    h = _rms(x_ref[...], g_ref[0]).astype(BF16)
    k = _dot(h, wk_ref[0])
    v = _dot(h, wv_ref[0])
    k_ref[0] = k
    v_ref[0] = v
    for hd in range(X_HEADS):
        cs = slice(hd * X_HEAD_DIM, (hd + 1) * X_HEAD_DIM)
        k4_ref[0, :, hd, :] = k[:, cs]
        v4_ref[0, :, hd, :] = v[:, cs]


def mem_kv(mem, g, wk, wv, tm):
    n = mem.shape[0]
    layers = g.shape[0]
    per_layer = lambda shape: pl.BlockSpec((1,) + shape, lambda l, i: (l,) + (0,) * len(shape))
    flat = pl.BlockSpec((1, tm, D_MODEL), lambda l, i: (l, i, 0))
    split = pl.BlockSpec((1, tm, X_HEADS, X_HEAD_DIM), lambda l, i: (l, i, 0, 0))
    return pl.pallas_call(
        _mem_kv_kernel,
        grid=(layers, n // tm),
        in_specs=[pl.BlockSpec((tm, D_MODEL), lambda l, i: (i, 0)), per_layer((1, D_MODEL)),
                  per_layer((D_MODEL, D_MODEL)), per_layer((D_MODEL, D_MODEL))],
        out_specs=(flat, flat, split, split),
        out_shape=(jax.ShapeDtypeStruct((layers, n, D_MODEL), F32),) * 2
        + (jax.ShapeDtypeStruct((layers, n, X_HEADS, X_HEAD_DIM), F32),) * 2,
        compiler_params=_params("parallel", "parallel"),
        name="mem_kv",
    )(mem, g, wk, wv)


def _xattn_body(x_ref, gpre_ref, wqg_ref, wo_ref, gpost_ref, o_ref, att_scr, mem_kv_of, bb, tq):
    x = x_ref[...].reshape(bb * tq, D_MODEL)
    qg = _dot(_rms(x, gpre_ref[...]), wqg_ref[...])
    q = qg[:, :D_MODEL] * (X_HEAD_DIM ** -0.5)
    for b in range(bb):
        for hd in range(X_HEADS):
            cs = slice(hd * X_HEAD_DIM, (hd + 1) * X_HEAD_DIM)
            mk, mv = mem_kv_of(b, hd)
            s = _dot_nt(q[b * tq:(b + 1) * tq, cs], mk)
            p = jnp.exp(s - jnp.max(s, axis=1, keepdims=True))
            o = _dot(p, mv) / jnp.sum(p, axis=1, keepdims=True)
            att_scr[b * tq:(b + 1) * tq, cs] = o
    y = _dot(att_scr[...] * _silu(qg[:, D_MODEL:]), wo_ref[...])
    o_ref[...] = (x + _rms(y, gpost_ref[...])).reshape(bb, tq, D_MODEL)


def _xattn_kernel(x_ref, mk_ref, mv_ref, gpre_ref, wqg_ref, wo_ref, gpost_ref, o_ref, att_scr, *, bb, tq):
    def mem_kv_of(b, hd):
        cs = slice(hd * X_HEAD_DIM, (hd + 1) * X_HEAD_DIM)
        return mk_ref[b, :, cs], mv_ref[b, :, cs]

    _xattn_body(x_ref, gpre_ref, wqg_ref, wo_ref, gpost_ref, o_ref, att_scr, mem_kv_of, bb, tq)


def xattn(x, mk, mv, layer, wts, bb, tq):
    bsz, t, _ = x.shape
    seq = pl.BlockSpec((bb, tq, D_MODEL), lambda b, i: (b, i, 0))
    mem = pl.BlockSpec((None, bb, MEM_LEN, D_MODEL), lambda b, i: (layer, b, 0, 0))
    weights = (wts["gprex"], wts["wqg"], wts["wo"], wts["gpostx"])
    return pl.pallas_call(
        functools.partial(_xattn_kernel, bb=bb, tq=tq),
        grid=(bsz // bb, t // tq),
        in_specs=[seq, mem, mem] + [_full(w.shape) for w in weights],
        out_specs=seq,
        out_shape=jax.ShapeDtypeStruct(x.shape, F32),
        scratch_shapes=[pltpu.VMEM((bb * tq, D_MODEL), F32)],
        compiler_params=_params("parallel", "arbitrary"),
        name="xattn",
    )(x, mk, mv, *weights)


def _xattn_cached_kernel(x_ref, mk_hbm, mv_hbm, gpre_ref, wqg_ref, wo_ref, gpost_ref, o_ref, att_scr,
                         k_buf, v_buf, sem, *, bb, tq, layer):
    g = pl.program_id(0)
    slot = g & 1

    def head_copies(gi, sl):
        copies = []
        for b in range(bb):
            for hd in range(X_HEADS):
                src = (layer, gi * bb + b, slice(None), hd, slice(None))
                copies.append(pltpu.make_async_copy(mk_hbm.at[src], k_buf.at[sl, b, hd], sem.at[sl]))
                copies.append(pltpu.make_async_copy(mv_hbm.at[src], v_buf.at[sl, b, hd], sem.at[sl]))
        return copies

    @pl.when(g == 0)
    def _():
        for cp in head_copies(g, slot):
            cp.start()

    @pl.when(g + 1 < pl.num_programs(0))
    def _():
        for cp in head_copies(g + 1, 1 - slot):
            cp.start()

    for cp in head_copies(g, slot):
        cp.wait()

    _xattn_body(x_ref, gpre_ref, wqg_ref, wo_ref, gpost_ref, o_ref, att_scr,
                lambda b, hd: (k_buf[slot, b, hd], v_buf[slot, b, hd]), bb, tq)


def xattn_cached(x, cache_k, cache_v, layer, wts, bb):
    bsz, tq, _ = x.shape
    seq = pl.BlockSpec((bb, tq, D_MODEL), lambda g: (g, 0, 0))
    hbm = pl.BlockSpec(memory_space=pl.ANY)
    weights = (wts["gprex"], wts["wqg"], wts["wo"], wts["gpostx"])
    buf = pltpu.VMEM((2, bb, X_HEADS, MEM_LEN, X_HEAD_DIM), F32)
    return pl.pallas_call(
        functools.partial(_xattn_cached_kernel, bb=bb, tq=tq, layer=layer),
        grid=(bsz // bb,),
        in_specs=[seq, hbm, hbm] + [_full(w.shape) for w in weights],
        out_specs=seq,
        out_shape=jax.ShapeDtypeStruct(x.shape, F32),
        scratch_shapes=[pltpu.VMEM((bb * tq, D_MODEL), F32), buf, buf, pltpu.SemaphoreType.DMA((2,))],
        compiler_params=_params("arbitrary"),
        name="xattn_cached",
    )(x, cache_k, cache_v, *weights)


def _odd_in_kernel(x_ref, gpre_ref, wz_ref, wxbc_ref, wdt_ref, wdtt_ref, brow_ref, bcol_ref,
                   sz_ref, xbc_ref, dt_ref, dtt_ref):
    h = _rms(x_ref[...], gpre_ref[...]).astype(BF16)
    sz_ref[...] = _silu(_dot(h, wz_ref[...])).astype(BF16)
    xbc_ref[...] = _dot(h, wxbc_ref[...])
    dt_ref[...] = jax.nn.softplus(_dot(h, wdt_ref[...]) + brow_ref[...])
    dtt_ref[...] = jax.nn.softplus(_dot_nt(wdtt_ref[...], h) + bcol_ref[...])


def odd_in(x, wts, tm):
    n = x.shape[0]
    row = lambda d: pl.BlockSpec((tm, d), lambda i: (i, 0))
    weights = (wts["gpre"], wts["wz"], wts["wxbc"], wts["wdt"], wts["wdtt"], wts["brow"], wts["bcol"])
    return pl.pallas_call(
        _odd_in_kernel,
        grid=(n // tm,),
        in_specs=[row(D_MODEL)] + [_full(w.shape) for w in weights],
        out_specs=(row(SSM_INNER), row(XBC_DIM), row(SSM_HEADS),
                   pl.BlockSpec((SSM_HEADS, tm), lambda i: (0, i))),
        out_shape=(jax.ShapeDtypeStruct((n, SSM_INNER), BF16), jax.ShapeDtypeStruct((n, XBC_DIM), F32),
                   jax.ShapeDtypeStruct((n, SSM_HEADS), F32), jax.ShapeDtypeStruct((SSM_HEADS, n), F32)),
        compiler_params=_params("parallel"),
        name="odd_in",
    )(x, *weights)


def _ssd_kernel(*refs, q, t_in, tvalid, has_init):
    if has_init:
        (xs_ref, bm_ref, cm_ref, dt_ref, dtt_ref, arow_ref, acol_ref, drow_ref, e64_ref, tri_ref,
         trit_ref, init_ref, y_ref, st_ref, state_scr) = refs
    else:
        (xs_ref, bm_ref, cm_ref, dt_ref, dtt_ref, arow_ref, acol_ref, drow_ref, e64_ref, tri_ref,
         trit_ref, y_ref, st_ref, state_scr) = refs
    c = pl.program_id(1)

    @pl.when(c == 0)
    def _():
        if has_init:
            state_scr[...] = init_ref[0].T
        else:
            state_scr[...] = jnp.zeros(state_scr.shape, F32)

    def rows(ref):
        v = ref[0]
        if t_in < q:
            v = jnp.concatenate([v, jnp.zeros((q - t_in, v.shape[1]), F32)], axis=0)
        return v

    xs, bm, cm = rows(xs_ref), rows(bm_ref), rows(cm_ref)
    dt = dt_ref[0]
    dtt = dtt_ref[...]
    if tvalid < q:
        dt = jnp.where(lax.broadcasted_iota(jnp.int32, dt.shape, 0) < tvalid, dt, 0.0)
        dtt = jnp.where(lax.broadcasted_iota(jnp.int32, dtt.shape, 1) < tvalid, dtt, 0.0)
    cum = _exact_dot_left(tri_ref[...], dt * arow_ref[...])
    cumt = _exact_dot(dtt * acol_ref[...], trit_ref[...])
    cum_last = cum[q - 1:q, :]
    causal = (lax.broadcasted_iota(jnp.int32, (t_in, q), 1) <= lax.broadcasted_iota(jnp.int32, (t_in, q), 0))
    lane = lax.broadcasted_iota(jnp.int32, (1, LANES), 1)
    left = lane < SSM_HEADDIM

    state = state_scr[...]
    y_parts = []
    for g in range(SSM_GROUPS):
        gs = slice(g * SSM_STATE, (g + 1) * SSM_STATE)
        cm_g = cm[:t_in, gs]
        cb = _dot_nt(cm_g, bm[:, gs])
        for pr in range(SSM_HPG // 2):
            pair = g * (SSM_HPG // 2) + pr
            ps = slice(pair * LANES, (pair + 1) * LANES)
            rhs = jnp.concatenate([xs[:, ps], state[:, ps]], axis=0)
            y_pair = None
            for side in range(2):
                hd = 2 * pair + side
                ccol = jnp.broadcast_to(cum[:t_in, hd:hd + 1], (t_in, q))
                seg = ccol - cumt[hd:hd + 1, :]
                w = cb * jnp.exp(jnp.where(causal, seg, -jnp.inf)) * dtt[hd:hd + 1, :]
                lhs = jnp.concatenate([w, cm_g * jnp.exp(ccol)], axis=1)
                keep = left if side == 0 else jnp.logical_not(left)
                part = _dot(lhs, jnp.where(keep, rhs, 0.0))
                y_pair = part if y_pair is None else y_pair + part
            y_parts.append(y_pair)
    y = jnp.concatenate(y_parts, axis=1) + drow_ref[...] * xs[:t_in]
    y_ref[0] = y.astype(y_ref.dtype)

    to_end = jnp.exp(cum_last - cum) * dt
    toxs = _dot(to_end, e64_ref[...]) * xs
    dec = _exact_dot(jnp.broadcast_to(jnp.exp(cum_last), (SUBLANES, SSM_HEADS)), e64_ref[...])[0:1, :]
    new_parts = []
    for g in range(SSM_GROUPS):
        gs = slice(g * SSM_STATE, (g + 1) * SSM_STATE)
        hs = slice(g * GROUP_WIDTH, (g + 1) * GROUP_WIDTH)
        new_parts.append(_dot(bm[:, gs].T, toxs[:, hs]))
    new_state = state * dec + jnp.concatenate(new_parts, axis=1)
    state_scr[...] = new_state

    @pl.when(c == pl.num_programs(1) - 1)
    def _():
        st_ref[0] = new_state.T


def ssd(xbc, dt, dtt, wts, init, q, t_in, tvalid, y_dtype):
    bsz, t, _ = xbc.shape
    nc = max(t // q, 1)
    nxb = SSM_INNER // (SSM_GROUPS * SSM_STATE)
    in_specs = [pl.BlockSpec((1, t_in, SSM_INNER), lambda b, c: (b, c, 0)),
                pl.BlockSpec((1, t_in, SSM_GROUPS * SSM_STATE), lambda b, c: (b, c, nxb)),
                pl.BlockSpec((1, t_in, SSM_GROUPS * SSM_STATE), lambda b, c: (b, c, nxb + 1)),
                pl.BlockSpec((1, q, SSM_HEADS), lambda b, c: (b, c, 0)),
                pl.BlockSpec((SSM_HEADS, q), lambda b, c: (0, b * nc + c))]
    consts = (wts["arow"], wts["acol"], wts["drow"], wts["e64"], wts["tri"], wts["trit"])
    in_specs += [_full(w.shape) for w in consts]
    args = [xbc, xbc, xbc, dt, dtt, *consts]
    if init is not None:
        in_specs.append(pl.BlockSpec((1, SSM_INNER, SSM_STATE), lambda b, c: (b, 0, 0)))
        args.append(init)
    return pl.pallas_call(
        functools.partial(_ssd_kernel, q=q, t_in=t_in, tvalid=tvalid, has_init=init is not None),
        grid=(bsz, nc),
        in_specs=in_specs,
        out_specs=(pl.BlockSpec((1, t_in, SSM_INNER), lambda b, c: (b, c, 0)),
                   pl.BlockSpec((1, SSM_INNER, SSM_STATE), lambda b, c: (b, 0, 0))),
        out_shape=(jax.ShapeDtypeStruct((bsz, t, SSM_INNER), y_dtype),
                   jax.ShapeDtypeStruct((bsz, SSM_INNER, SSM_STATE), F32)),
        scratch_shapes=[pltpu.VMEM((SSM_STATE, SSM_INNER), F32)],
        compiler_params=_params("parallel", "arbitrary"),
        name="ssd",
    )(*args)


def _odd_out_kernel(x_ref, y_ref, sz_ref, gn_ref, wout_ref, gpost_ref, o_ref):
    v = y_ref[...].astype(F32) * sz_ref[...].astype(F32)
    gn = gn_ref[...]
    parts = []
    for g in range(SSM_GROUPS):
        gs = slice(g * GROUP_WIDTH, (g + 1) * GROUP_WIDTH)
        parts.append(_rms(v[:, gs], gn[:, gs]).astype(BF16))
    y = _dot(jnp.concatenate(parts, axis=1), wout_ref[...])
    o_ref[...] = x_ref[...] + _rms(y, gpost_ref[...])


def odd_out(x, y, sz, wts, tm):
    n = x.shape[0]
    row = lambda d: pl.BlockSpec((tm, d), lambda i: (i, 0))
    weights = (wts["gnorm"], wts["wout"], wts["gpost"])
    return pl.pallas_call(
        _odd_out_kernel,
        grid=(n // tm,),
        in_specs=[row(D_MODEL), row(SSM_INNER), row(SSM_INNER)] + [_full(w.shape) for w in weights],
        out_specs=row(D_MODEL),
        out_shape=jax.ShapeDtypeStruct((n, D_MODEL), F32),
        compiler_params=_params("parallel"),
        name="odd_out",
    )(x, y, sz, *weights)


def _even_weights(i, l, p):
    w_in = p["e_w_in"][l]
    c = CONV_WIDTH
    o_q, o_kv, o_bg = 3 * c, 3 * c + Q_LORA, 3 * c + Q_LORA + KV_LORA + ROPE_DIM
    half = ROPE_DIM // 2
    w_kr = w_in[:, o_kv + KV_LORA:o_bg]
    w_uq = p["e_w_uq"][l]
    uq_r = w_uq[:, :, NOPE_DIM:]
    w_uv = jnp.transpose(p["e_w_uv"][l], (1, 0, 2))
    eye = jnp.eye(MLA_HEADS, dtype=F32)
    wbd = (eye[:, None, :, None] * w_uv[:, :, None, :]).reshape(MLA_HEADS * KV_LORA, MLA_HEADS * V_DIM)
    w_out = p["e_w_out"][l]
    return {
        "gpre": p["norm_pre_mix"][i][None], "gpost": p["norm_post_mix"][i][None],
        "wa": w_in[:, :o_q].astype(BF16), "wq": w_in[:, o_q:o_kv].astype(BF16),
        "wkv": jnp.concatenate([w_in[:, o_kv:o_bg], -w_kr[:, half:], w_kr[:, :half]], axis=1).astype(BF16),
        "wbg": w_in[:, o_bg:].astype(BF16),
        "gq": p["e_q_norm"][l][None], "gkv": p["e_kv_norm"][l][None],
        "wuqn": w_uq[:, :, :NOPE_DIM].reshape(Q_LORA, MLA_HEADS * NOPE_DIM).astype(BF16),
        "wuqr": uq_r.reshape(Q_LORA, MLA_HEADS * ROPE_DIM).astype(BF16),
        "wuqrot": jnp.concatenate([-uq_r[:, :, half:], uq_r[:, :, :half]], axis=2)
        .reshape(Q_LORA, MLA_HEADS * ROPE_DIM).astype(BF16),
        "wukt": jnp.transpose(p["e_w_uk"][l], (1, 2, 0)).astype(BF16),
        "conv_w": p["e_conv_w"][l], "conv_b": p["e_conv_b"][l][None],
        "ln_w": p["e_ln_w"][l][None], "ln_b": p["e_ln_b"][l][None],
        "wbd": wbd.astype(BF16), "woa": w_out[:c].astype(BF16), "wob": w_out[c:].astype(BF16),
    }


def _odd_weights(i, l, p):
    w_in = p["o_w_in"][l]
    w_dt = w_in[:, SSM_INNER + XBC_DIM:]
    a = -jnp.exp(p["o_a_log"][l])
    q = SSM_CHUNK
    tri = jnp.tril(jnp.ones((q, q), F32))
    return {
        "gpre": p["norm_pre_mix"][i][None], "gpost": p["norm_post_mix"][i][None],
        "wz": w_in[:, :SSM_INNER].astype(BF16),
        "wxbc": w_in[:, SSM_INNER:SSM_INNER + XBC_DIM].astype(BF16),
        "wdt": w_dt.astype(BF16), "wdtt": w_dt.T.astype(BF16),
        "brow": p["o_dt_bias"][l][None], "bcol": p["o_dt_bias"][l][:, None],
        "conv_w": p["o_conv_w"][l], "conv_b": p["o_conv_b"][l][None],
        "arow": a[None], "acol": a[:, None],
        "drow": jnp.repeat(p["o_d"][l], SSM_HEADDIM)[None],
        "e64": jnp.repeat(jnp.eye(SSM_HEADS, dtype=F32), SSM_HEADDIM, axis=1).astype(BF16),
        "tri": tri.astype(BF16), "trit": tri.T.astype(BF16),
        "gnorm": p["o_norm"][l][None], "wout": p["o_w_out"][l].astype(BF16),
    }


def _xattn_weights(i, p):
    return {"gprex": p["norm_pre_x"][i][None], "gpostx": p["norm_post_x"][i][None],
            "wqg": p["x_w_qg"][i].astype(BF16), "wo": p["x_w_o"][i].astype(BF16)}


def kernel(x_prompt, x_sample, mem_prompt, cache_ckv, cache_krope, page_table, state_conv_a,
           state_conv_c, state_ssm, cache_mem_k, cache_mem_v, norm_pre_mix, norm_post_mix,
           norm_pre_x, norm_post_x, norm_mem, x_w_qg, x_w_k, x_w_v, x_w_o, e_w_in, e_conv_w,
           e_conv_b, e_ln_w, e_ln_b, e_q_norm, e_kv_norm, e_w_uq, e_w_uk, e_w_uv, e_w_out,
           o_w_in, o_conv_w, o_conv_b, o_dt_bias, o_a_log, o_d, o_norm, o_w_out):
    p = dict(norm_pre_mix=norm_pre_mix, norm_post_mix=norm_post_mix, norm_pre_x=norm_pre_x,
             norm_post_x=norm_post_x, e_w_in=e_w_in, e_conv_w=e_conv_w, e_conv_b=e_conv_b,
             e_ln_w=e_ln_w, e_ln_b=e_ln_b, e_q_norm=e_q_norm, e_kv_norm=e_kv_norm, e_w_uq=e_w_uq,
             e_w_uk=e_w_uk, e_w_uv=e_w_uv, e_w_out=e_w_out, o_w_in=o_w_in, o_conv_w=o_conv_w,
             o_conv_b=o_conv_b, o_dt_bias=o_dt_bias, o_a_log=o_a_log, o_d=o_d, o_norm=o_norm,
             o_w_out=o_w_out, x_w_qg=x_w_qg, x_w_o=x_w_o)
    bp, sp, _ = x_prompt.shape
    bs, ss, _ = x_sample.shape
    depth = norm_pre_mix.shape[0]
    n_pages, page = page_table.shape[1], cache_ckv.shape[1]
    past_len = n_pages * page
    tpad = SUBLANES
    assert ss <= tpad and sp % SSM_CHUNK == 0
    np_, ns = bp * sp, bs * tpad
    tm_p, tm_s = 512, ns
    nb_s = 8 if bs % 8 == 0 else 1

    half = ROPE_DIM // 2
    inv = ROPE_THETA ** (-jnp.arange(half, dtype=F32) / half)
    inv_lanes = jnp.tile(inv, LANES // half)[None]
    cos_p, sin_p = rope_table(inv_lanes, sp, 0, sp)
    cos_s, sin_s = rope_table(inv_lanes, ns, past_len, tpad)

    xp = x_prompt.reshape(np_, D_MODEL)
    xs = jnp.pad(x_sample, ((0, 0), (0, tpad - ss), (0, 0))).reshape(ns, D_MODEL)
    mem_flat = mem_prompt.reshape(bp * MEM_LEN, D_MODEL)
    mk, mv, mk4, mv4 = mem_kv(mem_flat, norm_mem[:, None], x_w_k.astype(BF16), x_w_v.astype(BF16), 512)
    mem_k_p = mk.reshape(depth, bp, MEM_LEN, D_MODEL)
    mem_v_p = mv.reshape(depth, bp, MEM_LEN, D_MODEL)

    outs = {k: [] for k in ("ckv_p", "ckv_s", "kr_p", "kr_s", "ca_p", "ca_s", "cc_p", "cc_s",
                            "ssm_p", "ssm_s", "mk", "mv")}
    for i in range(depth):
        l = i // 2
        if i % 2 == 0:
            w = _even_weights(i, l, p)
            glu, sga, sgb, qcat, kcat, ckv, kr = even_in(xp, w, cos_p, sin_p, tm_p)
            glu3 = glu.reshape(bp, sp, CONV_WIDTH)
            ya = causal_conv(glu3, jnp.zeros((bp, CONV_A_HALO, CONV_WIDTH), F32), w["conv_w"], w["conv_b"],
                             512, CONV_A_HALO, CONV_WIDTH,
                             ln=(w["ln_w"], w["ln_b"], sga.reshape(bp, sp, CONV_WIDTH)), out_dtype=BF16)
            olat = mla_prompt(qcat, kcat.reshape(bp, sp, QK_CAT), bp, sp, 256)
            yp = even_out(xp, ya.reshape(np_, CONV_WIDTH), olat, sgb, w, tm_p)
            outs["ca_p"].append(glu3[:, sp - (CONV_K - 1):])
            outs["ckv_p"].append(ckv.reshape(bp, sp, KV_LORA))
            outs["kr_p"].append(kr.reshape(bp, sp, ROPE_DIM))
            glu, sga, sgb, qcat, kcat, ckv, kr = even_in(xs, w, cos_s, sin_s, tm_s)
            glu3 = glu.reshape(bs, tpad, CONV_WIDTH)
            hist = state_conv_a[l]
            hist_pad = jnp.pad(hist, ((0, 0), (CONV_A_HALO - (CONV_K - 1), 0), (0, 0)))
            ya = causal_conv(glu3, hist_pad, w["conv_w"], w["conv_b"], tpad, CONV_A_HALO, CONV_WIDTH,
                             ln=(w["ln_w"], w["ln_b"], sga.reshape(bs, tpad, CONV_WIDTH)), nb=nb_s)
            knew =jnp.pad(kcat.reshape(bs, tpad, QK_CAT), ((0, 0), (0, LANES - tpad), (0, 0)))
            olat = mla_sample(qcat, knew, cache_ckv[:, :, l, :],
                              jnp.transpose(cache_krope[:, :, l, :], (0, 2, 1)), page_table, tpad, ss,
                              min(128, n_pages))
            ys = even_out(xs, ya.reshape(ns, CONV_WIDTH), olat, sgb, w, tm_s)
            outs["ca_s"].append(jnp.concatenate([hist[:, ss:], glu3[:, :ss]], axis=1))
            outs["ckv_s"].append(ckv.reshape(bs, tpad, KV_LORA)[:, :ss])
            outs["kr_s"].append(kr.reshape(bs, tpad, ROPE_DIM)[:, :ss])
        else:
            w = _odd_weights(i, l, p)
            q = SSM_CHUNK
            sz, xbc, dt, dtt = odd_in(xp, w, 256)
            xbc3 = xbc.reshape(bp, sp, XBC_DIM)
            xbc_c = causal_conv(xbc3, jnp.zeros((bp, CONV_C_HALO, XBC_DIM), F32), w["conv_w"], w["conv_b"],
                                512, CONV_C_HALO, 512)
            y, st = ssd(xbc_c, dt.reshape(bp, sp, SSM_HEADS), dtt, w, None, q, q, q, BF16)
            yp = odd_out(xp, y.reshape(np_, SSM_INNER), sz, w, tm_p)
            outs["cc_p"].append(xbc3[:, sp - (SSM_CONV - 1):])
            outs["ssm_p"].append(st.reshape(bp, SSM_HEADS, SSM_HEADDIM, SSM_STATE))
            sz, xbc, dt, dtt = odd_in(xs, w, tm_s)
            xbc3 = xbc.reshape(bs, tpad, XBC_DIM)
            hist = state_conv_c[l]
            hist_pad = jnp.pad(hist, ((0, 0), (CONV_C_HALO - (SSM_CONV - 1), 0), (0, 0)))
            xbc_c = causal_conv(xbc3, hist_pad, w["conv_w"], w["conv_b"], tpad, CONV_C_HALO, 512, nb=nb_s)
            dt_pad = jnp.pad(dt.reshape(bs, tpad, SSM_HEADS), ((0, 0), (0, q - tpad), (0, 0)))
            dtt_pad = jnp.pad(dtt.reshape(SSM_HEADS, bs, tpad), ((0, 0), (0, 0), (0, q - tpad)))
            y, st = ssd(xbc_c, dt_pad, dtt_pad.reshape(SSM_HEADS, bs * q), w,
                        state_ssm[l].reshape(bs, SSM_INNER, SSM_STATE), q, tpad, ss, F32)
            ys = odd_out(xs, y.reshape(ns, SSM_INNER), sz, w, tm_s)
            outs["cc_s"].append(jnp.concatenate([hist[:, ss:], xbc3[:, :ss]], axis=1)[:, -(SSM_CONV - 1):])
            outs["ssm_s"].append(st.reshape(bs, SSM_HEADS, SSM_HEADDIM, SSM_STATE))
        xp, xs = yp, ys
        wx = _xattn_weights(i, p)
        xp = xattn(xp.reshape(bp, sp, D_MODEL), mem_k_p, mem_v_p, i, wx, 1, 512).reshape(np_, D_MODEL)
        xs = xattn_cached(xs.reshape(bs, tpad, D_MODEL), cache_mem_k, cache_mem_v, i, wx, 4).reshape(ns, D_MODEL)

    return (xp.reshape(bp, sp, D_MODEL), xs.reshape(bs, tpad, D_MODEL)[:, :ss],
            jnp.stack(outs["ckv_p"], axis=2), jnp.stack(outs["ckv_s"], axis=2),
            jnp.stack(outs["kr_p"], axis=2), jnp.stack(outs["kr_s"], axis=2),
            jnp.stack(outs["ca_p"]), jnp.stack(outs["ca_s"]),
            jnp.stack(outs["cc_p"]), jnp.stack(outs["cc_s"]),
            jnp.stack(outs["ssm_p"]), jnp.stack(outs["ssm_s"]),
            mk4.reshape(depth, bp, MEM_LEN, X_HEADS, X_HEAD_DIM),
            mv4.reshape(depth, bp, MEM_LEN, X_HEADS, X_HEAD_DIM))
```
